```python
import math
import jax, jax.numpy as jnp
from jax import lax
import numpy as np

D_MODEL = 1024
BATCH = 2
SEQ = 8192
DEPTH = 2
DEC_BATCH = 32
DEC_SEQ = 64
PAST_LEN = 2048

CHUNK = 64
Q_BLOCK = 128
EPS = 1e-6
N_BRANCH = 4

SSD_HEADS = 8
SSD_HEAD_DIM = 64
SSD_INNER = SSD_HEADS * SSD_HEAD_DIM
SSD_GROUPS = 2
SSD_STATE = 128
SSD_CONV = 4
SSD_CONV_DIM = SSD_INNER + 2 * SSD_GROUPS * SSD_STATE

MLA_HEADS = 8
MLA_Q_RANK = 384
MLA_KV_RANK = 256
MLA_NOPE = 64
MLA_ROPE = 32
MLA_V = 64
MLA_WIDTH = MLA_HEADS * MLA_V
ROPE_THETA = 10000.0

SCONV_WIDTH = 512
SCONV_K = 3

BAND_HEADS = 8
BAND_HEAD_DIM = 64
BAND_WIDTH = BAND_HEADS * BAND_HEAD_DIM
BAND_PAST_CHUNKS = 8
BAND_CHUNKS = BAND_PAST_CHUNKS + 1
REL_CLIP = 128

FF_HIDDEN = ((8 * D_MODEL // 3 + 255) // 256) * 256

IN_SPLITS = (SSD_INNER, SSD_INNER, SSD_GROUPS * SSD_STATE, SSD_GROUPS * SSD_STATE, SSD_HEADS,
             MLA_Q_RANK, MLA_KV_RANK, MLA_ROPE,
             SCONV_WIDTH, SCONV_WIDTH, SCONV_WIDTH,
             BAND_WIDTH, BAND_WIDTH, BAND_WIDTH,
             N_BRANCH * D_MODEL)
IN_COLS = sum(IN_SPLITS)

kernel_name = 'hybrid_streaming_encoder_step'


def rms_norm(x, g):
    xf = x.astype(jnp.float32)
    y = xf * lax.rsqrt(jnp.mean(xf * xf, axis=-1, keepdims=True) + EPS)
    return (y * g.astype(jnp.float32)).astype(x.dtype)


def causal_dwconv(x, prev, w):
    width = w.shape[0]
    t = x.shape[1]
    xp = jnp.concatenate([prev.astype(x.dtype), x], axis=1)
    y = xp[:, 0:t] * w[0]
    for i in range(1, width):
        y = y + xp[:, i:i + t] * w[i]
    return y, xp[:, xp.shape[1] - (width - 1):]


def rope_tables(pos):
    inv = ROPE_THETA ** (-jnp.arange(0, MLA_ROPE, 2, dtype=jnp.float32) / MLA_ROPE)
    ang = pos.astype(jnp.float32)[:, None] * inv[None, :]
    return jnp.cos(ang), jnp.sin(ang)


def apply_rope(x, cos, sin):
    half = x.shape[-1] // 2
    x1, x2 = x[..., :half], x[..., half:]
    cos = cos.astype(x.dtype)
    sin = sin.astype(x.dtype)
    return jnp.concatenate([x1 * cos - x2 * sin, x1 * sin + x2 * cos], axis=-1)


def ssd_scan(x, da, bm, cm, h0, chunk):
    f32 = jnp.float32
    bsz, t, nh, hp = x.shape
    ng, ns = bm.shape[-2:]
    nr = nh // ng
    nc = t // chunk
    xc = x.astype(f32).reshape(bsz, nc, chunk, ng, nr, hp)
    bc = bm.astype(f32).reshape(bsz, nc, chunk, ng, ns)
    cc = cm.astype(f32).reshape(bsz, nc, chunk, ng, ns)
    a_cs = jnp.cumsum(da.astype(f32).reshape(bsz, nc, chunk, ng, nr).transpose(0, 3, 4, 1, 2), axis=-1)
    tril = jnp.tril(jnp.ones((chunk, chunk), bool))
    seg = a_cs[..., :, None] - a_cs[..., None, :]
    decay_in = jnp.where(tril, jnp.exp(jnp.where(tril, seg, 0.0)), 0.0)
    cb = jnp.einsum('bclgn,bcsgn->bgcls', cc, bc)
    y_diag = jnp.einsum('bgrcls,bcsgrp->bclgrp', cb[:, :, None] * decay_in, xc)
    decay_to_end = jnp.exp(a_cs[..., -1:] - a_cs)
    chunk_states = jnp.einsum('bclgn,bgrcl,bclgrp->cbgrpn', bc, decay_to_end, xc)
    chunk_decay = jnp.exp(a_cs[..., -1]).transpose(3, 0, 1, 2)

    def step(h, inp):
        dec, st = inp
        return dec[..., None, None] * h + st, h

    h_fin, h_in = lax.scan(step, h0.astype(f32).reshape(bsz, ng, nr, hp, ns), (chunk_decay, chunk_states))
    y_off = jnp.einsum('bclgn,cbgrpn,bgrcl->bclgrp', cc, h_in, jnp.exp(a_cs))
    y = (y_diag + y_off).reshape(bsz, t, nh, hp)
    return y, h_fin.reshape(bsz, nh, hp, ns).astype(h0.dtype)


def ssd_branch(a_z, a_x, a_b, a_c, a_dt, ssm_prev, conv_prev, prm):
    bsz, t, _ = a_x.shape
    gn = SSD_GROUPS * SSD_STATE
    xbc, conv_new = causal_dwconv(jnp.concatenate([a_x, a_b, a_c], axis=-1), conv_prev, prm['ssd_conv_w'])
    xbc = jax.nn.silu(xbc + prm['ssd_conv_b'])
    xs = xbc[..., :SSD_INNER].reshape(bsz, t, SSD_HEADS, SSD_HEAD_DIM)
    bm = xbc[..., SSD_INNER:SSD_INNER + gn].reshape(bsz, t, SSD_GROUPS, SSD_STATE)
    cm = xbc[..., SSD_INNER + gn:].reshape(bsz, t, SSD_GROUPS, SSD_STATE)
    dt = jax.nn.softplus(a_dt.astype(jnp.float32) + prm['ssd_dt_bias'].astype(jnp.float32))
    a = -jnp.exp(prm['ssd_a_log'].astype(jnp.float32))
    y, ssm_new = ssd_scan(xs.astype(jnp.float32) * dt[..., None], dt * a, bm, cm, ssm_prev, min(CHUNK, t))
    y = y.astype(a_x.dtype) + prm['ssd_d'][:, None] * xs
    y = rms_norm(y.reshape(bsz, t, SSD_INNER) * jax.nn.silu(a_z), prm['ssd_norm_g'])
    return y @ prm['w_a_out'], ssm_new, conv_new


def mla_attend(q_nope, q_pe, k_nope, k_pe, v, q_pos, k_pos):
    bsz, t, nh, _ = q_nope.shape
    scale = (MLA_NOPE + MLA_ROPE) ** -0.5
    k_chunk = k_pos // CHUNK

    def block(args):
        qn, qp, qpos = args
        s = jnp.einsum('bqhd,bkhd->bhqk', qn, k_nope) + jnp.einsum('bqhr,bkr->bhqk', qp, k_pe)
        s = s.astype(jnp.float32) * scale
        mask = k_chunk[None, :] <= (qpos // CHUNK)[:, None]
        p = jax.nn.softmax(jnp.where(mask, s, -jnp.inf), axis=-1).astype(v.dtype)
        return jnp.einsum('bhqk,bkhd->bqhd', p, v)

    if t <= Q_BLOCK:
        return block((q_nope, q_pe, q_pos))
    nb = t // Q_BLOCK

    def to_blocks(a):
        return a.reshape(bsz, nb, Q_BLOCK, *a.shape[2:]).swapaxes(0, 1)

    out = lax.map(block, (to_blocks(q_nope), to_blocks(q_pe), q_pos.reshape(nb, Q_BLOCK)))
    return out.swapaxes(0, 1).reshape(bsz, t, nh, MLA_V)


def mla_branch(b_ql, b_kvl, b_kpe, pos, ckv_prev, kpe_prev, prm):
    bsz, t, _ = b_ql.shape
    cos, sin = rope_tables(pos)
    q = (rms_norm(b_ql, prm['mla_q_norm_g']) @ prm['mla_w_q_up']).reshape(bsz, t, MLA_HEADS, MLA_NOPE + MLA_ROPE)
    q_nope = rms_norm(q[..., :MLA_NOPE], prm['mla_qn_g'])
    q_pe = apply_rope(rms_norm(q[..., MLA_NOPE:], prm['mla_qr_g']), cos[:, None], sin[:, None])
    ckv = rms_norm(b_kvl, prm['mla_kv_norm_g'])
    kpe = apply_rope(rms_norm(b_kpe, prm['mla_kr_g']), cos, sin)
    if ckv_prev is None:
        ckv_all, kpe_all, k_pos = ckv, kpe, pos
    else:
        ckv_all = jnp.concatenate([ckv_prev, ckv], axis=1)
        kpe_all = jnp.concatenate([kpe_prev, kpe], axis=1)
        k_pos = jnp.arange(ckv_all.shape[1], dtype=jnp.int32)
    kv = (ckv_all @ prm['mla_w_kv_up']).reshape(bsz, ckv_all.shape[1], MLA_HEADS, MLA_NOPE + MLA_V)
    k_nope = rms_norm(kv[..., :MLA_NOPE], prm['mla_kn_g'])
    v = kv[..., MLA_NOPE:]
    o = mla_attend(q_nope, q_pe, k_nope, kpe_all, v, pos, k_pos)
    return o.reshape(bsz, t, MLA_WIDTH) @ prm['w_b_out'], ckv, kpe


def sconv_branch(c_b, c_c, c_x, prev, prm):
    u = c_c * c_x
    uc, new = causal_dwconv(u, prev, prm['sconv_w'])
    return (c_b * uc) @ prm['w_c_out'], new


def band_attend(q, k, v, rel, valid, rel_bias):
    idx = jnp.clip(rel, -REL_CLIP, REL_CLIP) + REL_CLIP
    bias = rel_bias[:, idx].astype(jnp.float32)
    s = jnp.einsum('bclhd,bcshd->bchls', q, k).astype(jnp.float32) * (BAND_HEAD_DIM ** -0.5) + bias[None, None]
    s = jnp.where(valid[None, :, None, None, :], s, -jnp.inf)
    p = jax.nn.softmax(s, axis=-1).astype(v.dtype)
    return jnp.einsum('bchls,bcshd->bclhd', p, v)


def band_branch(d_q, d_k, d_v, pos, k_prev, v_prev, prm):
    bsz, t, _ = d_q.shape
    shp = (bsz, t, BAND_HEADS, BAND_HEAD_DIM)
    q = rms_norm(d_q.reshape(shp), prm['band_qn_g'])
    k = rms_norm(d_k.reshape(shp), prm['band_kn_g'])
    v = d_v.reshape(shp)
    span = BAND_CHUNKS * CHUNK
    if k_prev is None:
        nc = t // CHUNK
        cshp = (bsz, nc, CHUNK, BAND_HEADS, BAND_HEAD_DIM)
        pad = jnp.zeros((bsz, BAND_PAST_CHUNKS, CHUNK, BAND_HEADS, BAND_HEAD_DIM), k.dtype)
        idx = jnp.arange(nc)[:, None] + jnp.arange(BAND_CHUNKS)[None, :]
        bshp = (bsz, nc, span, BAND_HEADS, BAND_HEAD_DIM)
        k_band = jnp.concatenate([pad, k.reshape(cshp)], axis=1)[:, idx].reshape(bshp)
        v_band = jnp.concatenate([pad, v.reshape(cshp)], axis=1)[:, idx].reshape(bshp)
        rel = jnp.arange(CHUNK)[:, None] - jnp.arange(span)[None, :] + BAND_PAST_CHUNKS * CHUNK
        valid = ((jnp.arange(nc)[:, None] - BAND_PAST_CHUNKS) * CHUNK + jnp.arange(span)[None, :]) >= 0
        o = band_attend(q.reshape(cshp), k_band, v_band, rel, valid, prm['band_rel_bias']).reshape(bsz, t, BAND_WIDTH)
        rows = min(BAND_PAST_CHUNKS * CHUNK, t)
        k_new, v_new = k[:, t - rows:], v[:, t - rows:]
    else:
        rows = k_prev.shape[1]
        k_all = jnp.concatenate([k_prev, k], axis=1)
        v_all = jnp.concatenate([v_prev, v], axis=1)
        k_pos = jnp.concatenate([pos[0] - rows + jnp.arange(rows, dtype=jnp.int32), pos])
        rel = pos[:, None] - k_pos[None, :]
        valid = jnp.ones((1, rows + t), bool)
        o = band_attend(q[:, None], k_all[:, None], v_all[:, None], rel, valid, prm['band_rel_bias'])[:, 0]
        o = o.reshape(bsz, t, BAND_WIDTH)
        k_new, v_new = k_all[:, t:], v_all[:, t:]
    return o @ prm['w_d_out'], k_new, v_new


def hybrid_layer(x, pos, past, prm):
    bsz, t, _ = x.shape
    h = rms_norm(x, prm['norm_mix_g'])
    proj = h @ prm['w_in']
    (a_z, a_x, a_b, a_c, a_dt, b_ql, b_kvl, b_kpe, c_b, c_c, c_x, d_q, d_k, d_v, g_logit) = jnp.split(
        proj, np.cumsum(IN_SPLITS)[:-1].tolist(), axis=-1)
    out_a, ssm_new, ssd_conv_new = ssd_branch(a_z, a_x, a_b, a_c, a_dt, past['ssm'], past['ssd_conv'], prm)
    out_b, ckv_new, kpe_new = mla_branch(b_ql, b_kvl, b_kpe, pos, past['mla_ckv'], past['mla_kpe'], prm)
    out_c, sconv_new = sconv_branch(c_b, c_c, c_x, past['sconv'], prm)
    out_d, bk_new, bv_new = band_branch(d_q, d_k, d_v, pos, past['band_k'], past['band_v'], prm)
    gates = jax.nn.sigmoid((g_logit + prm['b_gate']).astype(jnp.float32)).astype(x.dtype)
    gates = gates.reshape(bsz, t, N_BRANCH, D_MODEL)
    merged = gates[:, :, 0] * out_a + gates[:, :, 1] * out_b + gates[:, :, 2] * out_c + gates[:, :, 3] * out_d
    x = x + merged @ prm['w_o']
    gu = rms_norm(x, prm['norm_ffn_g']) @ prm['w_ffn_up']
    x = x + (jax.nn.silu(gu[..., :FF_HIDDEN]) * gu[..., FF_HIDDEN:]) @ prm['w_ffn_down']
    new = {'ssm': ssm_new, 'ssd_conv': ssd_conv_new, 'mla_ckv': ckv_new, 'mla_kpe': kpe_new,
           'sconv': sconv_new, 'band_k': bk_new, 'band_v': bv_new}
    return x, new


def setup_inputs(seed: int = 0) -> dict:
    key = jax.random.key(seed)
    keys = jax.random.split(key, 48)
    ctr = iter(range(48))
    L = DEPTH

    def nrm(shape, scale):
        return scale * jax.random.normal(keys[next(ctr)], shape, jnp.float32)

    def gain(shape):
        return 1.0 + nrm(shape, 0.05)

    band_rows = min(BAND_PAST_CHUNKS * CHUNK, PAST_LEN)
    dt0 = jnp.exp(jax.random.uniform(keys[next(ctr)], (L, SSD_HEADS), jnp.float32,
                                     minval=math.log(1e-3), maxval=math.log(1e-1)))
    dt_bias = dt0 + jnp.log(-jnp.expm1(-dt0))
    a_log = jnp.log(jax.random.uniform(keys[next(ctr)], (L, SSD_HEADS), jnp.float32, minval=1.0, maxval=16.0))
    return {
        'x_prompt': nrm((BATCH, SEQ, D_MODEL), 1.0),
        'x_sample': nrm((DEC_BATCH, DEC_SEQ, D_MODEL), 1.0),
        'state_ssm': nrm((L, DEC_BATCH, SSD_HEADS, SSD_HEAD_DIM, SSD_STATE), 0.5),
        'state_ssd_conv': nrm((L, DEC_BATCH, SSD_CONV - 1, SSD_CONV_DIM), 1.0),
        'cache_mla_ckv': nrm((L, DEC_BATCH, PAST_LEN, MLA_KV_RANK), 1.0),
        'cache_mla_kpe': nrm((L, DEC_BATCH, PAST_LEN, MLA_ROPE), 1.0),
        'state_sconv': nrm((L, DEC_BATCH, SCONV_K - 1, SCONV_WIDTH), 1.0),
        'cache_band_k': nrm((L, DEC_BATCH, band_rows, BAND_HEADS, BAND_HEAD_DIM), 1.0),
        'cache_band_v': nrm((L, DEC_BATCH, band_rows, BAND_HEADS, BAND_HEAD_DIM), 1.0),
        'norm_mix_g': gain((L, D_MODEL)),
        'w_in': nrm((L, D_MODEL, IN_COLS), D_MODEL ** -0.5),
        'b_gate': nrm((L, N_BRANCH * D_MODEL), 0.1),
        'ssd_conv_w': nrm((L, SSD_CONV, SSD_CONV_DIM), SSD_CONV ** -0.5),
        'ssd_conv_b': nrm((L, SSD_CONV_DIM), 0.02),
        'ssd_dt_bias': dt_bias,
        'ssd_a_log': a_log,
        'ssd_d': 1.0 + nrm((L, SSD_HEADS), 0.1),
        'ssd_norm_g': gain((L, SSD_INNER)),
        'w_a_out': nrm((L, SSD_INNER, D_MODEL), SSD_INNER ** -0.5),
        'mla_q_norm_g': gain((L, MLA_Q_RANK)),
        'mla_w_q_up': nrm((L, MLA_Q_RANK, MLA_HEADS * (MLA_NOPE + MLA_ROPE)), MLA_Q_RANK ** -0.5),
        'mla_kv_norm_g': gain((L, MLA_KV_RANK)),
        'mla_w_kv_up': nrm((L, MLA_KV_RANK, MLA_HEADS * (MLA_NOPE + MLA_V)), MLA_KV_RANK ** -0.5),
        'mla_qn_g': gain((L, MLA_NOPE)),
        'mla_kn_g': gain((L, MLA_NOPE)),
        'mla_qr_g': gain((L, MLA_ROPE)),
        'mla_kr_g': gain((L, MLA_ROPE)),
        'w_b_out': nrm((L, MLA_WIDTH, D_MODEL), MLA_WIDTH ** -0.5),
        'sconv_w': nrm((L, SCONV_K, SCONV_WIDTH), SCONV_K ** -0.5),
        'w_c_out': nrm((L, SCONV_WIDTH, D_MODEL), SCONV_WIDTH ** -0.5),
        'band_qn_g': gain((L, BAND_HEAD_DIM)),
        'band_kn_g': gain((L, BAND_HEAD_DIM)),
        'band_rel_bias': nrm((L, BAND_HEADS, 2 * REL_CLIP + 1), 0.1),
        'w_d_out': nrm((L, BAND_WIDTH, D_MODEL), BAND_WIDTH ** -0.5),
        'w_o': nrm((L, D_MODEL, D_MODEL), D_MODEL ** -0.5),
        'norm_ffn_g': gain((L, D_MODEL)),
        'w_ffn_up': nrm((L, D_MODEL, 2 * FF_HIDDEN), D_MODEL ** -0.5),
        'w_ffn_down': nrm((L, FF_HIDDEN, D_MODEL), FF_HIDDEN ** -0.5),
    }


def reference(x_prompt, x_sample, state_ssm, state_ssd_conv, cache_mla_ckv, cache_mla_kpe, state_sconv,
              cache_band_k, cache_band_v,
              norm_mix_g, w_in, b_gate,
              ssd_conv_w, ssd_conv_b, ssd_dt_bias, ssd_a_log, ssd_d, ssd_norm_g, w_a_out,
              mla_q_norm_g, mla_w_q_up, mla_kv_norm_g, mla_w_kv_up, mla_qn_g, mla_kn_g, mla_qr_g, mla_kr_g, w_b_out,
              sconv_w, w_c_out,
              band_qn_g, band_kn_g, band_rel_bias, w_d_out,
              w_o, norm_ffn_g, w_ffn_up, w_ffn_down):
    b_p, t_p, _ = x_prompt.shape
    past_len = cache_mla_ckv.shape[2]
    pos_p = jnp.arange(t_p, dtype=jnp.int32)
    pos_s = past_len + jnp.arange(x_sample.shape[1], dtype=jnp.int32)
    y_p, y_s = x_prompt, x_sample
    new_p, new_s = [], []
    for l in range(DEPTH):
        prm = {
            'norm_mix_g': norm_mix_g[l], 'w_in': w_in[l], 'b_gate': b_gate[l],
            'ssd_conv_w': ssd_conv_w[l], 'ssd_conv_b': ssd_conv_b[l], 'ssd_dt_bias': ssd_dt_bias[l],
            'ssd_a_log': ssd_a_log[l], 'ssd_d': ssd_d[l], 'ssd_norm_g': ssd_norm_g[l], 'w_a_out': w_a_out[l],
            'mla_q_norm_g': mla_q_norm_g[l], 'mla_w_q_up': mla_w_q_up[l], 'mla_kv_norm_g': mla_kv_norm_g[l],
            'mla_w_kv_up': mla_w_kv_up[l], 'mla_qn_g': mla_qn_g[l], 'mla_kn_g': mla_kn_g[l],
            'mla_qr_g': mla_qr_g[l], 'mla_kr_g': mla_kr_g[l], 'w_b_out': w_b_out[l],
            'sconv_w': sconv_w[l], 'w_c_out': w_c_out[l],
            'band_qn_g': band_qn_g[l], 'band_kn_g': band_kn_g[l], 'band_rel_bias': band_rel_bias[l],
            'w_d_out': w_d_out[l],
            'w_o': w_o[l], 'norm_ffn_g': norm_ffn_g[l], 'w_ffn_up': w_ffn_up[l], 'w_ffn_down': w_ffn_down[l],
        }
        init_p = {
            'ssm': jnp.zeros((b_p, SSD_HEADS, SSD_HEAD_DIM, SSD_STATE), x_prompt.dtype),
            'ssd_conv': jnp.zeros((b_p, SSD_CONV - 1, SSD_CONV_DIM), x_prompt.dtype),
            'sconv': jnp.zeros((b_p, SCONV_K - 1, SCONV_WIDTH), x_prompt.dtype),
            'mla_ckv': None, 'mla_kpe': None, 'band_k': None, 'band_v': None,
        }
        y_p, st_p = hybrid_layer(y_p, pos_p, init_p, prm)
        new_p.append(st_p)
        past_s = {
            'ssm': state_ssm[l], 'ssd_conv': state_ssd_conv[l], 'sconv': state_sconv[l],
            'mla_ckv': cache_mla_ckv[l], 'mla_kpe': cache_mla_kpe[l],
            'band_k': cache_band_k[l], 'band_v': cache_band_v[l],
        }
        y_s, st_s = hybrid_layer(y_s, pos_s, past_s, prm)
        new_s.append(st_s)

    def stack(states, name):
        return jnp.stack([s[name] for s in states], axis=0)

    ssm_p, ssm_s = stack(new_p, 'ssm'), stack(new_s, 'ssm')
    ssd_conv_p, ssd_conv_s = stack(new_p, 'ssd_conv'), stack(new_s, 'ssd_conv')
    mla_ckv_p, mla_ckv_s = stack(new_p, 'mla_ckv'), stack(new_s, 'mla_ckv')
    mla_kpe_p, mla_kpe_s = stack(new_p, 'mla_kpe'), stack(new_s, 'mla_kpe')
    sconv_p, sconv_s = stack(new_p, 'sconv'), stack(new_s, 'sconv')
    band_k_p, band_k_s = stack(new_p, 'band_k'), stack(new_s, 'band_k')
    band_v_p, band_v_s = stack(new_p, 'band_v'), stack(new_s, 'band_v')
    return (y_p, y_s, ssm_p, ssm_s, ssd_conv_p, ssd_conv_s, mla_ckv_p, mla_ckv_s, mla_kpe_p, mla_kpe_s,
            sconv_p, sconv_s, band_k_p, band_k_s, band_v_p, band_v_s)
```

```python
import functools
import math

import jax
import jax.numpy as jnp
from jax import lax
from jax.experimental import pallas as pl
from jax.experimental.pallas import tpu as pltpu

F32 = jnp.float32
MXU_DTYPE = jnp.bfloat16
EPS = 1e-6
NEG_BIG = -1e30

V7X_LANES = 128
V7X_VMEM_LIMIT_BYTES = 56 * 1024 * 1024

D_MODEL = 1024
CHUNK = 64
N_BRANCH = 4
SSD_HEADS, SSD_HEAD_DIM, SSD_GROUPS, SSD_STATE, SSD_CONV = 8, 64, 2, 128, 4
SSD_INNER = SSD_HEADS * SSD_HEAD_DIM
SSD_CONV_DIM = SSD_INNER + 2 * SSD_GROUPS * SSD_STATE
MLA_HEADS, MLA_Q_RANK, MLA_KV_RANK, MLA_NOPE, MLA_ROPE, MLA_V = 8, 384, 256, 64, 32, 64
MLA_WIDTH = MLA_HEADS * MLA_V
ROPE_THETA = 10000.0
SCONV_WIDTH, SCONV_K = 512, 3
BAND_HEADS, BAND_HEAD_DIM, BAND_PAST_CHUNKS, REL_CLIP = 8, 64, 8, 128
BAND_WIDTH = BAND_HEADS * BAND_HEAD_DIM
BAND_ROWS = BAND_PAST_CHUNKS * CHUNK
BAND_SPAN = BAND_ROWS + CHUNK
FF_HIDDEN = ((8 * D_MODEL // 3 + 255) // 256) * 256

SEG = {
    "gate": (0, 4096), "c_b": (4096, 512), "c_c": (4608, 512), "c_x": (5120, 512),
    "d_q": (5632, 512), "d_k": (6144, 512), "d_v": (6656, 512),
    "a_z": (7168, 512), "a_x": (7680, 512), "a_b": (8192, 256), "a_c": (8448, 256),
    "b_kvl": (8704, 256), "a_dt": (8960, 128), "b_kpe": (9088, 128), "b_ql": (9216, 384),
}
PROJ_COLS = 9600
KPE_LANE = 64


def _cparams(*sem):
    return pltpu.CompilerParams(dimension_semantics=sem, vmem_limit_bytes=V7X_VMEM_LIMIT_BYTES)


def _mm(a, b):
    return jnp.dot(a.astype(MXU_DTYPE), b.astype(MXU_DTYPE), preferred_element_type=F32)


def _mm_nt(a, b):
    return lax.dot_general(a.astype(MXU_DTYPE), b.astype(MXU_DTYPE), (((1,), (1,)), ((), ())),
                           preferred_element_type=F32)


def _mm_tn(a, b):
    return lax.dot_general(a.astype(MXU_DTYPE), b.astype(MXU_DTYPE), (((0,), (0,)), ((), ())),
                           preferred_element_type=F32)


def _split3(x):
    hi = x.astype(MXU_DTYPE)
    r1 = x - hi.astype(F32)
    mid = r1.astype(MXU_DTYPE)
    lo = (r1 - mid.astype(F32)).astype(MXU_DTYPE)
    return hi, mid, lo


def _mm_exact_lhs01(sel, x):
    hi, mid, lo = _split3(x)
    sel = sel.astype(MXU_DTYPE)
    d = functools.partial(jnp.dot, preferred_element_type=F32)
    return d(sel, hi) + d(sel, mid) + d(sel, lo)


def _mm_exact_rhs01(x, sel):
    hi, mid, lo = _split3(x)
    sel = sel.astype(MXU_DTYPE)
    d = functools.partial(jnp.dot, preferred_element_type=F32)
    return d(hi, sel) + d(mid, sel) + d(lo, sel)


def _mm_tn_exact_rhs01(x, sel):
    hi, mid, lo = _split3(x)
    sel = sel.astype(MXU_DTYPE)
    d = functools.partial(lax.dot_general, dimension_numbers=(((0,), (0,)), ((), ())), preferred_element_type=F32)
    return d(hi, sel) + d(mid, sel) + d(lo, sel)


def _silu(x):
    return x * jax.nn.sigmoid(x)


def _row_rms(x):
    return x * lax.rsqrt(jnp.mean(x * x, axis=-1, keepdims=True) + EPS)


def _lane_iota(rows):
    return lax.broadcasted_iota(jnp.int32, (rows, V7X_LANES), 1)


def _head64_rms(x):
    rows, width = x.shape
    low = _lane_iota(rows) < 64
    out = []
    for j in range(width // V7X_LANES):
        blk = x[:, j * V7X_LANES:(j + 1) * V7X_LANES]
        sq = blk * blk
        s_lo = jnp.sum(jnp.where(low, sq, 0.0), axis=-1, keepdims=True) * (1.0 / 64)
        s_hi = jnp.sum(jnp.where(low, 0.0, sq), axis=-1, keepdims=True) * (1.0 / 64)
        out.append(blk * jnp.where(low, lax.rsqrt(s_lo + EPS), lax.rsqrt(s_hi + EPS)))
    return jnp.concatenate(out, axis=-1)


def _rmsnorm_kernel(x_ref, g_ref, o_ref):
    o_ref[...] = (_row_rms(x_ref[...]) * g_ref[...]).astype(o_ref.dtype)


def rmsnorm_cast(x, g, tm):
    n, d = x.shape
    return pl.pallas_call(
        _rmsnorm_kernel, out_shape=jax.ShapeDtypeStruct((n, d), MXU_DTYPE), grid=(n // tm,),
        in_specs=[pl.BlockSpec((tm, d), lambda i: (i, 0)), pl.BlockSpec((1, d), lambda i: (0, 0))],
        out_specs=pl.BlockSpec((tm, d), lambda i: (i, 0)),
        compiler_params=_cparams("parallel"), name="rmsnorm_cast")(x, g.reshape(1, d))


def _matmul_kernel(a_ref, w_ref, o_ref):
    o_ref[...] = jnp.dot(a_ref[...], w_ref[...], preferred_element_type=F32)


def matmul(a, w, tm, tn):
    n, k = a.shape
    c = w.shape[1]
    return pl.pallas_call(
        _matmul_kernel, out_shape=jax.ShapeDtypeStruct((n, c), F32), grid=(c // tn, n // tm),
        in_specs=[pl.BlockSpec((tm, k), lambda j, i: (i, 0)), pl.BlockSpec((k, tn), lambda j, i: (0, j))],
        out_specs=pl.BlockSpec((tm, tn), lambda j, i: (i, j)),
        compiler_params=_cparams("parallel", "parallel"), name="in_proj_matmul")(a, w)


def _merge_kernel(x_ref, g_ref, ya_ref, ob_ref, yc_ref, od_ref, bg_ref, wa_ref, wb_ref, wc_ref, wd_ref, wo_ref, o_ref):
    def gate(k):
        return jax.nn.sigmoid(g_ref[:, k * D_MODEL:(k + 1) * D_MODEL] + bg_ref[:, k * D_MODEL:(k + 1) * D_MODEL])

    merged = gate(0) * _mm(ya_ref[...], wa_ref[...])
    merged = merged + gate(1) * _mm(ob_ref[...], wb_ref[...])
    merged = merged + gate(2) * _mm(yc_ref[...], wc_ref[...])
    merged = merged + gate(3) * _mm(od_ref[...], wd_ref[...])
    o_ref[...] = x_ref[...] + _mm(merged, wo_ref[...])


def merge(x, proj, ya, ob, yc, od, b_gate, wa, wb, wc, wd, wo, tm):
    n, d = x.shape
    row = lambda w: pl.BlockSpec((tm, w), lambda i: (i, 0))
    full = lambda a: pl.BlockSpec(a.shape, lambda i: (0, 0))
    bg = b_gate.reshape(1, N_BRANCH * d)
    return pl.pallas_call(
        _merge_kernel, out_shape=jax.ShapeDtypeStruct((n, d), F32), grid=(n // tm,),
        in_specs=[row(d), pl.BlockSpec((tm, N_BRANCH * d), lambda i: (i, 0)), row(512), row(512), row(512), row(512),
                  full(bg), full(wa), full(wb), full(wc), full(wd), full(wo)],
        out_specs=row(d), compiler_params=_cparams("parallel"), name="merge")(
            x, proj, ya, ob, yc, od, bg, wa, wb, wc, wd, wo)


def _ffn_kernel(x_ref, g_ref, wg_ref, wv_ref, wd_ref, o_ref, h_s, acc_s):
    k = pl.program_id(1)

    @pl.when(k == 0)
    def _():
        h_s[...] = (_row_rms(x_ref[...]) * g_ref[...]).astype(h_s.dtype)
        acc_s[...] = jnp.zeros_like(acc_s)

    h = h_s[...]
    gate = jnp.dot(h, wg_ref[...], preferred_element_type=F32)
    val = jnp.dot(h, wv_ref[...], preferred_element_type=F32)
    acc_s[...] += _mm(_silu(gate) * val, wd_ref[...])

    @pl.when(k == pl.num_programs(1) - 1)
    def _():
        o_ref[...] = x_ref[...] + acc_s[...]


def ffn(x, g, w_up, w_down, tm, tk):
    n, d = x.shape
    nk = FF_HIDDEN // tk
    return pl.pallas_call(
        _ffn_kernel, out_shape=jax.ShapeDtypeStruct((n, d), F32), grid=(n // tm, nk),
        in_specs=[pl.BlockSpec((tm, d), lambda i, k: (i, 0)), pl.BlockSpec((1, d), lambda i, k: (0, 0)),
                  pl.BlockSpec((d, tk), lambda i, k: (0, k)), pl.BlockSpec((d, tk), lambda i, k: (0, nk + k)),
                  pl.BlockSpec((tk, d), lambda i, k: (k, 0))],
        out_specs=pl.BlockSpec((tm, d), lambda i, k: (i, 0)),
        scratch_shapes=[pltpu.VMEM((tm, d), MXU_DTYPE), pltpu.VMEM((tm, d), F32)],
        compiler_params=_cparams("parallel", "arbitrary"), name="ffn")(x, g.reshape(1, d), w_up, w_up, w_down)


def _ssd_kernel(z_ref, x_ref, b_ref, c_ref, dt_ref, conv0_ref, h0_ref, cw_ref, cbias_ref, dtb_ref, alog_ref,
                dfull_ref, ng_ref, tri_ref, expand_ref, y_ref, convn_ref, hn_ref, xp_s, h_s):
    L = CHUNK
    c = pl.program_id(1)

    @pl.when(c == 0)
    def _():
        xp_s[0:8, :] = conv0_ref[0]
        h_s[...] = h0_ref[0]

    xp_s[8:8 + L, 0:512] = x_ref[...]
    xp_s[8:8 + L, 512:768] = b_ref[...]
    xp_s[8:8 + L, 768:1024] = c_ref[...]
    base = 8 - (SSD_CONV - 1)
    acc = xp_s[base:base + L, :] * cw_ref[0:1, :]
    for i in range(1, SSD_CONV):
        acc = acc + xp_s[base + i:base + i + L, :] * cw_ref[i:i + 1, :]
    xbc = _silu(acc + cbias_ref[...])
    tail = xp_s[L:L + 8, :]
    convn_ref[0] = tail
    xp_s[0:8, :] = tail

    xs = xbc[:, 0:512]
    dtr = dt_ref[...] + dtb_ref[...]
    dt = jnp.maximum(dtr, 0.0) + jnp.log1p(jnp.exp(-jnp.abs(dtr)))
    dt_full = _mm_exact_rhs01(dt, expand_ref[...])
    da_full = dt_full * (-jnp.exp(alog_ref[...]))
    acs = _mm_exact_lhs01(tri_ref[...], da_full)
    x_dt = xs * dt_full
    acs_last = acs[L - 1:L, :]
    xd_end = x_dt * jnp.exp(acs_last - acs)
    exp_acs = jnp.exp(acs)

    row = lax.broadcasted_iota(jnp.int32, (L, V7X_LANES), 0)
    lane = _lane_iota(L)
    lane_s = jnp.where(lane < 64, lane, lane - 64)
    diag2 = row == lane_s
    tril2 = lane_s <= row
    row2 = lax.broadcasted_iota(jnp.int32, (2 * L, V7X_LANES), 0)
    lane2 = lax.broadcasted_iota(jnp.int32, (2 * L, V7X_LANES), 1)
    blockdiag = (row2 < L) == (lane2 < 64)

    y_parts = []
    for g in range(SSD_GROUPS):
        bg = xbc[:, 512 + 128 * g:512 + 128 * (g + 1)]
        cg = xbc[:, 768 + 128 * g:768 + 128 * (g + 1)]
        cb2 = _mm_nt(cg, jnp.concatenate([bg, bg], axis=0))
        hg = h_s[256 * g:256 * (g + 1), :]
        y_off = _mm_nt(cg, hg) * exp_acs[:, 256 * g:256 * (g + 1)]
        for jp in range(2):
            pair = 2 * g + jp
            sl = slice(128 * pair, 128 * (pair + 1))
            blk = acs[:, sl]
            at_s = jnp.sum(jnp.where(diag2, blk, 0.0), axis=0, keepdims=True)
            decay = jnp.where(tril2, jnp.exp(jnp.minimum(blk - at_s, 0.0)), 0.0)
            xpair = x_dt[:, sl]
            xblk = jnp.where(blockdiag, jnp.concatenate([xpair, xpair], axis=0), 0.0)
            y_parts.append(_mm(cb2 * decay, xblk) + y_off[:, 128 * jp:128 * (jp + 1)])
        state = _mm_tn(xd_end[:, 256 * g:256 * (g + 1)], bg)
        decay_col = jnp.exp(_mm_tn_exact_rhs01(da_full[:, 256 * g:256 * (g + 1)], jnp.ones((L, SSD_STATE), F32)))
        h_s[256 * g:256 * (g + 1), :] = decay_col * hg + state
    y = jnp.concatenate(y_parts, axis=-1) + dfull_ref[...] * xs
    gated = y * _silu(z_ref[...])
    y_ref[...] = (_row_rms(gated) * ng_ref[...]).astype(y_ref.dtype)
    hn_ref[0] = h_s[...]


def ssd_branch(proj, bsz, t, conv_prev, ssm_prev, p):
    nc = t // CHUNK
    L = CHUNK
    n = bsz * t
    col = lambda name: SEG[name][0] // SEG[name][1]
    seg = lambda name: pl.BlockSpec((L, SEG[name][1]), lambda b, c, _j=col(name): (b * nc + c, _j))
    full = lambda a: pl.BlockSpec(a.shape, lambda b, c: (0,) * a.ndim)
    conv0 = jnp.pad(conv_prev, ((0, 0), (8 - (SSD_CONV - 1), 0), (0, 0)))
    h0 = ssm_prev.reshape(bsz, SSD_INNER, SSD_STATE)
    tri = jnp.tril(jnp.ones((L, L), F32))
    consts = [p["ssd_conv_w"], p["ssd_conv_b"], p["ssd_dt_bias"], p["ssd_a_log"], p["ssd_d"], p["ssd_norm_g"], tri,
              p["ssd_expand"]]
    y, convn, hn = pl.pallas_call(
        _ssd_kernel,
        out_shape=[jax.ShapeDtypeStruct((n, SSD_INNER), MXU_DTYPE), jax.ShapeDtypeStruct((bsz, 8, SSD_CONV_DIM), F32),
                   jax.ShapeDtypeStruct((bsz, SSD_INNER, SSD_STATE), F32)],
        grid=(bsz, nc),
        in_specs=[seg("a_z"), seg("a_x"), seg("a_b"), seg("a_c"), seg("a_dt"),
                  pl.BlockSpec((1, 8, SSD_CONV_DIM), lambda b, c: (b, 0, 0)),
                  pl.BlockSpec((1, SSD_INNER, SSD_STATE), lambda b, c: (b, 0, 0))] + [full(a) for a in consts],
        out_specs=[pl.BlockSpec((L, SSD_INNER), lambda b, c: (b * nc + c, 0)),
                   pl.BlockSpec((1, 8, SSD_CONV_DIM), lambda b, c: (b, 0, 0)),
                   pl.BlockSpec((1, SSD_INNER, SSD_STATE), lambda b, c: (b, 0, 0))],
        scratch_shapes=[pltpu.VMEM((L + 8, SSD_CONV_DIM), F32), pltpu.VMEM((SSD_INNER, SSD_STATE), F32)],
        compiler_params=_cparams("parallel", "arbitrary"), name="ssd_branch")(
            proj, proj, proj, proj, proj, conv0, h0, *consts)
    return y, hn.reshape(bsz, SSD_HEADS, SSD_HEAD_DIM, SSD_STATE), convn[:, 8 - (SSD_CONV - 1):]


def _sconv_kernel(cb_ref, cc_ref, cx_ref, st0_ref, w_ref, o_ref, stn_ref, up_s):
    rows = cb_ref.shape[0]

    @pl.when(pl.program_id(1) == 0)
    def _():
        up_s[0:8, :] = st0_ref[0]

    up_s[8:8 + rows, :] = cc_ref[...] * cx_ref[...]
    base = 8 - (SCONV_K - 1)
    uc = up_s[base:base + rows, :] * w_ref[0:1, :]
    for i in range(1, SCONV_K):
        uc = uc + up_s[base + i:base + i + rows, :] * w_ref[i:i + 1, :]
    o_ref[...] = (cb_ref[...] * uc).astype(o_ref.dtype)
    tail = up_s[rows:rows + 8, :]
    stn_ref[0] = tail
    up_s[0:8, :] = tail


def sconv_branch(proj, bsz, t, prev, w):
    rows = min(t, 512)
    nt = t // rows
    col = lambda name: SEG[name][0] // SCONV_WIDTH
    seg = lambda name: pl.BlockSpec((rows, SCONV_WIDTH), lambda b, c, _j=col(name): (b * nt + c, _j))
    st0 = jnp.pad(prev, ((0, 0), (8 - (SCONV_K - 1), 0), (0, 0)))
    o, stn = pl.pallas_call(
        _sconv_kernel,
        out_shape=[jax.ShapeDtypeStruct((bsz * t, SCONV_WIDTH), MXU_DTYPE), jax.ShapeDtypeStruct((bsz, 8, SCONV_WIDTH), F32)],
        grid=(bsz, nt),
        in_specs=[seg("c_b"), seg("c_c"), seg("c_x"), pl.BlockSpec((1, 8, SCONV_WIDTH), lambda b, c: (b, 0, 0)),
                  pl.BlockSpec(w.shape, lambda b, c: (0, 0))],
        out_specs=[pl.BlockSpec((rows, SCONV_WIDTH), lambda b, c: (b * nt + c, 0)),
                   pl.BlockSpec((1, 8, SCONV_WIDTH), lambda b, c: (b, 0, 0))],
        scratch_shapes=[pltpu.VMEM((rows + 8, SCONV_WIDTH), F32)],
        compiler_params=_cparams("parallel", "arbitrary"), name="sconv_branch")(proj, proj, proj, st0, w)
    return o, stn[:, 8 - (SCONV_K - 1):]


def _rope(y, cos_t, sin_a, sin_b):
    return y * cos_t + pltpu.roll(y, 16, 1) * sin_a + pltpu.roll(y, V7X_LANES - 16, 1) * sin_b


def _mla_prep_kernel(ql_ref, kvl_ref, kpe_ref, cos_ref, sa_ref, sb_ref, gq_ref, wq_ref, gqh_ref, gkv_ref, gkr_ref,
                     q_ref, ckv_ref, kpeo_ref):
    rows = ql_ref.shape[0]
    cos_t, sin_a, sin_b = cos_ref[...], sa_ref[...], sb_ref[...]
    q = _mm(_row_rms(ql_ref[...]) * gq_ref[...], wq_ref[...])
    lane = _lane_iota(rows)
    nope = lane < MLA_NOPE
    scale = (MLA_NOPE + MLA_ROPE) ** -0.5
    for h in range(MLA_HEADS):
        sl = slice(h * V7X_LANES, (h + 1) * V7X_LANES)
        blk = q[:, sl]
        sq = blk * blk
        s_n = jnp.sum(jnp.where(nope, sq, 0.0), axis=-1, keepdims=True) * (1.0 / MLA_NOPE)
        s_r = jnp.sum(jnp.where(nope, 0.0, sq), axis=-1, keepdims=True) * (1.0 / MLA_ROPE)
        y = blk * jnp.where(nope, lax.rsqrt(s_n + EPS), lax.rsqrt(s_r + EPS)) * gqh_ref[:, sl]
        q_ref[:, sl] = (_rope(y, cos_t, sin_a, sin_b) * scale).astype(q_ref.dtype)
    ckv_ref[...] = _row_rms(kvl_ref[...]) * gkv_ref[...]
    kp = kpe_ref[...]
    ms = jnp.sum(kp * kp, axis=-1, keepdims=True) * (1.0 / MLA_ROPE)
    kpeo_ref[...] = _rope(kp * lax.rsqrt(ms + EPS) * gkr_ref[...], cos_t, sin_a, sin_b)


def mla_prep(proj, n, tm, tabs, p):
    cos_t, sin_a, sin_b = tabs
    ntab = cos_t.shape[0] // tm
    seg = lambda name: pl.BlockSpec((tm, SEG[name][1]), lambda i, _j=SEG[name][0] // SEG[name][1]: (i, _j))
    tab = pl.BlockSpec((tm, V7X_LANES), lambda i: (i % ntab, 0))
    full = lambda a: pl.BlockSpec(a.shape, lambda i: (0, 0))
    consts = [p["mla_q_norm_g"], p["mla_wq"], p["mla_gq_head"], p["mla_kv_norm_g"], p["mla_gkr"]]
    return pl.pallas_call(
        _mla_prep_kernel,
        out_shape=[jax.ShapeDtypeStruct((n, MLA_HEADS * V7X_LANES), MXU_DTYPE),
                   jax.ShapeDtypeStruct((n, MLA_KV_RANK), F32), jax.ShapeDtypeStruct((n, V7X_LANES), F32)],
        grid=(n // tm,),
        in_specs=[seg("b_ql"), seg("b_kvl"), seg("b_kpe"), tab, tab, tab] + [full(a) for a in consts],
        out_specs=[pl.BlockSpec((tm, MLA_HEADS * V7X_LANES), lambda i: (i, 0)),
                   pl.BlockSpec((tm, MLA_KV_RANK), lambda i: (i, 0)), pl.BlockSpec((tm, V7X_LANES), lambda i: (i, 0))],
        compiler_params=_cparams("parallel"), name="mla_prep")(proj, proj, proj, cos_t, sin_a, sin_b, *consts)


def _mla_kv_kernel(ckv_ref, kpe_ref, wk_ref, wv_ref, gk_ref, k_ref, v_ref):
    c = ckv_ref[...].astype(MXU_DTYPE)
    kk = jnp.dot(c, wk_ref[...], preferred_element_type=F32)
    v_ref[...] = jnp.dot(c, wv_ref[...], preferred_element_type=F32).astype(v_ref.dtype)
    kpe = kpe_ref[...]
    for h in range(MLA_HEADS):
        sl = slice(h * V7X_LANES, (h + 1) * V7X_LANES)
        blk = kk[:, sl]
        ms = jnp.sum(blk * blk, axis=-1, keepdims=True) * (1.0 / MLA_NOPE)
        k_ref[:, sl] = (blk * lax.rsqrt(ms + EPS) * gk_ref[...] + kpe).astype(k_ref.dtype)


def mla_kv(ckv_all, kpe_all, ts, p):
    n = ckv_all.shape[0]
    full = lambda a: pl.BlockSpec(a.shape, lambda i: (0, 0))
    consts = [p["mla_wk"], p["mla_wv"], p["mla_gk"]]
    return pl.pallas_call(
        _mla_kv_kernel,
        out_shape=[jax.ShapeDtypeStruct((n, MLA_HEADS * V7X_LANES), MXU_DTYPE), jax.ShapeDtypeStruct((n, MLA_WIDTH), MXU_DTYPE)],
        grid=(n // ts,),
        in_specs=[pl.BlockSpec((ts, MLA_KV_RANK), lambda i: (i, 0)), pl.BlockSpec((ts, V7X_LANES), lambda i: (i, 0))]
        + [full(a) for a in consts],
        out_specs=[pl.BlockSpec((ts, MLA_HEADS * V7X_LANES), lambda i: (i, 0)), pl.BlockSpec((ts, MLA_WIDTH), lambda i: (i, 0))],
        compiler_params=_cparams("parallel"), name="mla_kv")(ckv_all, kpe_all, *consts)


def _mla_attn_kernel(q_ref, k_ref, v_ref, o_ref, m_s, l_s, acc_s, *, tq, tk, q_off):
    qi, sub, ki = pl.program_id(1), pl.program_id(3), pl.program_id(4)
    nk = pl.num_programs(4)

    @pl.when(ki == 0)
    def _():
        m_s[...] = jnp.full_like(m_s, NEG_BIG)
        l_s[...] = jnp.zeros_like(l_s)
        acc_s[...] = jnp.zeros_like(acc_s)

    q_start = q_off + qi * tq
    k_start = ki * tk

    def step(masked):
        s = _mm_nt(q_ref[0], k_ref[0])
        if masked:
            q_chunk = (q_start + lax.broadcasted_iota(jnp.int32, (tq, tk), 0)) // CHUNK
            k_chunk = (k_start + lax.broadcasted_iota(jnp.int32, (tq, tk), 1)) // CHUNK
            s = jnp.where(k_chunk <= q_chunk, s, NEG_BIG)
        m_prev = m_s[...]
        m_new = jnp.maximum(m_prev, jnp.max(s, axis=-1, keepdims=True))
        alpha = jnp.exp(m_prev - m_new)
        p = jnp.exp(s - m_new)
        l_s[...] = alpha * l_s[...] + jnp.sum(p, axis=-1, keepdims=True)
        acc_s[...] = alpha * acc_s[...] + _mm(p, v_ref[0])
        m_s[...] = m_new

    visible = k_start <= q_start + tq - 1
    all_visible = (k_start + tk - 1) // CHUNK <= q_start // CHUNK
    pl.when(jnp.logical_and(visible, all_visible))(lambda: step(False))
    pl.when(jnp.logical_and(visible, jnp.logical_not(all_visible)))(lambda: step(True))

    @pl.when(ki == nk - 1)
    def _():
        out = (acc_s[...] / l_s[...]).astype(o_ref.dtype)
        low = _lane_iota(tq) < MLA_V

        @pl.when(sub == 0)
        def _():
            o_ref[0] = out

        @pl.when(sub == 1)
        def _():
            o_ref[0] = jnp.where(low, o_ref[0], out)


def mla_attention(q, k, v, bsz, t, s, q_off, tq, tk):
    nq, nk = t // tq, s // tk

    def last_k(qi):
        return (q_off + qi * tq + tq - 1) // tk

    kern = functools.partial(_mla_attn_kernel, tq=tq, tk=tk, q_off=q_off)
    return pl.pallas_call(
        kern, out_shape=jax.ShapeDtypeStruct((bsz, t, MLA_WIDTH), MXU_DTYPE),
        grid=(bsz, nq, MLA_HEADS // 2, 2, nk),
        in_specs=[pl.BlockSpec((1, tq, V7X_LANES), lambda b, qi, hp, sub, ki: (b, qi, 2 * hp + sub)),
                  pl.BlockSpec((1, tk, V7X_LANES), lambda b, qi, hp, sub, ki: (b, jnp.minimum(ki, last_k(qi)), 2 * hp + sub)),
                  pl.BlockSpec((1, tk, V7X_LANES), lambda b, qi, hp, sub, ki: (b, jnp.minimum(ki, last_k(qi)), hp))],
        out_specs=pl.BlockSpec((1, tq, V7X_LANES), lambda b, qi, hp, sub, ki: (b, qi, hp)),
        scratch_shapes=[pltpu.VMEM((tq, 1), F32), pltpu.VMEM((tq, 1), F32), pltpu.VMEM((tq, V7X_LANES), F32)],
        compiler_params=_cparams("parallel", "parallel", "parallel", "arbitrary", "arbitrary"),
        name="mla_attention")(q, k, v)


def _band_kernel(q_ref, ka_ref, kb_ref, va_ref, vb_ref, bias_ref, gq_ref, gk_ref, o_ref, kn_ref, q_s, k_s, v_s,
                 *, a_is_cache):
    tb = q_ref.shape[0]
    i = pl.program_id(1)
    q_s[...] = (_head64_rms(q_ref[...]) * gq_ref[...] * (BAND_HEAD_DIM ** -0.5)).astype(q_s.dtype)
    kb = _head64_rms(kb_ref[...]) * gk_ref[...]
    kn_ref[...] = kb
    ka = ka_ref[...] if a_is_cache else _head64_rms(ka_ref[...]) * gk_ref[...]
    k_s[0:BAND_ROWS, :] = ka.astype(k_s.dtype)
    k_s[BAND_ROWS:BAND_ROWS + tb, :] = kb.astype(k_s.dtype)
    v_s[0:BAND_ROWS, :] = va_ref[...].astype(v_s.dtype)
    v_s[BAND_ROWS:BAND_ROWS + tb, :] = vb_ref[...].astype(v_s.dtype)
    low = _lane_iota(CHUNK) < BAND_HEAD_DIM
    win_row = lax.broadcasted_iota(jnp.int32, (1, BAND_SPAN), 1)

    def chunk(j, carry):
        r0 = pl.multiple_of(j * CHUNK, CHUNK)
        qc = q_s[pl.ds(r0, CHUNK), :]
        kw = k_s[pl.ds(r0, BAND_SPAN), :]
        vw = v_s[pl.ds(r0, BAND_SPAN), :]
        if a_is_cache:
            hole = jnp.zeros((1, BAND_SPAN), F32)
        else:
            hole = jnp.where(jnp.logical_and(i == 0, win_row + r0 < BAND_ROWS), NEG_BIG, 0.0)
        for hp in range(BAND_HEADS // 2):
            sl = slice(hp * V7X_LANES, (hp + 1) * V7X_LANES)
            qp, kp, vp = qc[:, sl], kw[:, sl], vw[:, sl]
            outs = []
            for sub in range(2):
                qm = jnp.where(low if sub == 0 else jnp.logical_not(low), qp, jnp.zeros_like(qp))
                s = _mm_nt(qm, kp) + bias_ref[2 * hp + sub] + hole
                e = jnp.exp(s - jnp.max(s, axis=-1, keepdims=True))
                outs.append(_mm(e / jnp.sum(e, axis=-1, keepdims=True), vp))
            o_ref[pl.ds(r0, CHUNK), sl] = jnp.where(low, outs[0], outs[1]).astype(o_ref.dtype)
        return carry

    lax.fori_loop(0, tb // CHUNK, chunk, 0)


def band_branch(q_src, k_prev, v_prev, bsz, t, bias, gq, gk, a_is_cache):
    tb = min(t, BAND_ROWS)
    nt = t // tb
    col = lambda name: SEG[name][0] // BAND_WIDTH
    cur = lambda name: pl.BlockSpec((tb, BAND_WIDTH), lambda b, i, _j=col(name): (b * nt + i, _j))
    if a_is_cache:
        prev = lambda name: pl.BlockSpec((BAND_ROWS, BAND_WIDTH), lambda b, i: (b, 0))
        ka, va = k_prev, v_prev
    else:
        prev = lambda name: pl.BlockSpec((BAND_ROWS, BAND_WIDTH), lambda b, i, _j=col(name): (b * nt + jnp.maximum(i - 1, 0), _j))
        ka, va = q_src, q_src
    full = lambda a: pl.BlockSpec(a.shape, lambda b, i: (0,) * a.ndim)
    kern = functools.partial(_band_kernel, a_is_cache=a_is_cache)
    return pl.pallas_call(
        kern,
        out_shape=[jax.ShapeDtypeStruct((bsz * t, BAND_WIDTH), MXU_DTYPE), jax.ShapeDtypeStruct((bsz * t, BAND_WIDTH), F32)],
        grid=(bsz, nt),
        in_specs=[cur("d_q"), prev("d_k"), cur("d_k"), prev("d_v"), cur("d_v"), full(bias), full(gq), full(gk)],
        out_specs=[pl.BlockSpec((tb, BAND_WIDTH), lambda b, i: (b * nt + i, 0)),
                   pl.BlockSpec((tb, BAND_WIDTH), lambda b, i: (b * nt + i, 0))],
        scratch_shapes=[pltpu.VMEM((tb, BAND_WIDTH), MXU_DTYPE), pltpu.VMEM((BAND_ROWS + tb, BAND_WIDTH), MXU_DTYPE),
                        pltpu.VMEM((BAND_ROWS + tb, BAND_WIDTH), MXU_DTYPE)],
        compiler_params=_cparams("parallel", "parallel"), name="band_branch")(
            q_src, ka, q_src, va, q_src, bias, gq, gk)


def _head_blocks(w, per_head, used):
    k = w.shape[0]
    w = w.reshape(k, -1, per_head)[:, :, :used]
    return jnp.pad(w, ((0, 0), (0, 0), (0, V7X_LANES - used))).reshape(k, -1)


def _lane_pad(v, offset, width=V7X_LANES):
    return jnp.pad(v, (offset, width - offset - v.shape[0])).reshape(1, width)


def _layer_params(l, w):
    src = {}
    off = 0
    for name, width in (("a_z", 512), ("a_x", 512), ("a_b", 256), ("a_c", 256), ("a_dt", 8), ("b_ql", 384),
                        ("b_kvl", 256), ("b_kpe", 32), ("c_b", 512), ("c_c", 512), ("c_x", 512),
                        ("d_q", 512), ("d_k", 512), ("d_v", 512), ("gate", 4096)):
        src[name] = (off, width)
        off += width
    w_in = w["w_in"][l]
    pieces = []
    for name, (dst, dwidth) in sorted(SEG.items(), key=lambda kv: kv[1][0]):
        s0, sw = src[name]
        lead = KPE_LANE if name == "b_kpe" else 0
        pieces.append(jnp.pad(w_in[:, s0:s0 + sw], ((0, 0), (lead, dwidth - sw - lead))))
    p = {"w_in": jnp.concatenate(pieces, axis=1).astype(MXU_DTYPE)}
    cast = lambda a: a.astype(MXU_DTYPE)
    row = lambda a: a.reshape(1, -1)
    p["norm_mix_g"], p["norm_ffn_g"], p["b_gate"] = w["norm_mix_g"][l], w["norm_ffn_g"][l], w["b_gate"][l]
    for name in ("w_a_out", "w_b_out", "w_c_out", "w_d_out", "w_o", "w_ffn_up", "w_ffn_down"):
        p[name] = cast(w[name][l])
    p["ssd_conv_w"] = w["ssd_conv_w"][l]
    p["ssd_conv_b"] = row(w["ssd_conv_b"][l])
    p["ssd_dt_bias"] = _lane_pad(w["ssd_dt_bias"][l], 0)
    p["ssd_a_log"] = row(jnp.repeat(w["ssd_a_log"][l], SSD_HEAD_DIM))
    p["ssd_d"] = row(jnp.repeat(w["ssd_d"][l], SSD_HEAD_DIM))
    p["ssd_norm_g"] = row(w["ssd_norm_g"][l])
    head_of_lane = jnp.arange(SSD_INNER) // SSD_HEAD_DIM
    p["ssd_expand"] = (jnp.arange(V7X_LANES)[:, None] == head_of_lane[None, :]).astype(F32)
    p["mla_q_norm_g"] = row(w["mla_q_norm_g"][l])
    p["mla_wq"] = cast(_head_blocks(w["mla_w_q_up"][l], MLA_NOPE + MLA_ROPE, MLA_NOPE + MLA_ROPE))
    gqh = jnp.concatenate([w["mla_qn_g"][l], w["mla_qr_g"][l], jnp.zeros((V7X_LANES - MLA_NOPE - MLA_ROPE,), F32)])
    p["mla_gq_head"] = row(jnp.tile(gqh, MLA_HEADS))
    p["mla_kv_norm_g"] = row(w["mla_kv_norm_g"][l])
    p["mla_gkr"] = _lane_pad(w["mla_kr_g"][l], KPE_LANE)
    p["mla_wk"] = cast(_head_blocks(w["mla_w_kv_up"][l], MLA_NOPE + MLA_V, MLA_NOPE))
    p["mla_wv"] = cast(w["mla_w_kv_up"][l].reshape(MLA_KV_RANK, MLA_HEADS, MLA_NOPE + MLA_V)[:, :, MLA_NOPE:]
                       .reshape(MLA_KV_RANK, MLA_WIDTH))
    p["mla_gk"] = _lane_pad(w["mla_kn_g"][l], 0)
    p["sconv_w"] = w["sconv_w"][l]
    p["band_gq"] = row(jnp.tile(w["band_qn_g"][l], BAND_HEADS))
    p["band_gk"] = row(jnp.tile(w["band_kn_g"][l], BAND_HEADS))
    rel = jnp.arange(CHUNK)[:, None] - jnp.arange(BAND_SPAN)[None, :] + BAND_ROWS
    p["band_bias"] = w["band_rel_bias"][l][:, jnp.clip(rel, -REL_CLIP, REL_CLIP) + REL_CLIP]
    return p


def _rope_tables(pos, rows):
    inv = ROPE_THETA ** (-jnp.arange(0, MLA_ROPE, 2, dtype=F32) / MLA_ROPE)
    ang = pos.astype(F32)[:, None] * inv[None, :]
    cos, sin = jnp.cos(ang), jnp.sin(ang)
    half = MLA_ROPE // 2
    z = lambda w: jnp.zeros((pos.shape[0], w), F32)
    tail = V7X_LANES - MLA_NOPE - MLA_ROPE
    cos_t = jnp.concatenate([jnp.ones((pos.shape[0], MLA_NOPE), F32), cos, cos, z(tail)], axis=1)
    sin_a = jnp.concatenate([z(MLA_NOPE + half), sin, z(tail)], axis=1)
    sin_b = jnp.concatenate([z(MLA_NOPE), -sin, z(half + tail)], axis=1)
    reps = max(rows // pos.shape[0], 1)
    return tuple(jnp.tile(a, (reps, 1)) for a in (cos_t, sin_a, sin_b))


def _layer(x, bsz, t, past, p, tabs, q_off):
    n = bsz * t
    tm = min(n, 512)
    h = rmsnorm_cast(x, p["norm_mix_g"], tm)
    proj = matmul(h, p["w_in"], min(n, 1024), 1920)
    ya, ssm_new, ssd_conv_new = ssd_branch(proj, bsz, t, past["ssd_conv"], past["ssm"], p)
    yc, sconv_new = sconv_branch(proj, bsz, t, past["sconv"], p["sconv_w"])
    q, ckv, kpe_pad = mla_prep(proj, n, tm, tabs, p)
    is_sample = past["mla_ckv"] is not None
    if is_sample:
        plen = past["mla_ckv"].shape[1]
        s = plen + t
        ckv_all = jnp.concatenate([past["mla_ckv"], ckv.reshape(bsz, t, MLA_KV_RANK)], axis=1).reshape(bsz * s, MLA_KV_RANK)
        kpe_prev = jnp.pad(past["mla_kpe"], ((0, 0), (0, 0), (KPE_LANE, V7X_LANES - KPE_LANE - MLA_ROPE)))
        kpe_all = jnp.concatenate([kpe_prev, kpe_pad.reshape(bsz, t, V7X_LANES)], axis=1).reshape(bsz * s, V7X_LANES)
        tq, tk = t, s
    else:
        s, ckv_all, kpe_all = t, ckv, kpe_pad
        tq, tk = min(t, 1024), min(t, 512)
    ts = 512 if (bsz * s) % 512 == 0 else s
    kk, vv = mla_kv(ckv_all, kpe_all, ts, p)
    ob = mla_attention(q.reshape(bsz, t, -1), kk.reshape(bsz, s, -1), vv.reshape(bsz, s, -1), bsz, t, s, q_off, tq, tk)
    ob = ob.reshape(n, MLA_WIDTH)
    if is_sample:
        kprev = past["band_k"].reshape(bsz * BAND_ROWS, BAND_WIDTH)
        vprev = past["band_v"].reshape(bsz * BAND_ROWS, BAND_WIDTH)
        od, kn = band_branch(proj, kprev, vprev, bsz, t, p["band_bias"], p["band_gq"], p["band_gk"], True)
    else:
        od, kn = band_branch(proj, None, None, bsz, t, p["band_bias"], p["band_gq"], p["band_gk"], False)
    x = merge(x, proj, ya, ob, yc, od, p["b_gate"], p["w_a_out"], p["w_b_out"], p["w_c_out"], p["w_d_out"], p["w_o"], tm)
    x = ffn(x, p["norm_ffn_g"], p["w_ffn_up"], p["w_ffn_down"], tm, FF_HIDDEN // 2)
    dv0 = SEG["d_v"][0]
    kn = kn.reshape(bsz, t, BAND_HEADS, BAND_HEAD_DIM)
    vn = proj[:, dv0:dv0 + BAND_WIDTH].reshape(bsz, t, BAND_HEADS, BAND_HEAD_DIM)
    if is_sample:
        band_k = jnp.concatenate([past["band_k"], kn], axis=1)[:, t:]
        band_v = jnp.concatenate([past["band_v"], vn], axis=1)[:, t:]
    else:
        keep = min(BAND_ROWS, t)
        band_k, band_v = kn[:, t - keep:], vn[:, t - keep:]
    new = {"ssm": ssm_new, "ssd_conv": ssd_conv_new, "sconv": sconv_new,
           "mla_ckv": ckv.reshape(bsz, t, MLA_KV_RANK),
           "mla_kpe": kpe_pad[:, KPE_LANE:KPE_LANE + MLA_ROPE].reshape(bsz, t, MLA_ROPE),
           "band_k": band_k, "band_v": band_v}
    return x, new


def kernel(x_prompt, x_sample, state_ssm, state_ssd_conv, cache_mla_ckv, cache_mla_kpe, state_sconv, cache_band_k, cache_band_v, norm_mix_g, w_in, b_gate, ssd_conv_w, ssd_conv_b, ssd_dt_bias, ssd_a_log, ssd_d, ssd_norm_g, w_a_out, mla_q_norm_g, mla_w_q_up, mla_kv_norm_g, mla_w_kv_up, mla_qn_g, mla_kn_g, mla_qr_g, mla_kr_g, w_b_out, sconv_w, w_c_out, band_qn_g, band_kn_g, band_rel_bias, w_d_out, w_o, norm_ffn_g, w_ffn_up, w_ffn_down):
    weights = dict(norm_mix_g=norm_mix_g, w_in=w_in, b_gate=b_gate, ssd_conv_w=ssd_conv_w, ssd_conv_b=ssd_conv_b,
                   ssd_dt_bias=ssd_dt_bias, ssd_a_log=ssd_a_log, ssd_d=ssd_d, ssd_norm_g=ssd_norm_g, w_a_out=w_a_out,
                   mla_q_norm_g=mla_q_norm_g, mla_w_q_up=mla_w_q_up, mla_kv_norm_g=mla_kv_norm_g,
                   mla_w_kv_up=mla_w_kv_up, mla_qn_g=mla_qn_g, mla_kn_g=mla_kn_g, mla_qr_g=mla_qr_g,
                   mla_kr_g=mla_kr_g, w_b_out=w_b_out, sconv_w=sconv_w, w_c_out=w_c_out, band_qn_g=band_qn_g,
                   band_kn_g=band_kn_g, band_rel_bias=band_rel_bias, w_d_out=w_d_out, w_o=w_o,
                   norm_ffn_g=norm_ffn_g, w_ffn_up=w_ffn_up, w_ffn_down=w_ffn_down)
    depth = w_in.shape[0]
    b_p, t_p, d = x_prompt.shape
    b_s, t_s, _ = x_sample.shape
    past_len = cache_mla_ckv.shape[2]
    assert d == D_MODEL and t_p % BAND_ROWS == 0 and t_s == CHUNK and cache_band_k.shape[2] == BAND_ROWS
    assert past_len % CHUNK == 0
    tabs_p = _rope_tables(jnp.arange(t_p, dtype=jnp.int32), min(b_p * t_p, 512))
    tabs_s = _rope_tables(past_len + jnp.arange(t_s, dtype=jnp.int32), min(b_s * t_s, 512))
    y_p = x_prompt.reshape(b_p * t_p, d)
    y_s = x_sample.reshape(b_s * t_s, d)
    new_p, new_s = [], []
    for l in range(depth):
        p = _layer_params(l, weights)
        init_p = {"ssm": jnp.zeros((b_p, SSD_HEADS, SSD_HEAD_DIM, SSD_STATE), F32),
                  "ssd_conv": jnp.zeros((b_p, SSD_CONV - 1, SSD_CONV_DIM), F32),
                  "sconv": jnp.zeros((b_p, SCONV_K - 1, SCONV_WIDTH), F32),
                  "mla_ckv": None, "mla_kpe": None, "band_k": None, "band_v": None}
        y_p, st_p = _layer(y_p, b_p, t_p, init_p, p, tabs_p, 0)
        new_p.append(st_p)
        past_s = {"ssm": state_ssm[l], "ssd_conv": state_ssd_conv[l], "sconv": state_sconv[l],
                  "mla_ckv": cache_mla_ckv[l], "mla_kpe": cache_mla_kpe[l],
                  "band_k": cache_band_k[l], "band_v": cache_band_v[l]}
        y_s, st_s = _layer(y_s, b_s, t_s, past_s, p, tabs_s, past_len)
        new_s.append(st_s)

    def stack(states, name):
        return jnp.stack([s[name] for s in states], axis=0)

    out = [y_p.reshape(b_p, t_p, d), y_s.reshape(b_s, t_s, d)]
    for name in ("ssm", "ssd_conv", "mla_ckv", "mla_kpe", "sconv", "band_k", "band_v"):
        out += [stack(new_p, name), stack(new_s, name)]
    return tuple(out)
```

```python
import functools
import math

import jax
import jax.numpy as jnp
from jax import lax
from jax.experimental import pallas as pl
from jax.experimental.pallas import tpu as pltpu

F32 = jnp.float32
MXU_DTYPE = jnp.bfloat16
EPS = 1e-6
NEG_BIG = -1e30

V7X_LANES = 128
V7X_VMEM_LIMIT_BYTES = 56 * 1024 * 1024

D_MODEL = 1024
CHUNK = 64
N_BRANCH = 4
SSD_HEADS, SSD_HEAD_DIM, SSD_GROUPS, SSD_STATE, SSD_CONV = 8, 64, 2, 128, 4
SSD_INNER = SSD_HEADS * SSD_HEAD_DIM
SSD_CONV_DIM = SSD_INNER + 2 * SSD_GROUPS * SSD_STATE
MLA_HEADS, MLA_Q_RANK, MLA_KV_RANK, MLA_NOPE, MLA_ROPE, MLA_V = 8, 384, 256, 64, 32, 64
MLA_WIDTH = MLA_HEADS * MLA_V
ROPE_THETA = 10000.0
SCONV_WIDTH, SCONV_K = 512, 3
BAND_HEADS, BAND_HEAD_DIM, BAND_PAST_CHUNKS, REL_CLIP = 8, 64, 8, 128
BAND_WIDTH = BAND_HEADS * BAND_HEAD_DIM
BAND_ROWS = BAND_PAST_CHUNKS * CHUNK
BAND_SPAN = BAND_ROWS + CHUNK
FF_HIDDEN = ((8 * D_MODEL // 3 + 255) // 256) * 256

SEG = {
    "gate": (0, 4096), "c_b": (4096, 512), "c_c": (4608, 512), "c_x": (5120, 512),
    "d_q": (5632, 512), "d_k": (6144, 512), "d_v": (6656, 512),
    "a_z": (7168, 512), "a_x": (7680, 512), "a_b": (8192, 256), "a_c": (8448, 256),
    "b_kvl": (8704, 256), "a_dt": (8960, 128), "b_kpe": (9088, 128), "b_ql": (9216, 384),
}
PROJ_COLS = 9600
KPE_LANE = 64


def _cparams(*sem):
    return pltpu.CompilerParams(dimension_semantics=sem, vmem_limit_bytes=V7X_VMEM_LIMIT_BYTES)


def _mm(a, b):
    return jnp.dot(a.astype(MXU_DTYPE), b.astype(MXU_DTYPE), preferred_element_type=F32)


def _mm_nt(a, b):
    return lax.dot_general(a.astype(MXU_DTYPE), b.astype(MXU_DTYPE), (((1,), (1,)), ((), ())),
                           preferred_element_type=F32)


def _mm_tn(a, b):
    return lax.dot_general(a.astype(MXU_DTYPE), b.astype(MXU_DTYPE), (((0,), (0,)), ((), ())),
                           preferred_element_type=F32)


def _split3(x):
    hi = x.astype(MXU_DTYPE)
    r1 = x - hi.astype(F32)
    mid = r1.astype(MXU_DTYPE)
    lo = (r1 - mid.astype(F32)).astype(MXU_DTYPE)
    return hi, mid, lo


def _mm_exact_lhs01(sel, x):
    hi, mid, lo = _split3(x)
    sel = sel.astype(MXU_DTYPE)
    d = functools.partial(jnp.dot, preferred_element_type=F32)
    return d(sel, hi) + d(sel, mid) + d(sel, lo)


def _mm_exact_rhs01(x, sel):
    hi, mid, lo = _split3(x)
    sel = sel.astype(MXU_DTYPE)
    d = functools.partial(jnp.dot, preferred_element_type=F32)
    return d(hi, sel) + d(mid, sel) + d(lo, sel)


def _mm_tn_exact_rhs01(x, sel):
    hi, mid, lo = _split3(x)
    sel = sel.astype(MXU_DTYPE)
    d = functools.partial(lax.dot_general, dimension_numbers=(((0,), (0,)), ((), ())), preferred_element_type=F32)
    return d(hi, sel) + d(mid, sel) + d(lo, sel)


def _silu(x):
    return x * jax.nn.sigmoid(x)


def _row_rms(x):
    return x * lax.rsqrt(jnp.mean(x * x, axis=-1, keepdims=True) + EPS)


def _lane_iota(rows):
    return lax.broadcasted_iota(jnp.int32, (rows, V7X_LANES), 1)


def _head64_rms(x):
    rows, width = x.shape
    low = _lane_iota(rows) < 64
    out = []
    for j in range(width // V7X_LANES):
        blk = x[:, j * V7X_LANES:(j + 1) * V7X_LANES]
        sq = blk * blk
        s_lo = jnp.sum(jnp.where(low, sq, 0.0), axis=-1, keepdims=True) * (1.0 / 64)
        s_hi = jnp.sum(jnp.where(low, 0.0, sq), axis=-1, keepdims=True) * (1.0 / 64)
        out.append(blk * jnp.where(low, lax.rsqrt(s_lo + EPS), lax.rsqrt(s_hi + EPS)))
    return jnp.concatenate(out, axis=-1)


def _rmsnorm_kernel(x_ref, g_ref, o_ref):
    o_ref[...] = (_row_rms(x_ref[...]) * g_ref[...]).astype(o_ref.dtype)


def rmsnorm_cast(x, g, tm):
    n, d = x.shape
    return pl.pallas_call(
        _rmsnorm_kernel, out_shape=jax.ShapeDtypeStruct((n, d), MXU_DTYPE), grid=(n // tm,),
        in_specs=[pl.BlockSpec((tm, d), lambda i: (i, 0)), pl.BlockSpec((1, d), lambda i: (0, 0))],
        out_specs=pl.BlockSpec((tm, d), lambda i: (i, 0)),
        compiler_params=_cparams("parallel"), name="rmsnorm_cast")(x, g.reshape(1, d))


def _matmul_kernel(a_ref, w_ref, o_ref):
    o_ref[...] = jnp.dot(a_ref[...], w_ref[...], preferred_element_type=F32)


def matmul(a, w, tm, tn):
    n, k = a.shape
    c = w.shape[1]
    return pl.pallas_call(
        _matmul_kernel, out_shape=jax.ShapeDtypeStruct((n, c), F32), grid=(c // tn, n // tm),
        in_specs=[pl.BlockSpec((tm, k), lambda j, i: (i, 0)), pl.BlockSpec((k, tn), lambda j, i: (0, j))],
        out_specs=pl.BlockSpec((tm, tn), lambda j, i: (i, j)),
        compiler_params=_cparams("parallel", "parallel"), name="in_proj_matmul")(a, w)


def _merge_kernel(x_ref, g_ref, ya_ref, ob_ref, yc_ref, od_ref, bg_ref, wa_ref, wb_ref, wc_ref, wd_ref, wo_ref, o_ref):
    def gate(k):
        return jax.nn.sigmoid(g_ref[:, k * D_MODEL:(k + 1) * D_MODEL] + bg_ref[:, k * D_MODEL:(k + 1) * D_MODEL])

    merged = gate(0) * _mm(ya_ref[...], wa_ref[...])
    merged = merged + gate(1) * _mm(ob_ref[...], wb_ref[...])
    merged = merged + gate(2) * _mm(yc_ref[...], wc_ref[...])
    merged = merged + gate(3) * _mm(od_ref[...], wd_ref[...])
    o_ref[...] = x_ref[...] + _mm(merged, wo_ref[...])


def merge(x, proj, ya, ob, yc, od, b_gate, wa, wb, wc, wd, wo, tm):
    n, d = x.shape
    row = lambda w: pl.BlockSpec((tm, w), lambda i: (i, 0))
    full = lambda a: pl.BlockSpec(a.shape, lambda i: (0, 0))
    bg = b_gate.reshape(1, N_BRANCH * d)
    return pl.pallas_call(
        _merge_kernel, out_shape=jax.ShapeDtypeStruct((n, d), F32), grid=(n // tm,),
        in_specs=[row(d), pl.BlockSpec((tm, N_BRANCH * d), lambda i: (i, 0)), row(512), row(512), row(512), row(512),
                  full(bg), full(wa), full(wb), full(wc), full(wd), full(wo)],
        out_specs=row(d), compiler_params=_cparams("parallel"), name="merge")(
            x, proj, ya, ob, yc, od, bg, wa, wb, wc, wd, wo)


def _ffn_kernel(x_ref, g_ref, wg_ref, wv_ref, wd_ref, o_ref, h_s, acc_s):
    k = pl.program_id(1)

    @pl.when(k == 0)
    def _():
        h_s[...] = (_row_rms(x_ref[...]) * g_ref[...]).astype(h_s.dtype)
        acc_s[...] = jnp.zeros_like(acc_s)

    h = h_s[...]
    gate = jnp.dot(h, wg_ref[...], preferred_element_type=F32)
    val = jnp.dot(h, wv_ref[...], preferred_element_type=F32)
    acc_s[...] += _mm(_silu(gate) * val, wd_ref[...])

    @pl.when(k == pl.num_programs(1) - 1)
    def _():
        o_ref[...] = x_ref[...] + acc_s[...]


def ffn(x, g, w_up, w_down, tm, tk):
    n, d = x.shape
    nk = FF_HIDDEN // tk
    return pl.pallas_call(
        _ffn_kernel, out_shape=jax.ShapeDtypeStruct((n, d), F32), grid=(n // tm, nk),
        in_specs=[pl.BlockSpec((tm, d), lambda i, k: (i, 0)), pl.BlockSpec((1, d), lambda i, k: (0, 0)),
                  pl.BlockSpec((d, tk), lambda i, k: (0, k)), pl.BlockSpec((d, tk), lambda i, k: (0, nk + k)),
                  pl.BlockSpec((tk, d), lambda i, k: (k, 0))],
        out_specs=pl.BlockSpec((tm, d), lambda i, k: (i, 0)),
        scratch_shapes=[pltpu.VMEM((tm, d), MXU_DTYPE), pltpu.VMEM((tm, d), F32)],
        compiler_params=_cparams("parallel", "arbitrary"), name="ffn")(x, g.reshape(1, d), w_up, w_up, w_down)


def _ssd_kernel(z_ref, x_ref, b_ref, c_ref, dt_ref, conv0_ref, h0_ref, cw_ref, cbias_ref, dtb_ref, alog_ref,
                dfull_ref, ng_ref, tri_ref, expand_ref, y_ref, convn_ref, hn_ref, xp_s, h_s):
    L = CHUNK
    c = pl.program_id(1)

    @pl.when(c == 0)
    def _():
        xp_s[0:8, :] = conv0_ref[0]
        h_s[...] = h0_ref[0]

    xp_s[8:8 + L, 0:512] = x_ref[...]
    xp_s[8:8 + L, 512:768] = b_ref[...]
    xp_s[8:8 + L, 768:1024] = c_ref[...]
    base = 8 - (SSD_CONV - 1)
    acc = xp_s[base:base + L, :] * cw_ref[0:1, :]
    for i in range(1, SSD_CONV):
        acc = acc + xp_s[base + i:base + i + L, :] * cw_ref[i:i + 1, :]
    xbc = _silu(acc + cbias_ref[...])
    tail = xp_s[L:L + 8, :]
    convn_ref[0] = tail
    xp_s[0:8, :] = tail

    xs = xbc[:, 0:512]
    dtr = dt_ref[...] + dtb_ref[...]
    dt = jnp.maximum(dtr, 0.0) + jnp.log1p(jnp.exp(-jnp.abs(dtr)))
    dt_full = _mm_exact_rhs01(dt, expand_ref[...])
    da_full = dt_full * (-jnp.exp(alog_ref[...]))
    acs = _mm_exact_lhs01(tri_ref[...], da_full)
    x_dt = xs * dt_full
    acs_last = acs[L - 1:L, :]
    xd_end = x_dt * jnp.exp(acs_last - acs)
    exp_acs = jnp.exp(acs)

    row = lax.broadcasted_iota(jnp.int32, (L, V7X_LANES), 0)
    lane = _lane_iota(L)
    lane_s = jnp.where(lane < 64, lane, lane - 64)
    diag2 = row == lane_s
    tril2 = lane_s <= row
    row2 = lax.broadcasted_iota(jnp.int32, (2 * L, V7X_LANES), 0)
    lane2 = lax.broadcasted_iota(jnp.int32, (2 * L, V7X_LANES), 1)
    blockdiag = (row2 < L) == (lane2 < 64)

    y_parts = []
    for g in range(SSD_GROUPS):
        bg = xbc[:, 512 + 128 * g:512 + 128 * (g + 1)]
        cg = xbc[:, 768 + 128 * g:768 + 128 * (g + 1)]
        cb2 = _mm_nt(cg, jnp.concatenate([bg, bg], axis=0))
        hg = h_s[256 * g:256 * (g + 1), :]
        y_off = _mm_nt(cg, hg) * exp_acs[:, 256 * g:256 * (g + 1)]
        for jp in range(2):
            pair = 2 * g + jp
            sl = slice(128 * pair, 128 * (pair + 1))
            blk = acs[:, sl]
            at_s = jnp.sum(jnp.where(diag2, blk, 0.0), axis=0, keepdims=True)
            decay = jnp.where(tril2, jnp.exp(jnp.minimum(blk - at_s, 0.0)), 0.0)
            xpair = x_dt[:, sl]
            xblk = jnp.where(blockdiag, jnp.concatenate([xpair, xpair], axis=0), 0.0)
            y_parts.append(_mm(cb2 * decay, xblk) + y_off[:, 128 * jp:128 * (jp + 1)])
        state = _mm_tn(xd_end[:, 256 * g:256 * (g + 1)], bg)
        decay_col = jnp.exp(_mm_tn_exact_rhs01(da_full[:, 256 * g:256 * (g + 1)], jnp.ones((L, SSD_STATE), F32)))
        h_s[256 * g:256 * (g + 1), :] = decay_col * hg + state
    y = jnp.concatenate(y_parts, axis=-1) + dfull_ref[...] * xs
    gated = y * _silu(z_ref[...])
    y_ref[...] = (_row_rms(gated) * ng_ref[...]).astype(y_ref.dtype)
    hn_ref[0] = h_s[...]


def ssd_branch(proj, bsz, t, conv_prev, ssm_prev, p):
    nc = t // CHUNK
    L = CHUNK
    n = bsz * t
    col = lambda name: SEG[name][0] // SEG[name][1]
    seg = lambda name: pl.BlockSpec((L, SEG[name][1]), lambda b, c, _j=col(name): (b * nc + c, _j))
    full = lambda a: pl.BlockSpec(a.shape, lambda b, c: (0,) * a.ndim)
    conv0 = jnp.pad(conv_prev, ((0, 0), (8 - (SSD_CONV - 1), 0), (0, 0)))
    h0 = ssm_prev.reshape(bsz, SSD_INNER, SSD_STATE)
    tri = jnp.tril(jnp.ones((L, L), F32))
    consts = [p["ssd_conv_w"], p["ssd_conv_b"], p["ssd_dt_bias"], p["ssd_a_log"], p["ssd_d"], p["ssd_norm_g"], tri,
              p["ssd_expand"]]
    y, convn, hn = pl.pallas_call(
        _ssd_kernel,
        out_shape=[jax.ShapeDtypeStruct((n, SSD_INNER), MXU_DTYPE), jax.ShapeDtypeStruct((bsz, 8, SSD_CONV_DIM), F32),
                   jax.ShapeDtypeStruct((bsz, SSD_INNER, SSD_STATE), F32)],
        grid=(bsz, nc),
        in_specs=[seg("a_z"), seg("a_x"), seg("a_b"), seg("a_c"), seg("a_dt"),
                  pl.BlockSpec((1, 8, SSD_CONV_DIM), lambda b, c: (b, 0, 0)),
                  pl.BlockSpec((1, SSD_INNER, SSD_STATE), lambda b, c: (b, 0, 0))] + [full(a) for a in consts],
        out_specs=[pl.BlockSpec((L, SSD_INNER), lambda b, c: (b * nc + c, 0)),
                   pl.BlockSpec((1, 8, SSD_CONV_DIM), lambda b, c: (b, 0, 0)),
                   pl.BlockSpec((1, SSD_INNER, SSD_STATE), lambda b, c: (b, 0, 0))],
        scratch_shapes=[pltpu.VMEM((L + 8, SSD_CONV_DIM), F32), pltpu.VMEM((SSD_INNER, SSD_STATE), F32)],
        compiler_params=_cparams("parallel", "arbitrary"), name="ssd_branch")(
            proj, proj, proj, proj, proj, conv0, h0, *consts)
    return y, hn.reshape(bsz, SSD_HEADS, SSD_HEAD_DIM, SSD_STATE), convn[:, 8 - (SSD_CONV - 1):]


def _sconv_kernel(cb_ref, cc_ref, cx_ref, st0_ref, w_ref, o_ref, stn_ref, up_s):
    rows = cb_ref.shape[0]

    @pl.when(pl.program_id(1) == 0)
    def _():
        up_s[0:8, :] = st0_ref[0]

    up_s[8:8 + rows, :] = cc_ref[...] * cx_ref[...]
    base = 8 - (SCONV_K - 1)
    uc = up_s[base:base + rows, :] * w_ref[0:1, :]
    for i in range(1, SCONV_K):
        uc = uc + up_s[base + i:base + i + rows, :] * w_ref[i:i + 1, :]
    o_ref[...] = (cb_ref[...] * uc).astype(o_ref.dtype)
    tail = up_s[rows:rows + 8, :]
    stn_ref[0] = tail
    up_s[0:8, :] = tail


def sconv_branch(proj, bsz, t, prev, w):
    rows = min(t, 512)
    nt = t // rows
    col = lambda name: SEG[name][0] // SCONV_WIDTH
    seg = lambda name: pl.BlockSpec((rows, SCONV_WIDTH), lambda b, c, _j=col(name): (b * nt + c, _j))
    st0 = jnp.pad(prev, ((0, 0), (8 - (SCONV_K - 1), 0), (0, 0)))
    o, stn = pl.pallas_call(
        _sconv_kernel,
        out_shape=[jax.ShapeDtypeStruct((bsz * t, SCONV_WIDTH), MXU_DTYPE), jax.ShapeDtypeStruct((bsz, 8, SCONV_WIDTH), F32)],
        grid=(bsz, nt),
        in_specs=[seg("c_b"), seg("c_c"), seg("c_x"), pl.BlockSpec((1, 8, SCONV_WIDTH), lambda b, c: (b, 0, 0)),
                  pl.BlockSpec(w.shape, lambda b, c: (0, 0))],
        out_specs=[pl.BlockSpec((rows, SCONV_WIDTH), lambda b, c: (b * nt + c, 0)),
                   pl.BlockSpec((1, 8, SCONV_WIDTH), lambda b, c: (b, 0, 0))],
        scratch_shapes=[pltpu.VMEM((rows + 8, SCONV_WIDTH), F32)],
        compiler_params=_cparams("parallel", "arbitrary"), name="sconv_branch")(proj, proj, proj, st0, w)
    return o, stn[:, 8 - (SCONV_K - 1):]


def _rope(y, cos_t, sin_a, sin_b):
    return y * cos_t + pltpu.roll(y, 16, 1) * sin_a + pltpu.roll(y, V7X_LANES - 16, 1) * sin_b


def _mla_prep_kernel(ql_ref, kvl_ref, kpe_ref, cos_ref, sa_ref, sb_ref, gq_ref, wq_ref, gqh_ref, gkv_ref, gkr_ref,
                     q_ref, ckv_ref, kpeo_ref):
    rows = ql_ref.shape[0]
    cos_t, sin_a, sin_b = cos_ref[...], sa_ref[...], sb_ref[...]
    q = _mm(_row_rms(ql_ref[...]) * gq_ref[...], wq_ref[...])
    lane = _lane_iota(rows)
    nope = lane < MLA_NOPE
    scale = (MLA_NOPE + MLA_ROPE) ** -0.5 * math.log2(math.e)
    for h in range(MLA_HEADS):
        sl = slice(h * V7X_LANES, (h + 1) * V7X_LANES)
        blk = q[:, sl]
        sq = blk * blk
        s_n = jnp.sum(jnp.where(nope, sq, 0.0), axis=-1, keepdims=True) * (1.0 / MLA_NOPE)
        s_r = jnp.sum(jnp.where(nope, 0.0, sq), axis=-1, keepdims=True) * (1.0 / MLA_ROPE)
        y = blk * jnp.where(nope, lax.rsqrt(s_n + EPS), lax.rsqrt(s_r + EPS)) * gqh_ref[:, sl]
        q_ref[:, sl] = (_rope(y, cos_t, sin_a, sin_b) * scale).astype(q_ref.dtype)
    ckv_ref[...] = _row_rms(kvl_ref[...]) * gkv_ref[...]
    kp = kpe_ref[...]
    ms = jnp.sum(kp * kp, axis=-1, keepdims=True) * (1.0 / MLA_ROPE)
    kpeo_ref[...] = _rope(kp * lax.rsqrt(ms + EPS) * gkr_ref[...], cos_t, sin_a, sin_b)


def mla_prep(proj, n, tm, tabs, p):
    cos_t, sin_a, sin_b = tabs
    ntab = cos_t.shape[0] // tm
    seg = lambda name: pl.BlockSpec((tm, SEG[name][1]), lambda i, _j=SEG[name][0] // SEG[name][1]: (i, _j))
    tab = pl.BlockSpec((tm, V7X_LANES), lambda i: (i % ntab, 0))
    full = lambda a: pl.BlockSpec(a.shape, lambda i: (0, 0))
    consts = [p["mla_q_norm_g"], p["mla_wq"], p["mla_gq_head"], p["mla_kv_norm_g"], p["mla_gkr"]]
    return pl.pallas_call(
        _mla_prep_kernel,
        out_shape=[jax.ShapeDtypeStruct((n, MLA_HEADS * V7X_LANES), MXU_DTYPE),
                   jax.ShapeDtypeStruct((n, MLA_KV_RANK), F32), jax.ShapeDtypeStruct((n, V7X_LANES), F32)],
        grid=(n // tm,),
        in_specs=[seg("b_ql"), seg("b_kvl"), seg("b_kpe"), tab, tab, tab] + [full(a) for a in consts],
        out_specs=[pl.BlockSpec((tm, MLA_HEADS * V7X_LANES), lambda i: (i, 0)),
                   pl.BlockSpec((tm, MLA_KV_RANK), lambda i: (i, 0)), pl.BlockSpec((tm, V7X_LANES), lambda i: (i, 0))],
        compiler_params=_cparams("parallel"), name="mla_prep")(proj, proj, proj, cos_t, sin_a, sin_b, *consts)


def _mla_kv_kernel(ckv_ref, kpe_ref, wk_ref, wvt_ref, gk_ref, k_ref, vt_ref):
    c = ckv_ref[0].astype(MXU_DTYPE)
    kk = jnp.dot(c, wk_ref[...], preferred_element_type=F32)
    vt_ref[0, 0] = _mm_nt(wvt_ref[...], c).astype(vt_ref.dtype)
    kpe = kpe_ref[0]
    for h in range(MLA_HEADS):
        sl = slice(h * V7X_LANES, (h + 1) * V7X_LANES)
        blk = kk[:, sl]
        ms = jnp.sum(blk * blk, axis=-1, keepdims=True) * (1.0 / MLA_NOPE)
        k_ref[0, :, sl] = (blk * lax.rsqrt(ms + EPS) * gk_ref[...] + kpe).astype(k_ref.dtype)


def mla_kv(ckv_all, kpe_all, ts, p):
    bsz, s, _ = ckv_all.shape
    full = lambda a: pl.BlockSpec(a.shape, lambda b, i: (0, 0))
    consts = [p["mla_wk"], p["mla_wvt"], p["mla_gk"]]
    return pl.pallas_call(
        _mla_kv_kernel,
        out_shape=[jax.ShapeDtypeStruct((bsz, s, MLA_HEADS * V7X_LANES), MXU_DTYPE),
                   jax.ShapeDtypeStruct((bsz, s // ts, MLA_WIDTH, ts), MXU_DTYPE)],
        grid=(bsz, s // ts),
        in_specs=[pl.BlockSpec((1, ts, MLA_KV_RANK), lambda b, i: (b, i, 0)),
                  pl.BlockSpec((1, ts, V7X_LANES), lambda b, i: (b, i, 0))] + [full(a) for a in consts],
        out_specs=[pl.BlockSpec((1, ts, MLA_HEADS * V7X_LANES), lambda b, i: (b, i, 0)),
                   pl.BlockSpec((1, 1, MLA_WIDTH, ts), lambda b, i: (b, i, 0, 0))],
        compiler_params=_cparams("parallel", "parallel"), name="mla_kv")(ckv_all, kpe_all, *consts)


def _mla_attn_kernel(q_ref, k_ref, vt_ref, ot_ref, *, tq, tkc, q_off, nh):
    q_start = q_off + pl.program_id(2) * tq
    n_full = (q_start + CHUNK) // tkc
    n_total = (q_start + tq + tkc - 1) // tkc
    lanes = lambda h: slice(h * V7X_LANES, (h + 1) * V7X_LANES)
    qs = [q_ref[0, :, lanes(h)] for h in range(nh)]

    def body(c, stats, masked):
        k0 = pl.multiple_of(c * tkc, tkc)
        ss = [_mm_nt(k_ref[0, pl.ds(k0, tkc), lanes(h)], qs[h]) for h in range(nh)]
        if masked:
            k_chunk = (k0 + lax.broadcasted_iota(jnp.int32, (tkc, tq), 0)) // CHUNK
            q_chunk = (q_start + lax.broadcasted_iota(jnp.int32, (tkc, tq), 1)) // CHUNK
            ss = [jnp.where(k_chunk <= q_chunk, s, NEG_BIG) for s in ss]
        out = []
        ones = jnp.ones((8, tkc), MXU_DTYPE)
        for h in range(nh):
            m, acc = stats[h]
            m_new = jnp.maximum(m, jnp.max(ss[h], axis=0, keepdims=True))
            alpha = jnp.exp2(m - m_new)
            p = jnp.exp2(ss[h] - m_new).astype(MXU_DTYPE)
            v_ext = jnp.concatenate([vt_ref[0, c, h * MLA_V:(h + 1) * MLA_V, :], ones], axis=0)
            out.append((m_new, alpha * acc + jnp.dot(v_ext, p, preferred_element_type=F32)))
        return tuple(out)

    init = tuple((jnp.full((1, tq), NEG_BIG, F32), jnp.zeros((MLA_V + 8, tq), F32)) for _ in range(nh))
    stats = lax.fori_loop(0, n_full, functools.partial(body, masked=False), init)
    stats = lax.fori_loop(n_full, n_total, functools.partial(body, masked=True), stats)
    for h in range(nh):
        acc = stats[h][1]
        ot_ref[0, h * MLA_V:(h + 1) * MLA_V, :] = (acc[:MLA_V] / acc[MLA_V:MLA_V + 1]).astype(ot_ref.dtype)


def mla_attention(q, k, vt, q_off, tq, nh):
    bsz, t, _ = q.shape
    _, nchunk, _, tkc = vt.shape
    s = k.shape[1]
    kern = functools.partial(_mla_attn_kernel, tq=tq, tkc=tkc, q_off=q_off, nh=nh)
    return pl.pallas_call(
        kern, out_shape=jax.ShapeDtypeStruct((bsz, MLA_WIDTH, t), MXU_DTYPE),
        grid=(bsz, MLA_HEADS // nh, t // tq),
        in_specs=[pl.BlockSpec((1, tq, nh * V7X_LANES), lambda b, h, qi: (b, qi, h)),
                  pl.BlockSpec((1, s, nh * V7X_LANES), lambda b, h, qi: (b, 0, h), pipeline_mode=pl.Buffered(1)),
                  pl.BlockSpec((1, nchunk, nh * MLA_V, tkc), lambda b, h, qi: (b, 0, h, 0),
                               pipeline_mode=pl.Buffered(1))],
        out_specs=pl.BlockSpec((1, nh * MLA_V, tq), lambda b, h, qi: (b, h, qi)),
        compiler_params=_cparams("parallel", "parallel", "parallel"), name="mla_attention")(q, k, vt)


def _band_kernel(q_ref, ka_ref, kb_ref, va_ref, vb_ref, bias_ref, gq_ref, gk_ref, ot_ref, kn_ref, q_s, k_s, vt_s,
                 *, a_is_cache):
    tb = q_ref.shape[0]
    rows_q = q_s.shape[0]
    pad = rows_q - tb
    qn = _head64_rms(q_ref[...]) * gq_ref[...] * (BAND_HEAD_DIM ** -0.5 * math.log2(math.e))
    kb = _head64_rms(kb_ref[...]) * gk_ref[...]
    kn_ref[...] = kb
    ka = ka_ref[...] if a_is_cache else _head64_rms(ka_ref[...]) * gk_ref[...]
    vb = vb_ref[...]
    if pad:
        zeros = jnp.zeros((pad, BAND_WIDTH), F32)
        qn, kb, vb = (jnp.concatenate([a, zeros], axis=0) for a in (qn, kb, vb))
    q_s[...] = qn.astype(q_s.dtype)
    k_s[0:BAND_ROWS, :] = ka.astype(k_s.dtype)
    k_s[BAND_ROWS:, :] = kb.astype(k_s.dtype)
    vt_s[:, 0:BAND_ROWS] = va_ref[...].T.astype(vt_s.dtype)
    vt_s[:, BAND_ROWS:] = vb.T.astype(vt_s.dtype)

    win = BAND_ROWS + V7X_LANES
    low = _lane_iota(V7X_LANES) < BAND_HEAD_DIM
    ones = jnp.ones((8, win), MXU_DTYPE)
    win_row = lax.broadcasted_iota(jnp.int32, (win, V7X_LANES), 0)

    def tiles(first):
        for j in range(rows_q // V7X_LANES):
            r0 = j * V7X_LANES
            qj = q_s[r0:r0 + V7X_LANES, :]
            kw = k_s[r0:r0 + win, :]
            ss = []
            for hp in range(BAND_HEADS // 2):
                sl = slice(hp * V7X_LANES, (hp + 1) * V7X_LANES)
                zero = jnp.zeros((), q_s.dtype)
                q2 = jnp.concatenate([jnp.where(low, qj[:, sl], zero), jnp.where(low, zero, qj[:, sl])], axis=0)
                ss.append(_mm_nt(kw[:, sl], q2))
            for hp in range(BAND_HEADS // 2):
                sl = slice(hp * V7X_LANES, (hp + 1) * V7X_LANES)
                v_ext = jnp.concatenate([vt_s[sl, r0:r0 + win], ones], axis=0)
                s = ss[hp] + jnp.concatenate([bias_ref[2 * hp], bias_ref[2 * hp + 1]], axis=1)
                if first:
                    s = jnp.where(jnp.concatenate([win_row, win_row], axis=1) + r0 < BAND_ROWS, NEG_BIG, s)
                p = jnp.exp2(s - jnp.max(s, axis=0, keepdims=True)).astype(MXU_DTYPE)
                res = jnp.dot(v_ext, p, preferred_element_type=F32)
                den = res[V7X_LANES:V7X_LANES + 1]
                halves = [res[0:BAND_HEAD_DIM, 0:V7X_LANES] / den[:, 0:V7X_LANES],
                          res[BAND_HEAD_DIM:V7X_LANES, V7X_LANES:] / den[:, V7X_LANES:]]
                o_pair = jnp.concatenate(halves, axis=0).astype(ot_ref.dtype)
                if pad:
                    ot_ref[0, sl, :] = o_pair[:, 0:tb]
                else:
                    ot_ref[0, sl, r0:r0 + V7X_LANES] = o_pair

    if a_is_cache:
        tiles(False)
    else:
        pl.when(pl.program_id(1) == 0)(lambda: tiles(True))
        pl.when(pl.program_id(1) > 0)(lambda: tiles(False))


def band_branch(q_src, k_prev, v_prev, bsz, t, bias, gq, gk, a_is_cache):
    tb = min(t, BAND_ROWS)
    nt = t // tb
    rows_q = -(-tb // V7X_LANES) * V7X_LANES
    col = lambda name: SEG[name][0] // BAND_WIDTH
    cur = lambda name: pl.BlockSpec((tb, BAND_WIDTH), lambda b, i, _j=col(name): (b * nt + i, _j))
    if a_is_cache:
        prev = lambda name: pl.BlockSpec((BAND_ROWS, BAND_WIDTH), lambda b, i: (b, 0))
        ka, va = k_prev, v_prev
    else:
        prev = lambda name: pl.BlockSpec((BAND_ROWS, BAND_WIDTH), lambda b, i, _j=col(name): (b * nt + jnp.maximum(i - 1, 0), _j))
        ka, va = q_src, q_src
    full = lambda a: pl.BlockSpec(a.shape, lambda b, i: (0,) * a.ndim)
    kern = functools.partial(_band_kernel, a_is_cache=a_is_cache)
    return pl.pallas_call(
        kern,
        out_shape=[jax.ShapeDtypeStruct((bsz, BAND_WIDTH, t), MXU_DTYPE), jax.ShapeDtypeStruct((bsz * t, BAND_WIDTH), F32)],
        grid=(bsz, nt),
        in_specs=[cur("d_q"), prev("d_k"), cur("d_k"), prev("d_v"), cur("d_v"), full(bias), full(gq), full(gk)],
        out_specs=[pl.BlockSpec((1, BAND_WIDTH, tb), lambda b, i: (b, 0, i)),
                   pl.BlockSpec((tb, BAND_WIDTH), lambda b, i: (b * nt + i, 0))],
        scratch_shapes=[pltpu.VMEM((rows_q, BAND_WIDTH), MXU_DTYPE), pltpu.VMEM((BAND_ROWS + rows_q, BAND_WIDTH), MXU_DTYPE),
                        pltpu.VMEM((BAND_WIDTH, BAND_ROWS + rows_q), MXU_DTYPE)],
        compiler_params=_cparams("parallel", "parallel"), name="band_branch")(
            q_src, ka, q_src, va, q_src, bias, gq, gk)


def _head_blocks(w, per_head, used):
    k = w.shape[0]
    w = w.reshape(k, -1, per_head)[:, :, :used]
    return jnp.pad(w, ((0, 0), (0, 0), (0, V7X_LANES - used))).reshape(k, -1)


def _lane_pad(v, offset, width=V7X_LANES):
    return jnp.pad(v, (offset, width - offset - v.shape[0])).reshape(1, width)


def _layer_params(l, w):
    src = {}
    off = 0
    for name, width in (("a_z", 512), ("a_x", 512), ("a_b", 256), ("a_c", 256), ("a_dt", 8), ("b_ql", 384),
                        ("b_kvl", 256), ("b_kpe", 32), ("c_b", 512), ("c_c", 512), ("c_x", 512),
                        ("d_q", 512), ("d_k", 512), ("d_v", 512), ("gate", 4096)):
        src[name] = (off, width)
        off += width
    w_in = w["w_in"][l]
    pieces = []
    for name, (dst, dwidth) in sorted(SEG.items(), key=lambda kv: kv[1][0]):
        s0, sw = src[name]
        lead = KPE_LANE if name == "b_kpe" else 0
        pieces.append(jnp.pad(w_in[:, s0:s0 + sw], ((0, 0), (lead, dwidth - sw - lead))))
    p = {"w_in": jnp.concatenate(pieces, axis=1).astype(MXU_DTYPE)}
    cast = lambda a: a.astype(MXU_DTYPE)
    row = lambda a: a.reshape(1, -1)
    p["norm_mix_g"], p["norm_ffn_g"], p["b_gate"] = w["norm_mix_g"][l], w["norm_ffn_g"][l], w["b_gate"][l]
    for name in ("w_a_out", "w_b_out", "w_c_out", "w_d_out", "w_o", "w_ffn_up", "w_ffn_down"):
        p[name] = cast(w[name][l])
    p["ssd_conv_w"] = w["ssd_conv_w"][l]
    p["ssd_conv_b"] = row(w["ssd_conv_b"][l])
    p["ssd_dt_bias"] = _lane_pad(w["ssd_dt_bias"][l], 0)
    p["ssd_a_log"] = row(jnp.repeat(w["ssd_a_log"][l], SSD_HEAD_DIM))
    p["ssd_d"] = row(jnp.repeat(w["ssd_d"][l], SSD_HEAD_DIM))
    p["ssd_norm_g"] = row(w["ssd_norm_g"][l])
    head_of_lane = jnp.arange(SSD_INNER) // SSD_HEAD_DIM
    p["ssd_expand"] = (jnp.arange(V7X_LANES)[:, None] == head_of_lane[None, :]).astype(F32)
    p["mla_q_norm_g"] = row(w["mla_q_norm_g"][l])
    p["mla_wq"] = cast(_head_blocks(w["mla_w_q_up"][l], MLA_NOPE + MLA_ROPE, MLA_NOPE + MLA_ROPE))
    gqh = jnp.concatenate([w["mla_qn_g"][l], w["mla_qr_g"][l], jnp.zeros((V7X_LANES - MLA_NOPE - MLA_ROPE,), F32)])
    p["mla_gq_head"] = row(jnp.tile(gqh, MLA_HEADS))
    p["mla_kv_norm_g"] = row(w["mla_kv_norm_g"][l])
    p["mla_gkr"] = _lane_pad(w["mla_kr_g"][l], KPE_LANE)
    p["mla_wk"] = cast(_head_blocks(w["mla_w_kv_up"][l], MLA_NOPE + MLA_V, MLA_NOPE))
    p["mla_wvt"] = cast(w["mla_w_kv_up"][l].reshape(MLA_KV_RANK, MLA_HEADS, MLA_NOPE + MLA_V)[:, :, MLA_NOPE:]
                        .reshape(MLA_KV_RANK, MLA_WIDTH).T)
    p["mla_gk"] = _lane_pad(w["mla_kn_g"][l], 0)
    p["sconv_w"] = w["sconv_w"][l]
    p["band_gq"] = row(jnp.tile(w["band_qn_g"][l], BAND_HEADS))
    p["band_gk"] = row(jnp.tile(w["band_kn_g"][l], BAND_HEADS))
    key = jnp.arange(BAND_ROWS + V7X_LANES)[:, None]
    qry = jnp.arange(V7X_LANES)[None, :]
    rel = qry + BAND_ROWS - key
    in_band = jnp.logical_and(key // CHUNK >= qry // CHUNK, key // CHUNK <= qry // CHUNK + BAND_PAST_CHUNKS)
    bias = w["band_rel_bias"][l][:, jnp.clip(rel, -REL_CLIP, REL_CLIP) + REL_CLIP] * math.log2(math.e)
    p["band_bias"] = jnp.where(in_band[None], bias, NEG_BIG)
    return p


def _rope_tables(pos, rows):
    inv = ROPE_THETA ** (-jnp.arange(0, MLA_ROPE, 2, dtype=F32) / MLA_ROPE)
    ang = pos.astype(F32)[:, None] * inv[None, :]
    cos, sin = jnp.cos(ang), jnp.sin(ang)
    half = MLA_ROPE // 2
    z = lambda w: jnp.zeros((pos.shape[0], w), F32)
    tail = V7X_LANES - MLA_NOPE - MLA_ROPE
    cos_t = jnp.concatenate([jnp.ones((pos.shape[0], MLA_NOPE), F32), cos, cos, z(tail)], axis=1)
    sin_a = jnp.concatenate([z(MLA_NOPE + half), sin, z(tail)], axis=1)
    sin_b = jnp.concatenate([z(MLA_NOPE), -sin, z(half + tail)], axis=1)
    reps = max(rows // pos.shape[0], 1)
    return tuple(jnp.tile(a, (reps, 1)) for a in (cos_t, sin_a, sin_b))


def _layer(x, bsz, t, past, p, tabs, q_off):
    n = bsz * t
    tm = min(n, 512)
    h = rmsnorm_cast(x, p["norm_mix_g"], tm)
    proj = matmul(h, p["w_in"], min(n, 1024), 1920)
    ya, ssm_new, ssd_conv_new = ssd_branch(proj, bsz, t, past["ssd_conv"], past["ssm"], p)
    yc, sconv_new = sconv_branch(proj, bsz, t, past["sconv"], p["sconv_w"])
    q, ckv, kpe_pad = mla_prep(proj, n, tm, tabs, p)
    is_sample = past["mla_ckv"] is not None
    if is_sample:
        plen = past["mla_ckv"].shape[1]
        s = plen + t
        ckv_all = jnp.concatenate([past["mla_ckv"], ckv.reshape(bsz, t, MLA_KV_RANK)], axis=1)
        kpe_prev = jnp.pad(past["mla_kpe"], ((0, 0), (0, 0), (KPE_LANE, V7X_LANES - KPE_LANE - MLA_ROPE)))
        kpe_all = jnp.concatenate([kpe_prev, kpe_pad.reshape(bsz, t, V7X_LANES)], axis=1)
    else:
        s = t
        ckv_all, kpe_all = ckv.reshape(bsz, t, MLA_KV_RANK), kpe_pad.reshape(bsz, t, V7X_LANES)
    tkc = next(c for c in (256, 704, 128, s) if s % c == 0)
    kk, vt = mla_kv(ckv_all, kpe_all, tkc, p)
    ot = mla_attention(q.reshape(bsz, t, -1), kk, vt, q_off, min(t, 512), 8)
    ob = jnp.swapaxes(ot, 1, 2).reshape(n, MLA_WIDTH)
    if is_sample:
        kprev = past["band_k"].reshape(bsz * BAND_ROWS, BAND_WIDTH)
        vprev = past["band_v"].reshape(bsz * BAND_ROWS, BAND_WIDTH)
        odt, kn = band_branch(proj, kprev, vprev, bsz, t, p["band_bias"], p["band_gq"], p["band_gk"], True)
    else:
        odt, kn = band_branch(proj, None, None, bsz, t, p["band_bias"], p["band_gq"], p["band_gk"], False)
    od = jnp.swapaxes(odt, 1, 2).reshape(n, BAND_WIDTH)
    x = merge(x, proj, ya, ob, yc, od, p["b_gate"], p["w_a_out"], p["w_b_out"], p["w_c_out"], p["w_d_out"], p["w_o"], tm)
    x = ffn(x, p["norm_ffn_g"], p["w_ffn_up"], p["w_ffn_down"], tm, FF_HIDDEN // 2)
    dv0 = SEG["d_v"][0]
    kn = kn.reshape(bsz, t, BAND_HEADS, BAND_HEAD_DIM)
    vn = proj[:, dv0:dv0 + BAND_WIDTH].reshape(bsz, t, BAND_HEADS, BAND_HEAD_DIM)
    if is_sample:
        band_k = jnp.concatenate([past["band_k"], kn], axis=1)[:, t:]
        band_v = jnp.concatenate([past["band_v"], vn], axis=1)[:, t:]
    else:
        keep = min(BAND_ROWS, t)
        band_k, band_v = kn[:, t - keep:], vn[:, t - keep:]
    new = {"ssm": ssm_new, "ssd_conv": ssd_conv_new, "sconv": sconv_new,
           "mla_ckv": ckv.reshape(bsz, t, MLA_KV_RANK),
           "mla_kpe": kpe_pad[:, KPE_LANE:KPE_LANE + MLA_ROPE].reshape(bsz, t, MLA_ROPE),
           "band_k": band_k, "band_v": band_v}
    return x, new


def kernel(x_prompt, x_sample, state_ssm, state_ssd_conv, cache_mla_ckv, cache_mla_kpe, state_sconv, cache_band_k, cache_band_v, norm_mix_g, w_in, b_gate, ssd_conv_w, ssd_conv_b, ssd_dt_bias, ssd_a_log, ssd_d, ssd_norm_g, w_a_out, mla_q_norm_g, mla_w_q_up, mla_kv_norm_g, mla_w_kv_up, mla_qn_g, mla_kn_g, mla_qr_g, mla_kr_g, w_b_out, sconv_w, w_c_out, band_qn_g, band_kn_g, band_rel_bias, w_d_out, w_o, norm_ffn_g, w_ffn_up, w_ffn_down):
    weights = dict(norm_mix_g=norm_mix_g, w_in=w_in, b_gate=b_gate, ssd_conv_w=ssd_conv_w, ssd_conv_b=ssd_conv_b,
                   ssd_dt_bias=ssd_dt_bias, ssd_a_log=ssd_a_log, ssd_d=ssd_d, ssd_norm_g=ssd_norm_g, w_a_out=w_a_out,
                   mla_q_norm_g=mla_q_norm_g, mla_w_q_up=mla_w_q_up, mla_kv_norm_g=mla_kv_norm_g,
                   mla_w_kv_up=mla_w_kv_up, mla_qn_g=mla_qn_g, mla_kn_g=mla_kn_g, mla_qr_g=mla_qr_g,
                   mla_kr_g=mla_kr_g, w_b_out=w_b_out, sconv_w=sconv_w, w_c_out=w_c_out, band_qn_g=band_qn_g,
                   band_kn_g=band_kn_g, band_rel_bias=band_rel_bias, w_d_out=w_d_out, w_o=w_o,
                   norm_ffn_g=norm_ffn_g, w_ffn_up=w_ffn_up, w_ffn_down=w_ffn_down)
    depth = w_in.shape[0]
    b_p, t_p, d = x_prompt.shape
    b_s, t_s, _ = x_sample.shape
    past_len = cache_mla_ckv.shape[2]
    assert d == D_MODEL and t_p % BAND_ROWS == 0 and t_s == CHUNK and cache_band_k.shape[2] == BAND_ROWS
    assert past_len % CHUNK == 0
    tabs_p = _rope_tables(jnp.arange(t_p, dtype=jnp.int32), min(b_p * t_p, 512))
    tabs_s = _rope_tables(past_len + jnp.arange(t_s, dtype=jnp.int32), min(b_s * t_s, 512))
    y_p = x_prompt.reshape(b_p * t_p, d)
    y_s = x_sample.reshape(b_s * t_s, d)
    new_p, new_s = [], []
    for l in range(depth):
        p = _layer_params(l, weights)
        init_p = {"ssm": jnp.zeros((b_p, SSD_HEADS, SSD_HEAD_DIM, SSD_STATE), F32),
                  "ssd_conv": jnp.zeros((b_p, SSD_CONV - 1, SSD_CONV_DIM), F32),
                  "sconv": jnp.zeros((b_p, SCONV_K - 1, SCONV_WIDTH), F32),
                  "mla_ckv": None, "mla_kpe": None, "band_k": None, "band_v": None}
        y_p, st_p = _layer(y_p, b_p, t_p, init_p, p, tabs_p, 0)
        new_p.append(st_p)
        past_s = {"ssm": state_ssm[l], "ssd_conv": state_ssd_conv[l], "sconv": state_sconv[l],
                  "mla_ckv": cache_mla_ckv[l], "mla_kpe": cache_mla_kpe[l],
                  "band_k": cache_band_k[l], "band_v": cache_band_v[l]}
        y_s, st_s = _layer(y_s, b_s, t_s, past_s, p, tabs_s, past_len)
        new_s.append(st_s)

    def stack(states, name):
        return jnp.stack([s[name] for s in states], axis=0)

    out = [y_p.reshape(b_p, t_p, d), y_s.reshape(b_s, t_s, d)]
    for name in ("ssm", "ssd_conv", "mla_ckv", "mla_kpe", "sconv", "band_k", "band_v"):
        out += [stack(new_p, name), stack(new_s, name)]
    return tuple(out)
```

```python
import functools
import math

import jax
import jax.numpy as jnp
import numpy as np
from jax import lax
from jax.experimental import pallas as pl
from jax.experimental.pallas import tpu as pltpu

F32 = jnp.float32
MXU_DTYPE = jnp.bfloat16
EPS = 1e-6
NEG_BIG = -1e30

V7X_LANES = 128
V7X_VMEM_LIMIT_BYTES = 56 * 1024 * 1024

D_MODEL = 1024
CHUNK = 64
N_BRANCH = 4
SSD_HEADS, SSD_HEAD_DIM, SSD_GROUPS, SSD_STATE, SSD_CONV = 8, 64, 2, 128, 4
SSD_INNER = SSD_HEADS * SSD_HEAD_DIM
SSD_CONV_DIM = SSD_INNER + 2 * SSD_GROUPS * SSD_STATE
MLA_HEADS, MLA_Q_RANK, MLA_KV_RANK, MLA_NOPE, MLA_ROPE, MLA_V = 8, 384, 256, 64, 32, 64
MLA_WIDTH = MLA_HEADS * MLA_V
ROPE_THETA = 10000.0
SCONV_WIDTH, SCONV_K = 512, 3
BAND_HEADS, BAND_HEAD_DIM, BAND_PAST_CHUNKS, REL_CLIP = 8, 64, 8, 128
BAND_WIDTH = BAND_HEADS * BAND_HEAD_DIM
BAND_ROWS = BAND_PAST_CHUNKS * CHUNK
BAND_SPAN = BAND_ROWS + CHUNK
FF_HIDDEN = ((8 * D_MODEL // 3 + 255) // 256) * 256

SEG = {
    "gate": (0, 4096), "c_b": (4096, 512), "c_c": (4608, 512), "c_x": (5120, 512),
    "d_q": (5632, 512), "d_k": (6144, 512), "d_v": (6656, 512),
    "a_z": (7168, 512), "a_x": (7680, 512), "a_b": (8192, 256), "a_c": (8448, 256),
    "b_kvl": (8704, 256), "a_dt": (8960, 128), "b_kpe": (9088, 128), "b_ql": (9216, 384),
}
PROJ_COLS = 9600
KPE_LANE = 64


def _cparams(*sem):
    return pltpu.CompilerParams(dimension_semantics=sem, vmem_limit_bytes=V7X_VMEM_LIMIT_BYTES)


def _mm(a, b):
    return jnp.dot(a.astype(MXU_DTYPE), b.astype(MXU_DTYPE), preferred_element_type=F32)


def _mm_nt(a, b):
    return lax.dot_general(a.astype(MXU_DTYPE), b.astype(MXU_DTYPE), (((1,), (1,)), ((), ())),
                           preferred_element_type=F32)


def _mm_tn(a, b):
    return lax.dot_general(a.astype(MXU_DTYPE), b.astype(MXU_DTYPE), (((0,), (0,)), ((), ())),
                           preferred_element_type=F32)


def _split3(x):
    hi = x.astype(MXU_DTYPE)
    r1 = x - hi.astype(F32)
    mid = r1.astype(MXU_DTYPE)
    lo = (r1 - mid.astype(F32)).astype(MXU_DTYPE)
    return hi, mid, lo


def _mm_exact_lhs01(sel, x):
    hi, mid, lo = _split3(x)
    sel = sel.astype(MXU_DTYPE)
    d = functools.partial(jnp.dot, preferred_element_type=F32)
    return d(sel, hi) + d(sel, mid) + d(sel, lo)


def _mm_exact_rhs01(x, sel):
    hi, mid, lo = _split3(x)
    sel = sel.astype(MXU_DTYPE)
    d = functools.partial(jnp.dot, preferred_element_type=F32)
    return d(hi, sel) + d(mid, sel) + d(lo, sel)


def _mm_tn_exact_rhs01(x, sel):
    hi, mid, lo = _split3(x)
    sel = sel.astype(MXU_DTYPE)
    d = functools.partial(lax.dot_general, dimension_numbers=(((0,), (0,)), ((), ())), preferred_element_type=F32)
    return d(hi, sel) + d(mid, sel) + d(lo, sel)


def _silu(x):
    return x * jax.nn.sigmoid(x)


def _row_rms(x):
    return x * lax.rsqrt(jnp.mean(x * x, axis=-1, keepdims=True) + EPS)


def _lane_iota(rows):
    return lax.broadcasted_iota(jnp.int32, (rows, V7X_LANES), 1)


def _head64_rms(x):
    rows, width = x.shape
    low = _lane_iota(rows) < 64
    out = []
    for j in range(width // V7X_LANES):
        blk = x[:, j * V7X_LANES:(j + 1) * V7X_LANES]
        sq = blk * blk
        s_lo = jnp.sum(jnp.where(low, sq, 0.0), axis=-1, keepdims=True) * (1.0 / 64)
        s_hi = jnp.sum(jnp.where(low, 0.0, sq), axis=-1, keepdims=True) * (1.0 / 64)
        out.append(blk * jnp.where(low, lax.rsqrt(s_lo + EPS), lax.rsqrt(s_hi + EPS)))
    return jnp.concatenate(out, axis=-1)


def _rmsnorm_kernel(x_ref, g_ref, o_ref):
    o_ref[...] = (_row_rms(x_ref[...]) * g_ref[...]).astype(o_ref.dtype)


def rmsnorm_cast(x, g, tm):
    n, d = x.shape
    return pl.pallas_call(
        _rmsnorm_kernel, out_shape=jax.ShapeDtypeStruct((n, d), MXU_DTYPE), grid=(n // tm,),
        in_specs=[pl.BlockSpec((tm, d), lambda i: (i, 0)), pl.BlockSpec((1, d), lambda i: (0, 0))],
        out_specs=pl.BlockSpec((tm, d), lambda i: (i, 0)),
        compiler_params=_cparams("parallel"), name="rmsnorm_cast")(x, g.reshape(1, d))


def _matmul_kernel(a_ref, w_ref, o_ref):
    o_ref[...] = jnp.dot(a_ref[...], w_ref[...], preferred_element_type=F32)


def matmul(a, w, tm, tn):
    n, k = a.shape
    c = w.shape[1]
    return pl.pallas_call(
        _matmul_kernel, out_shape=jax.ShapeDtypeStruct((n, c), F32), grid=(c // tn, n // tm),
        in_specs=[pl.BlockSpec((tm, k), lambda j, i: (i, 0)), pl.BlockSpec((k, tn), lambda j, i: (0, j))],
        out_specs=pl.BlockSpec((tm, tn), lambda j, i: (i, j)),
        compiler_params=_cparams("parallel", "parallel"), name="in_proj_matmul")(a, w)


def _merge_kernel(x_ref, g_ref, ya_ref, ob_ref, yc_ref, od_ref, bg_ref, wa_ref, wb_ref, wc_ref, wd_ref, wo_ref, o_ref,
                  *, bd_transposed):
    def gate(k):
        return jax.nn.sigmoid(g_ref[:, k * D_MODEL:(k + 1) * D_MODEL] + bg_ref[:, k * D_MODEL:(k + 1) * D_MODEL])

    def attn_out(a_ref, w_ref):
        return _mm_tn(a_ref[0], w_ref[...]) if bd_transposed else _mm(a_ref[...], w_ref[...])

    merged = gate(0) * _mm(ya_ref[...], wa_ref[...])
    merged = merged + gate(1) * attn_out(ob_ref, wb_ref)
    merged = merged + gate(2) * _mm(yc_ref[...], wc_ref[...])
    merged = merged + gate(3) * attn_out(od_ref, wd_ref)
    o_ref[...] = x_ref[...] + _mm(merged, wo_ref[...])


def merge(x, proj, ya, ob, yc, od, b_gate, wa, wb, wc, wd, wo, tm):
    n, d = x.shape
    row = lambda w: pl.BlockSpec((tm, w), lambda i: (i, 0))
    full = lambda a: pl.BlockSpec(a.shape, lambda i: (0, 0))
    bd_transposed = ob.ndim == 3
    if bd_transposed:
        per_seq = ob.shape[2] // tm
        attn = pl.BlockSpec((1, ob.shape[1], tm), lambda i: (i // per_seq, 0, i % per_seq))
    else:
        attn = row(512)
    bg = b_gate.reshape(1, N_BRANCH * d)
    return pl.pallas_call(
        functools.partial(_merge_kernel, bd_transposed=bd_transposed),
        out_shape=jax.ShapeDtypeStruct((n, d), F32), grid=(n // tm,),
        in_specs=[row(d), pl.BlockSpec((tm, N_BRANCH * d), lambda i: (i, 0)), row(512), attn, row(512), attn,
                  full(bg), full(wa), full(wb), full(wc), full(wd), full(wo)],
        out_specs=row(d), compiler_params=_cparams("parallel"), name="merge")(
            x, proj, ya, ob, yc, od, bg, wa, wb, wc, wd, wo)


def _ffn_kernel(x_ref, g_ref, wg_ref, wv_ref, wd_ref, o_ref, h_s, acc_s):
    k = pl.program_id(1)

    @pl.when(k == 0)
    def _():
        h_s[...] = (_row_rms(x_ref[...]) * g_ref[...]).astype(h_s.dtype)
        acc_s[...] = jnp.zeros_like(acc_s)

    h = h_s[...]
    gate = jnp.dot(h, wg_ref[...], preferred_element_type=F32)
    val = jnp.dot(h, wv_ref[...], preferred_element_type=F32)
    acc_s[...] += _mm(_silu(gate) * val, wd_ref[...])

    @pl.when(k == pl.num_programs(1) - 1)
    def _():
        o_ref[...] = x_ref[...] + acc_s[...]


def ffn(x, g, w_up, w_down, tm, tk):
    n, d = x.shape
    nk = FF_HIDDEN // tk
    return pl.pallas_call(
        _ffn_kernel, out_shape=jax.ShapeDtypeStruct((n, d), F32), grid=(n // tm, nk),
        in_specs=[pl.BlockSpec((tm, d), lambda i, k: (i, 0)), pl.BlockSpec((1, d), lambda i, k: (0, 0)),
                  pl.BlockSpec((d, tk), lambda i, k: (0, k)), pl.BlockSpec((d, tk), lambda i, k: (0, nk + k)),
                  pl.BlockSpec((tk, d), lambda i, k: (k, 0))],
        out_specs=pl.BlockSpec((tm, d), lambda i, k: (i, 0)),
        scratch_shapes=[pltpu.VMEM((tm, d), MXU_DTYPE), pltpu.VMEM((tm, d), F32)],
        compiler_params=_cparams("parallel", "arbitrary"), name="ffn")(x, g.reshape(1, d), w_up, w_up, w_down)


def _ssd_kernel(z_ref, x_ref, b_ref, c_ref, dt_ref, conv0_ref, h0_ref, cw_ref, cbias_ref, dtb_ref, alog_ref,
                dfull_ref, ng_ref, tri_ref, expand_ref, y_ref, convn_ref, hn_ref, xp_s, h_s):
    L = CHUNK
    c = pl.program_id(1)

    @pl.when(c == 0)
    def _():
        xp_s[0:8, :] = conv0_ref[0]
        h_s[...] = h0_ref[0]

    xp_s[8:8 + L, 0:512] = x_ref[...]
    xp_s[8:8 + L, 512:768] = b_ref[...]
    xp_s[8:8 + L, 768:1024] = c_ref[...]
    base = 8 - (SSD_CONV - 1)
    acc = xp_s[base:base + L, :] * cw_ref[0:1, :]
    for i in range(1, SSD_CONV):
        acc = acc + xp_s[base + i:base + i + L, :] * cw_ref[i:i + 1, :]
    xbc = _silu(acc + cbias_ref[...])
    tail = xp_s[L:L + 8, :]
    convn_ref[0] = tail
    xp_s[0:8, :] = tail

    xs = xbc[:, 0:512]
    dtr = dt_ref[...] + dtb_ref[...]
    dt = jnp.maximum(dtr, 0.0) + jnp.log1p(jnp.exp(-jnp.abs(dtr)))
    dt_full = _mm_exact_rhs01(dt, expand_ref[...])
    da_full = dt_full * (-jnp.exp(alog_ref[...]))
    acs = _mm_exact_lhs01(tri_ref[...], da_full)
    x_dt = xs * dt_full
    acs_last = acs[L - 1:L, :]
    xd_end = x_dt * jnp.exp(acs_last - acs)
    exp_acs = jnp.exp(acs)

    row = lax.broadcasted_iota(jnp.int32, (L, V7X_LANES), 0)
    lane = _lane_iota(L)
    lane_s = jnp.where(lane < 64, lane, lane - 64)
    diag2 = row == lane_s
    tril2 = lane_s <= row
    row2 = lax.broadcasted_iota(jnp.int32, (2 * L, V7X_LANES), 0)
    lane2 = lax.broadcasted_iota(jnp.int32, (2 * L, V7X_LANES), 1)
    blockdiag = (row2 < L) == (lane2 < 64)

    y_parts = []
    for g in range(SSD_GROUPS):
        bg = xbc[:, 512 + 128 * g:512 + 128 * (g + 1)]
        cg = xbc[:, 768 + 128 * g:768 + 128 * (g + 1)]
        cb2 = _mm_nt(cg, jnp.concatenate([bg, bg], axis=0))
        hg = h_s[256 * g:256 * (g + 1), :]
        y_off = _mm_nt(cg, hg) * exp_acs[:, 256 * g:256 * (g + 1)]
        for jp in range(2):
            pair = 2 * g + jp
            sl = slice(128 * pair, 128 * (pair + 1))
            blk = acs[:, sl]
            at_s = jnp.sum(jnp.where(diag2, blk, 0.0), axis=0, keepdims=True)
            decay = jnp.where(tril2, jnp.exp(jnp.minimum(blk - at_s, 0.0)), 0.0)
            xpair = x_dt[:, sl]
            xblk = jnp.where(blockdiag, jnp.concatenate([xpair, xpair], axis=0), 0.0)
            y_parts.append(_mm(cb2 * decay, xblk) + y_off[:, 128 * jp:128 * (jp + 1)])
        state = _mm_tn(xd_end[:, 256 * g:256 * (g + 1)], bg)
        decay_col = jnp.exp(_mm_tn_exact_rhs01(da_full[:, 256 * g:256 * (g + 1)], jnp.ones((L, SSD_STATE), F32)))
        h_s[256 * g:256 * (g + 1), :] = decay_col * hg + state
    y = jnp.concatenate(y_parts, axis=-1) + dfull_ref[...] * xs
    gated = y * _silu(z_ref[...])
    y_ref[...] = (_row_rms(gated) * ng_ref[...]).astype(y_ref.dtype)
    hn_ref[0] = h_s[...]


def ssd_branch(proj, bsz, t, conv_prev, ssm_prev, p):
    nc = t // CHUNK
    L = CHUNK
    n = bsz * t
    col = lambda name: SEG[name][0] // SEG[name][1]
    seg = lambda name: pl.BlockSpec((L, SEG[name][1]), lambda b, c, _j=col(name): (b * nc + c, _j))
    full = lambda a: pl.BlockSpec(a.shape, lambda b, c: (0,) * a.ndim)
    conv0 = jnp.pad(conv_prev, ((0, 0), (8 - (SSD_CONV - 1), 0), (0, 0)))
    h0 = ssm_prev.reshape(bsz, SSD_INNER, SSD_STATE)
    tri = jnp.tril(jnp.ones((L, L), F32))
    consts = [p["ssd_conv_w"], p["ssd_conv_b"], p["ssd_dt_bias"], p["ssd_a_log"], p["ssd_d"], p["ssd_norm_g"], tri,
              p["ssd_expand"]]
    y, convn, hn = pl.pallas_call(
        _ssd_kernel,
        out_shape=[jax.ShapeDtypeStruct((n, SSD_INNER), MXU_DTYPE), jax.ShapeDtypeStruct((bsz, 8, SSD_CONV_DIM), F32),
                   jax.ShapeDtypeStruct((bsz, SSD_INNER, SSD_STATE), F32)],
        grid=(bsz, nc),
        in_specs=[seg("a_z"), seg("a_x"), seg("a_b"), seg("a_c"), seg("a_dt"),
                  pl.BlockSpec((1, 8, SSD_CONV_DIM), lambda b, c: (b, 0, 0)),
                  pl.BlockSpec((1, SSD_INNER, SSD_STATE), lambda b, c: (b, 0, 0))] + [full(a) for a in consts],
        out_specs=[pl.BlockSpec((L, SSD_INNER), lambda b, c: (b * nc + c, 0)),
                   pl.BlockSpec((1, 8, SSD_CONV_DIM), lambda b, c: (b, 0, 0)),
                   pl.BlockSpec((1, SSD_INNER, SSD_STATE), lambda b, c: (b, 0, 0))],
        scratch_shapes=[pltpu.VMEM((L + 8, SSD_CONV_DIM), F32), pltpu.VMEM((SSD_INNER, SSD_STATE), F32)],
        compiler_params=_cparams("parallel", "arbitrary"), name="ssd_branch")(
            proj, proj, proj, proj, proj, conv0, h0, *consts)
    return y, hn.reshape(bsz, SSD_HEADS, SSD_HEAD_DIM, SSD_STATE), convn[:, 8 - (SSD_CONV - 1):]


def _sconv_kernel(cb_ref, cc_ref, cx_ref, st0_ref, w_ref, o_ref, stn_ref, up_s):
    rows = cb_ref.shape[0]

    @pl.when(pl.program_id(1) == 0)
    def _():
        up_s[0:8, :] = st0_ref[0]

    up_s[8:8 + rows, :] = cc_ref[...] * cx_ref[...]
    base = 8 - (SCONV_K - 1)
    uc = up_s[base:base + rows, :] * w_ref[0:1, :]
    for i in range(1, SCONV_K):
        uc = uc + up_s[base + i:base + i + rows, :] * w_ref[i:i + 1, :]
    o_ref[...] = (cb_ref[...] * uc).astype(o_ref.dtype)
    tail = up_s[rows:rows + 8, :]
    stn_ref[0] = tail
    up_s[0:8, :] = tail


def sconv_branch(proj, bsz, t, prev, w):
    rows = min(t, 512)
    nt = t // rows
    col = lambda name: SEG[name][0] // SCONV_WIDTH
    seg = lambda name: pl.BlockSpec((rows, SCONV_WIDTH), lambda b, c, _j=col(name): (b * nt + c, _j))
    st0 = jnp.pad(prev, ((0, 0), (8 - (SCONV_K - 1), 0), (0, 0)))
    o, stn = pl.pallas_call(
        _sconv_kernel,
        out_shape=[jax.ShapeDtypeStruct((bsz * t, SCONV_WIDTH), MXU_DTYPE), jax.ShapeDtypeStruct((bsz, 8, SCONV_WIDTH), F32)],
        grid=(bsz, nt),
        in_specs=[seg("c_b"), seg("c_c"), seg("c_x"), pl.BlockSpec((1, 8, SCONV_WIDTH), lambda b, c: (b, 0, 0)),
                  pl.BlockSpec(w.shape, lambda b, c: (0, 0))],
        out_specs=[pl.BlockSpec((rows, SCONV_WIDTH), lambda b, c: (b * nt + c, 0)),
                   pl.BlockSpec((1, 8, SCONV_WIDTH), lambda b, c: (b, 0, 0))],
        scratch_shapes=[pltpu.VMEM((rows + 8, SCONV_WIDTH), F32)],
        compiler_params=_cparams("parallel", "arbitrary"), name="sconv_branch")(proj, proj, proj, st0, w)
    return o, stn[:, 8 - (SCONV_K - 1):]


def _rope(y, cos_t, sin_a, sin_b):
    return y * cos_t + pltpu.roll(y, 16, 1) * sin_a + pltpu.roll(y, V7X_LANES - 16, 1) * sin_b


def _mla_prep_kernel(ql_ref, kvl_ref, kpe_ref, cos_ref, sa_ref, sb_ref, gq_ref, wq_ref, gqh_ref, gkv_ref, gkr_ref,
                     q_ref, ckv_ref, kpeo_ref):
    rows = ql_ref.shape[0]
    cos_t, sin_a, sin_b = cos_ref[...], sa_ref[...], sb_ref[...]
    q = _mm(_row_rms(ql_ref[...]) * gq_ref[...], wq_ref[...])
    lane = _lane_iota(rows)
    nope = lane < MLA_NOPE
    scale = (MLA_NOPE + MLA_ROPE) ** -0.5 * math.log2(math.e)
    for h in range(MLA_HEADS):
        sl = slice(h * V7X_LANES, (h + 1) * V7X_LANES)
        blk = q[:, sl]
        sq = blk * blk
        s_n = jnp.sum(jnp.where(nope, sq, 0.0), axis=-1, keepdims=True) * (1.0 / MLA_NOPE)
        s_r = jnp.sum(jnp.where(nope, 0.0, sq), axis=-1, keepdims=True) * (1.0 / MLA_ROPE)
        y = blk * jnp.where(nope, lax.rsqrt(s_n + EPS), lax.rsqrt(s_r + EPS)) * gqh_ref[:, sl]
        q_ref[:, sl] = (_rope(y, cos_t, sin_a, sin_b) * scale).astype(q_ref.dtype)
    ckv_ref[...] = _row_rms(kvl_ref[...]) * gkv_ref[...]
    kp = kpe_ref[...]
    ms = jnp.sum(kp * kp, axis=-1, keepdims=True) * (1.0 / MLA_ROPE)
    kpeo_ref[...] = _rope(kp * lax.rsqrt(ms + EPS) * gkr_ref[...], cos_t, sin_a, sin_b)


def mla_prep(proj, n, tm, tabs, p):
    cos_t, sin_a, sin_b = tabs
    ntab = cos_t.shape[0] // tm
    seg = lambda name: pl.BlockSpec((tm, SEG[name][1]), lambda i, _j=SEG[name][0] // SEG[name][1]: (i, _j))
    tab = pl.BlockSpec((tm, V7X_LANES), lambda i: (i % ntab, 0))
    full = lambda a: pl.BlockSpec(a.shape, lambda i: (0, 0))
    consts = [p["mla_q_norm_g"], p["mla_wq"], p["mla_gq_head"], p["mla_kv_norm_g"], p["mla_gkr"]]
    return pl.pallas_call(
        _mla_prep_kernel,
        out_shape=[jax.ShapeDtypeStruct((n, MLA_HEADS * V7X_LANES), MXU_DTYPE),
                   jax.ShapeDtypeStruct((n, MLA_KV_RANK), F32), jax.ShapeDtypeStruct((n, V7X_LANES), F32)],
        grid=(n // tm,),
        in_specs=[seg("b_ql"), seg("b_kvl"), seg("b_kpe"), tab, tab, tab] + [full(a) for a in consts],
        out_specs=[pl.BlockSpec((tm, MLA_HEADS * V7X_LANES), lambda i: (i, 0)),
                   pl.BlockSpec((tm, MLA_KV_RANK), lambda i: (i, 0)), pl.BlockSpec((tm, V7X_LANES), lambda i: (i, 0))],
        compiler_params=_cparams("parallel"), name="mla_prep")(proj, proj, proj, cos_t, sin_a, sin_b, *consts)


def _mla_kv_kernel(ckv_ref, kpe_ref, wk_ref, wv_ref, gk_ref, k_ref, v_ref, *, transpose_v):
    c = ckv_ref[0].astype(MXU_DTYPE)
    kk = jnp.dot(c, wk_ref[...], preferred_element_type=F32)
    if transpose_v:
        v_ref[0, 0] = _mm_nt(wv_ref[...], c).astype(v_ref.dtype)
    else:
        v_ref[0] = jnp.dot(c, wv_ref[...], preferred_element_type=F32).astype(v_ref.dtype)
    kpe = kpe_ref[0]
    for h in range(MLA_HEADS):
        sl = slice(h * V7X_LANES, (h + 1) * V7X_LANES)
        blk = kk[:, sl]
        ms = jnp.sum(blk * blk, axis=-1, keepdims=True) * (1.0 / MLA_NOPE)
        k_ref[0, :, sl] = (blk * lax.rsqrt(ms + EPS) * gk_ref[...] + kpe).astype(k_ref.dtype)


def mla_kv(ckv_all, kpe_all, ts, p, transpose_v):
    bsz, s, _ = ckv_all.shape
    full = lambda a: pl.BlockSpec(a.shape, lambda b, i: (0, 0))
    consts = [p["mla_wk"], p["mla_wvt"] if transpose_v else p["mla_wv"], p["mla_gk"]]
    if transpose_v:
        v_shape = jax.ShapeDtypeStruct((bsz, s // ts, MLA_WIDTH, ts), MXU_DTYPE)
        v_spec = pl.BlockSpec((1, 1, MLA_WIDTH, ts), lambda b, i: (b, i, 0, 0))
    else:
        v_shape = jax.ShapeDtypeStruct((bsz, s, MLA_WIDTH), MXU_DTYPE)
        v_spec = pl.BlockSpec((1, ts, MLA_WIDTH), lambda b, i: (b, i, 0))
    return pl.pallas_call(
        functools.partial(_mla_kv_kernel, transpose_v=transpose_v),
        out_shape=[jax.ShapeDtypeStruct((bsz, s, MLA_HEADS * V7X_LANES), MXU_DTYPE), v_shape],
        grid=(bsz, s // ts),
        in_specs=[pl.BlockSpec((1, ts, MLA_KV_RANK), lambda b, i: (b, i, 0)),
                  pl.BlockSpec((1, ts, V7X_LANES), lambda b, i: (b, i, 0))] + [full(a) for a in consts],
        out_specs=[pl.BlockSpec((1, ts, MLA_HEADS * V7X_LANES), lambda b, i: (b, i, 0)), v_spec],
        compiler_params=_cparams("parallel", "parallel"), name="mla_kv")(ckv_all, kpe_all, *consts)


def _mla_attn_rows_kernel(q_ref, k_ref, v_ref, o_ref, *, q_off):
    t, s = q_ref.shape[1], k_ref.shape[1]
    low = _lane_iota(t) < MLA_V
    needs_mask = (s - 1) // CHUNK > q_off // CHUNK
    if needs_mask:
        q_chunk = (q_off + lax.broadcasted_iota(jnp.int32, (t, s), 0)) // CHUNK
        visible = lax.broadcasted_iota(jnp.int32, (t, s), 1) // CHUNK <= q_chunk
    for hp in range(MLA_HEADS // 2):
        v_pair = v_ref[0, :, hp * V7X_LANES:(hp + 1) * V7X_LANES]
        outs = []
        for sub in range(2):
            sl = slice((2 * hp + sub) * V7X_LANES, (2 * hp + sub + 1) * V7X_LANES)
            sc = _mm_nt(q_ref[0, :, sl], k_ref[0, :, sl])
            if needs_mask:
                sc = jnp.where(visible, sc, NEG_BIG)
            e = jnp.exp2(sc - jnp.max(sc, axis=-1, keepdims=True))
            outs.append(_mm(e, v_pair) / jnp.sum(e, axis=-1, keepdims=True))
        o_ref[0, :, hp * V7X_LANES:(hp + 1) * V7X_LANES] = jnp.where(low, outs[0], outs[1]).astype(o_ref.dtype)


def mla_attention_rows(q, k, v, q_off):
    bsz, t, _ = q.shape
    s = k.shape[1]
    return pl.pallas_call(
        functools.partial(_mla_attn_rows_kernel, q_off=q_off),
        out_shape=jax.ShapeDtypeStruct((bsz, t, MLA_WIDTH), MXU_DTYPE), grid=(bsz,),
        in_specs=[pl.BlockSpec((1, t, MLA_HEADS * V7X_LANES), lambda b: (b, 0, 0)),
                  pl.BlockSpec((1, s, MLA_HEADS * V7X_LANES), lambda b: (b, 0, 0)),
                  pl.BlockSpec((1, s, MLA_WIDTH), lambda b: (b, 0, 0))],
        out_specs=pl.BlockSpec((1, t, MLA_WIDTH), lambda b: (b, 0, 0)),
        compiler_params=_cparams("parallel"), name="mla_attention_rows")(q, k, v)


def _mla_attn_kernel(q_ref, k_ref, vt_ref, ot_ref, sa_s, sb_s, *, tq, tkc, q_off, nh):
    q_start = q_off + pl.program_id(2) * tq
    n_full = (q_start + CHUNK) // tkc
    n_total = (q_start + tq + tkc - 1) // tkc
    lanes = lambda h: slice(h * V7X_LANES, (h + 1) * V7X_LANES)
    qs = [q_ref[0, :, lanes(h)] for h in range(nh)]
    ones = jnp.ones((8, tkc), MXU_DTYPE)

    def produce(c, buf):
        k0 = pl.multiple_of(c * tkc, tkc)
        for h in range(nh):
            buf[h] = _mm_nt(k_ref[0, pl.ds(k0, tkc), lanes(h)], qs[h])

    def consume(c, buf, stats, masked):
        if masked:
            k_chunk = (c * tkc + lax.broadcasted_iota(jnp.int32, (tkc, tq), 0)) // CHUNK
            q_chunk = (q_start + lax.broadcasted_iota(jnp.int32, (tkc, tq), 1)) // CHUNK
            visible = k_chunk <= q_chunk
        out = []
        for h in range(nh):
            m, acc = stats[h]
            s = jnp.where(visible, buf[h], NEG_BIG) if masked else buf[h]
            m_new = jnp.maximum(m, jnp.max(s, axis=0, keepdims=True))
            alpha = jnp.exp2(m - m_new)
            p = jnp.exp2(s - m_new).astype(MXU_DTYPE)
            v_ext = jnp.concatenate([vt_ref[0, c, h * MLA_V:(h + 1) * MLA_V, :], ones], axis=0)
            out.append((m_new, alpha * acc + jnp.dot(v_ext, p, preferred_element_type=F32)))
        return tuple(out)

    def pair(i, stats):
        c = 2 * i
        produce(c + 1, sb_s)
        stats = consume(c, sa_s, stats, False)
        produce(jnp.minimum(c + 2, n_total - 1), sa_s)
        return consume(c + 1, sb_s, stats, False)

    def single(c, stats):
        produce(c, sb_s)
        return consume(c, sb_s, stats, True)

    init = tuple((jnp.full((1, tq), NEG_BIG, F32), jnp.zeros((MLA_V + 8, tq), F32)) for _ in range(nh))
    produce(0, sa_s)
    n_pairs = n_full // 2
    stats = lax.fori_loop(0, n_pairs, pair, init)
    stats = consume(2 * n_pairs, sa_s, stats, True)
    stats = lax.fori_loop(2 * n_pairs + 1, n_total, single, stats)
    for h in range(nh):
        acc = stats[h][1]
        ot_ref[0, h * MLA_V:(h + 1) * MLA_V, :] = (acc[:MLA_V] / acc[MLA_V:MLA_V + 1]).astype(ot_ref.dtype)


def mla_attention(q, k, vt, q_off, tq, nh):
    bsz, t, _ = q.shape
    _, nchunk, _, tkc = vt.shape
    s = k.shape[1]
    kern = functools.partial(_mla_attn_kernel, tq=tq, tkc=tkc, q_off=q_off, nh=nh)
    return pl.pallas_call(
        kern, out_shape=jax.ShapeDtypeStruct((bsz, MLA_WIDTH, t), MXU_DTYPE),
        grid=(bsz, MLA_HEADS // nh, t // tq),
        in_specs=[pl.BlockSpec((1, tq, nh * V7X_LANES), lambda b, h, qi: (b, qi, h)),
                  pl.BlockSpec((1, s, nh * V7X_LANES), lambda b, h, qi: (b, 0, h), pipeline_mode=pl.Buffered(1)),
                  pl.BlockSpec((1, nchunk, nh * MLA_V, tkc), lambda b, h, qi: (b, 0, h, 0),
                               pipeline_mode=pl.Buffered(1))],
        out_specs=pl.BlockSpec((1, nh * MLA_V, tq), lambda b, h, qi: (b, h, qi)),
        scratch_shapes=[pltpu.VMEM((nh, tkc, tq), F32), pltpu.VMEM((nh, tkc, tq), F32)],
        compiler_params=_cparams("parallel", "parallel", "parallel"), name="mla_attention")(q, k, vt)


def _band_kernel(q_ref, ka_ref, kb_ref, va_ref, vb_ref, bias_ref, gq_ref, gk_ref, ot_ref, kn_ref, q_s, k_s, vt_s,
                 *, a_is_cache):
    tb = q_ref.shape[0]
    rows_q = q_s.shape[0]
    pad = rows_q - tb
    qn = _head64_rms(q_ref[...]) * gq_ref[...] * (BAND_HEAD_DIM ** -0.5 * math.log2(math.e))
    kb = _head64_rms(kb_ref[...]) * gk_ref[...]
    kn_ref[...] = kb
    ka = ka_ref[...] if a_is_cache else _head64_rms(ka_ref[...]) * gk_ref[...]
    vb = vb_ref[...]
    if pad:
        zeros = jnp.zeros((pad, BAND_WIDTH), F32)
        qn, kb, vb = (jnp.concatenate([a, zeros], axis=0) for a in (qn, kb, vb))
    q_s[...] = qn.astype(q_s.dtype)
    k_s[0:BAND_ROWS, :] = ka.astype(k_s.dtype)
    k_s[BAND_ROWS:, :] = kb.astype(k_s.dtype)
    vt_s[:, 0:BAND_ROWS] = va_ref[...].T.astype(vt_s.dtype)
    vt_s[:, BAND_ROWS:] = vb.T.astype(vt_s.dtype)

    win = BAND_ROWS + V7X_LANES
    low = _lane_iota(V7X_LANES) < BAND_HEAD_DIM
    ones = jnp.ones((8, win), MXU_DTYPE)
    win_row = lax.broadcasted_iota(jnp.int32, (win, V7X_LANES), 0)

    def tiles(first):
        for j in range(rows_q // V7X_LANES):
            r0 = j * V7X_LANES
            qj = q_s[r0:r0 + V7X_LANES, :]
            kw = k_s[r0:r0 + win, :]
            ss = []
            for hp in range(BAND_HEADS // 2):
                sl = slice(hp * V7X_LANES, (hp + 1) * V7X_LANES)
                zero = jnp.zeros((), q_s.dtype)
                q2 = jnp.concatenate([jnp.where(low, qj[:, sl], zero), jnp.where(low, zero, qj[:, sl])], axis=0)
                ss.append(_mm_nt(kw[:, sl], q2))
            for hp in range(BAND_HEADS // 2):
                sl = slice(hp * V7X_LANES, (hp + 1) * V7X_LANES)
                v_ext = jnp.concatenate([vt_s[sl, r0:r0 + win], ones], axis=0)
                s = ss[hp] + jnp.concatenate([bias_ref[2 * hp], bias_ref[2 * hp + 1]], axis=1)
                if first:
                    s = jnp.where(jnp.concatenate([win_row, win_row], axis=1) + r0 < BAND_ROWS, NEG_BIG, s)
                p = jnp.exp2(s - jnp.max(s, axis=0, keepdims=True)).astype(MXU_DTYPE)
                res = jnp.dot(v_ext, p, preferred_element_type=F32)
                den = res[V7X_LANES:V7X_LANES + 1]
                halves = [res[0:BAND_HEAD_DIM, 0:V7X_LANES] / den[:, 0:V7X_LANES],
                          res[BAND_HEAD_DIM:V7X_LANES, V7X_LANES:] / den[:, V7X_LANES:]]
                o_pair = jnp.concatenate(halves, axis=0).astype(ot_ref.dtype)
                if pad:
                    ot_ref[0, sl, :] = o_pair[:, 0:tb]
                else:
                    ot_ref[0, sl, r0:r0 + V7X_LANES] = o_pair

    if a_is_cache:
        tiles(False)
    else:
        pl.when(pl.program_id(1) == 0)(lambda: tiles(True))
        pl.when(pl.program_id(1) > 0)(lambda: tiles(False))


def band_branch(q_src, k_prev, v_prev, bsz, t, bias, gq, gk, a_is_cache):
    tb = min(t, BAND_ROWS)
    nt = t // tb
    rows_q = -(-tb // V7X_LANES) * V7X_LANES
    col = lambda name: SEG[name][0] // BAND_WIDTH
    cur = lambda name: pl.BlockSpec((tb, BAND_WIDTH), lambda b, i, _j=col(name): (b * nt + i, _j))
    if a_is_cache:
        prev = lambda name: pl.BlockSpec((BAND_ROWS, BAND_WIDTH), lambda b, i: (b, 0))
        ka, va = k_prev, v_prev
    else:
        prev = lambda name: pl.BlockSpec((BAND_ROWS, BAND_WIDTH), lambda b, i, _j=col(name): (b * nt + jnp.maximum(i - 1, 0), _j))
        ka, va = q_src, q_src
    full = lambda a: pl.BlockSpec(a.shape, lambda b, i: (0,) * a.ndim)
    kern = functools.partial(_band_kernel, a_is_cache=a_is_cache)
    return pl.pallas_call(
        kern,
        out_shape=[jax.ShapeDtypeStruct((bsz, BAND_WIDTH, t), MXU_DTYPE), jax.ShapeDtypeStruct((bsz * t, BAND_WIDTH), F32)],
        grid=(bsz, nt),
        in_specs=[cur("d_q"), prev("d_k"), cur("d_k"), prev("d_v"), cur("d_v"), full(bias), full(gq), full(gk)],
        out_specs=[pl.BlockSpec((1, BAND_WIDTH, tb), lambda b, i: (b, 0, i)),
                   pl.BlockSpec((tb, BAND_WIDTH), lambda b, i: (b * nt + i, 0))],
        scratch_shapes=[pltpu.VMEM((rows_q, BAND_WIDTH), MXU_DTYPE), pltpu.VMEM((BAND_ROWS + rows_q, BAND_WIDTH), MXU_DTYPE),
                        pltpu.VMEM((BAND_WIDTH, BAND_ROWS + rows_q), MXU_DTYPE)],
        compiler_params=_cparams("parallel", "parallel"), name="band_branch")(
            q_src, ka, q_src, va, q_src, bias, gq, gk)


def _head_blocks(w, per_head, used):
    k = w.shape[0]
    w = w.reshape(k, -1, per_head)[:, :, :used]
    return jnp.pad(w, ((0, 0), (0, 0), (0, V7X_LANES - used))).reshape(k, -1)


def _lane_pad(v, offset, width=V7X_LANES):
    return jnp.pad(v, (offset, width - offset - v.shape[0])).reshape(1, width)


def _layer_params(l, w):
    src = {}
    off = 0
    for name, width in (("a_z", 512), ("a_x", 512), ("a_b", 256), ("a_c", 256), ("a_dt", 8), ("b_ql", 384),
                        ("b_kvl", 256), ("b_kpe", 32), ("c_b", 512), ("c_c", 512), ("c_x", 512),
                        ("d_q", 512), ("d_k", 512), ("d_v", 512), ("gate", 4096)):
        src[name] = (off, width)
        off += width
    w_in = w["w_in"][l].astype(MXU_DTYPE)
    pieces = []
    for name, (dst, dwidth) in sorted(SEG.items(), key=lambda kv: kv[1][0]):
        s0, sw = src[name]
        lead = KPE_LANE if name == "b_kpe" else 0
        for width in (lead, None, dwidth - sw - lead):
            if width is None:
                pieces.append(w_in[:, s0:s0 + sw])
            elif width:
                pieces.append(jnp.zeros((D_MODEL, width), MXU_DTYPE))
    p = {"w_in": jnp.concatenate(pieces, axis=1)}
    cast = lambda a: a.astype(MXU_DTYPE)
    row = lambda a: a.reshape(1, -1)
    p["norm_mix_g"], p["norm_ffn_g"], p["b_gate"] = w["norm_mix_g"][l], w["norm_ffn_g"][l], w["b_gate"][l]
    for name in ("w_a_out", "w_b_out", "w_c_out", "w_d_out", "w_o", "w_ffn_up", "w_ffn_down"):
        p[name] = cast(w[name][l])
    p["ssd_conv_w"] = w["ssd_conv_w"][l]
    p["ssd_conv_b"] = row(w["ssd_conv_b"][l])
    p["ssd_dt_bias"] = _lane_pad(w["ssd_dt_bias"][l], 0)
    p["ssd_a_log"] = row(jnp.repeat(w["ssd_a_log"][l], SSD_HEAD_DIM))
    p["ssd_d"] = row(jnp.repeat(w["ssd_d"][l], SSD_HEAD_DIM))
    p["ssd_norm_g"] = row(w["ssd_norm_g"][l])
    head_of_lane = jnp.arange(SSD_INNER) // SSD_HEAD_DIM
    p["ssd_expand"] = (jnp.arange(V7X_LANES)[:, None] == head_of_lane[None, :]).astype(F32)
    p["mla_q_norm_g"] = row(w["mla_q_norm_g"][l])
    p["mla_wq"] = cast(_head_blocks(w["mla_w_q_up"][l], MLA_NOPE + MLA_ROPE, MLA_NOPE + MLA_ROPE))
    gqh = jnp.concatenate([w["mla_qn_g"][l], w["mla_qr_g"][l], jnp.zeros((V7X_LANES - MLA_NOPE - MLA_ROPE,), F32)])
    p["mla_gq_head"] = row(jnp.tile(gqh, MLA_HEADS))
    p["mla_kv_norm_g"] = row(w["mla_kv_norm_g"][l])
    p["mla_gkr"] = _lane_pad(w["mla_kr_g"][l], KPE_LANE)
    p["mla_wk"] = cast(_head_blocks(w["mla_w_kv_up"][l], MLA_NOPE + MLA_V, MLA_NOPE))
    p["mla_wv"] = cast(w["mla_w_kv_up"][l].reshape(MLA_KV_RANK, MLA_HEADS, MLA_NOPE + MLA_V)[:, :, MLA_NOPE:]
                       .reshape(MLA_KV_RANK, MLA_WIDTH))
    p["mla_wvt"] = p["mla_wv"].T
    p["mla_gk"] = _lane_pad(w["mla_kn_g"][l], 0)
    p["sconv_w"] = w["sconv_w"][l]
    p["band_gq"] = row(jnp.tile(w["band_qn_g"][l], BAND_HEADS))
    p["band_gk"] = row(jnp.tile(w["band_kn_g"][l], BAND_HEADS))
    nkey, nqry = BAND_ROWS + V7X_LANES, V7X_LANES
    key = np.arange(nkey)[:, None]
    qry = np.arange(nqry)[None, :]
    in_band = np.logical_and(key // CHUNK >= qry // CHUNK, key // CHUNK <= qry // CHUNK + BAND_PAST_CHUNKS)
    diff = np.arange(nqry - 1 + BAND_ROWS, -nkey + BAND_ROWS, -1)
    by_diff = w["band_rel_bias"][l][:, np.clip(diff, -REL_CLIP, REL_CLIP) + REL_CLIP]
    skew = jnp.tile(jnp.pad(by_diff, ((0, 0), (0, 1)))[:, None, :], (1, nqry, 1)).reshape(BAND_HEADS, -1)
    skew = skew[:, :nqry * (nkey + nqry - 1)].reshape(BAND_HEADS, nqry, nkey + nqry - 1)
    bias = jnp.swapaxes(skew[:, :, nqry - 1:], 1, 2) * math.log2(math.e)
    p["band_bias"] = jnp.where(in_band[None], bias, NEG_BIG)
    return p


def _rope_tables(pos, rows):
    inv = ROPE_THETA ** (-jnp.arange(0, MLA_ROPE, 2, dtype=F32) / MLA_ROPE)
    ang = pos.astype(F32)[:, None] * inv[None, :]
    cos, sin = jnp.cos(ang), jnp.sin(ang)
    half = MLA_ROPE // 2
    z = lambda w: jnp.zeros((pos.shape[0], w), F32)
    tail = V7X_LANES - MLA_NOPE - MLA_ROPE
    cos_t = jnp.concatenate([jnp.ones((pos.shape[0], MLA_NOPE), F32), cos, cos, z(tail)], axis=1)
    sin_a = jnp.concatenate([z(MLA_NOPE + half), sin, z(tail)], axis=1)
    sin_b = jnp.concatenate([z(MLA_NOPE), -sin, z(half + tail)], axis=1)
    reps = max(rows // pos.shape[0], 1)
    return tuple(jnp.tile(a, (reps, 1)) for a in (cos_t, sin_a, sin_b))


def _layer(x, bsz, t, past, p, tabs, q_off):
    n = bsz * t
    tm = min(n, 512)
    h = rmsnorm_cast(x, p["norm_mix_g"], tm)
    proj = matmul(h, p["w_in"], min(n, 1024), 1920)
    ya, ssm_new, ssd_conv_new = ssd_branch(proj, bsz, t, past["ssd_conv"], past["ssm"], p)
    yc, sconv_new = sconv_branch(proj, bsz, t, past["sconv"], p["sconv_w"])
    q, ckv, kpe_pad = mla_prep(proj, n, tm, tabs, p)
    is_sample = past["mla_ckv"] is not None
    if is_sample:
        plen = past["mla_ckv"].shape[1]
        s = plen + t
        ckv_all = jnp.concatenate([past["mla_ckv"], ckv.reshape(bsz, t, MLA_KV_RANK)], axis=1)
        kpe_prev = jnp.pad(past["mla_kpe"], ((0, 0), (0, 0), (KPE_LANE, V7X_LANES - KPE_LANE - MLA_ROPE)))
        kpe_all = jnp.concatenate([kpe_prev, kpe_pad.reshape(bsz, t, V7X_LANES)], axis=1)
    else:
        s = t
        ckv_all, kpe_all = ckv.reshape(bsz, t, MLA_KV_RANK), kpe_pad.reshape(bsz, t, V7X_LANES)
    q3 = q.reshape(bsz, t, -1)
    if t <= V7X_LANES:
        ts = next(c for c in (512, 704, 256, s) if s % c == 0)
        kk, vv = mla_kv(ckv_all, kpe_all, ts, p, False)
        ob = mla_attention_rows(q3, kk, vv, q_off).reshape(n, MLA_WIDTH)
    else:
        tkc = next(c for c in (256, 128, s) if s % c == 0)
        kk, vt = mla_kv(ckv_all, kpe_all, tkc, p, True)
        ob = mla_attention(q3, kk, vt, q_off, min(t, 512), 8)
    if is_sample:
        kprev = past["band_k"].reshape(bsz * BAND_ROWS, BAND_WIDTH)
        vprev = past["band_v"].reshape(bsz * BAND_ROWS, BAND_WIDTH)
        od, kn = band_branch(proj, kprev, vprev, bsz, t, p["band_bias"], p["band_gq"], p["band_gk"], True)
    else:
        od, kn = band_branch(proj, None, None, bsz, t, p["band_bias"], p["band_gq"], p["band_gk"], False)
    if ob.ndim != od.ndim or t % tm:
        ob = ob if ob.ndim == 2 else jnp.swapaxes(ob, 1, 2).reshape(n, MLA_WIDTH)
        od = jnp.swapaxes(od, 1, 2).reshape(n, BAND_WIDTH)
    x = merge(x, proj, ya, ob, yc, od, p["b_gate"], p["w_a_out"], p["w_b_out"], p["w_c_out"], p["w_d_out"], p["w_o"], tm)
    x = ffn(x, p["norm_ffn_g"], p["w_ffn_up"], p["w_ffn_down"], tm, FF_HIDDEN // 2)
    dv0 = SEG["d_v"][0]
    keep = min(BAND_ROWS, t)
    heads = lambda a: a.reshape(bsz, keep, BAND_HEADS, BAND_HEAD_DIM)
    kn = heads(kn.reshape(bsz, t, BAND_WIDTH)[:, t - keep:])
    vn = heads(proj.reshape(bsz, t, PROJ_COLS)[:, t - keep:, dv0:dv0 + BAND_WIDTH])
    if is_sample:
        band_k = jnp.concatenate([past["band_k"][:, t:], kn], axis=1)
        band_v = jnp.concatenate([past["band_v"][:, t:], vn], axis=1)
    else:
        band_k, band_v = kn, vn
    new = {"ssm": ssm_new, "ssd_conv": ssd_conv_new, "sconv": sconv_new,
           "mla_ckv": ckv.reshape(bsz, t, MLA_KV_RANK),
           "mla_kpe": kpe_pad[:, KPE_LANE:KPE_LANE + MLA_ROPE].reshape(bsz, t, MLA_ROPE),
           "band_k": band_k, "band_v": band_v}
    return x, new


def kernel(x_prompt, x_sample, state_ssm, state_ssd_conv, cache_mla_ckv, cache_mla_kpe, state_sconv, cache_band_k, cache_band_v, norm_mix_g, w_in, b_gate, ssd_conv_w, ssd_conv_b, ssd_dt_bias, ssd_a_log, ssd_d, ssd_norm_g, w_a_out, mla_q_norm_g, mla_w_q_up, mla_kv_norm_g, mla_w_kv_up, mla_qn_g, mla_kn_g, mla_qr_g, mla_kr_g, w_b_out, sconv_w, w_c_out, band_qn_g, band_kn_g, band_rel_bias, w_d_out, w_o, norm_ffn_g, w_ffn_up, w_ffn_down):
    weights = dict(norm_mix_g=norm_mix_g, w_in=w_in, b_gate=b_gate, ssd_conv_w=ssd_conv_w, ssd_conv_b=ssd_conv_b,
                   ssd_dt_bias=ssd_dt_bias, ssd_a_log=ssd_a_log, ssd_d=ssd_d, ssd_norm_g=ssd_norm_g, w_a_out=w_a_out,
                   mla_q_norm_g=mla_q_norm_g, mla_w_q_up=mla_w_q_up, mla_kv_norm_g=mla_kv_norm_g,
                   mla_w_kv_up=mla_w_kv_up, mla_qn_g=mla_qn_g, mla_kn_g=mla_kn_g, mla_qr_g=mla_qr_g,
                   mla_kr_g=mla_kr_g, w_b_out=w_b_out, sconv_w=sconv_w, w_c_out=w_c_out, band_qn_g=band_qn_g,
                   band_kn_g=band_kn_g, band_rel_bias=band_rel_bias, w_d_out=w_d_out, w_o=w_o,
                   norm_ffn_g=norm_ffn_g, w_ffn_up=w_ffn_up, w_ffn_down=w_ffn_down)
    depth = w_in.shape[0]
    b_p, t_p, d = x_prompt.shape
    b_s, t_s, _ = x_sample.shape
    past_len = cache_mla_ckv.shape[2]
    assert d == D_MODEL and t_p % BAND_ROWS == 0 and t_s == CHUNK and cache_band_k.shape[2] == BAND_ROWS
    assert past_len % CHUNK == 0
    tabs_p = _rope_tables(jnp.arange(t_p, dtype=jnp.int32), min(b_p * t_p, 512))
    tabs_s = _rope_tables(past_len + jnp.arange(t_s, dtype=jnp.int32), min(b_s * t_s, 512))
    y_p = x_prompt.reshape(b_p * t_p, d)
    y_s = x_sample.reshape(b_s * t_s, d)
    new_p, new_s = [], []
    for l in range(depth):
        p = _layer_params(l, weights)
        init_p = {"ssm": jnp.zeros((b_p, SSD_HEADS, SSD_HEAD_DIM, SSD_STATE), F32),
                  "ssd_conv": jnp.zeros((b_p, SSD_CONV - 1, SSD_CONV_DIM), F32),
                  "sconv": jnp.zeros((b_p, SCONV_K - 1, SCONV_WIDTH), F32),
                  "mla_ckv": None, "mla_kpe": None, "band_k": None, "band_v": None}
        y_p, st_p = _layer(y_p, b_p, t_p, init_p, p, tabs_p, 0)
        new_p.append(st_p)
        past_s = {"ssm": state_ssm[l], "ssd_conv": state_ssd_conv[l], "sconv": state_sconv[l],
                  "mla_ckv": cache_mla_ckv[l], "mla_kpe": cache_mla_kpe[l],
                  "band_k": cache_band_k[l], "band_v": cache_band_v[l]}
        y_s, st_s = _layer(y_s, b_s, t_s, past_s, p, tabs_s, past_len)
        new_s.append(st_s)

    def stack(states, name):
        return jnp.stack([s[name] for s in states], axis=0)

    out = [y_p.reshape(b_p, t_p, d), y_s.reshape(b_s, t_s, d)]
    for name in ("ssm", "ssd_conv", "mla_ckv", "mla_kpe", "sconv", "band_k", "band_v"):
        out += [stack(new_p, name), stack(new_s, name)]
    return tuple(out)
```

```python
import functools
import math

import jax
import jax.numpy as jnp
import numpy as np
from jax import lax
from jax.experimental import pallas as pl
from jax.experimental.pallas import tpu as pltpu

F32 = jnp.float32
MXU_DTYPE = jnp.bfloat16
EPS = 1e-6
NEG_BIG = -1e30

V7X_LANES = 128
V7X_VMEM_LIMIT_BYTES = 56 * 1024 * 1024

D_MODEL = 1024
CHUNK = 64
N_BRANCH = 4
SSD_HEADS, SSD_HEAD_DIM, SSD_GROUPS, SSD_STATE, SSD_CONV = 8, 64, 2, 128, 4
SSD_INNER = SSD_HEADS * SSD_HEAD_DIM
SSD_CONV_DIM = SSD_INNER + 2 * SSD_GROUPS * SSD_STATE
MLA_HEADS, MLA_Q_RANK, MLA_KV_RANK, MLA_NOPE, MLA_ROPE, MLA_V = 8, 384, 256, 64, 32, 64
MLA_WIDTH = MLA_HEADS * MLA_V
ROPE_THETA = 10000.0
SCONV_WIDTH, SCONV_K = 512, 3
BAND_HEADS, BAND_HEAD_DIM, BAND_PAST_CHUNKS, REL_CLIP = 8, 64, 8, 128
BAND_WIDTH = BAND_HEADS * BAND_HEAD_DIM
BAND_ROWS = BAND_PAST_CHUNKS * CHUNK
BAND_SPAN = BAND_ROWS + CHUNK
FF_HIDDEN = ((8 * D_MODEL // 3 + 255) // 256) * 256

SEG = {
    "gate": (0, 4096), "c_b": (4096, 512), "c_c": (4608, 512), "c_x": (5120, 512),
    "d_q": (5632, 512), "d_k": (6144, 512), "d_v": (6656, 512),
    "a_z": (7168, 512), "a_x": (7680, 512), "a_b": (8192, 256), "a_c": (8448, 256),
    "b_kvl": (8704, 256), "a_dt": (8960, 128), "b_kpe": (9088, 128), "b_ql": (9216, 384),
}
PROJ_COLS = 9600
KPE_LANE = 64


def _cparams(*sem):
    return pltpu.CompilerParams(dimension_semantics=sem, vmem_limit_bytes=V7X_VMEM_LIMIT_BYTES)


def _mm(a, b):
    return jnp.dot(a.astype(MXU_DTYPE), b.astype(MXU_DTYPE), preferred_element_type=F32)


def _mm_nt(a, b):
    return lax.dot_general(a.astype(MXU_DTYPE), b.astype(MXU_DTYPE), (((1,), (1,)), ((), ())),
                           preferred_element_type=F32)


def _mm_tn(a, b):
    return lax.dot_general(a.astype(MXU_DTYPE), b.astype(MXU_DTYPE), (((0,), (0,)), ((), ())),
                           preferred_element_type=F32)


def _split3(x):
    hi = x.astype(MXU_DTYPE)
    r1 = x - hi.astype(F32)
    mid = r1.astype(MXU_DTYPE)
    lo = (r1 - mid.astype(F32)).astype(MXU_DTYPE)
    return hi, mid, lo


def _mm_exact_rhs01(x, sel):
    hi, mid, lo = _split3(x)
    sel = sel.astype(MXU_DTYPE)
    d = functools.partial(jnp.dot, preferred_element_type=F32)
    return d(hi, sel) + d(mid, sel) + d(lo, sel)


def _silu(x):
    return x * jax.nn.sigmoid(x)


def _row_rms(x):
    return x * lax.rsqrt(jnp.mean(x * x, axis=-1, keepdims=True) + EPS)


def _lane_iota(rows):
    return lax.broadcasted_iota(jnp.int32, (rows, V7X_LANES), 1)


def _head64_rms(x):
    rows, width = x.shape
    low = _lane_iota(rows) < 64
    out = []
    for j in range(width // V7X_LANES):
        blk = x[:, j * V7X_LANES:(j + 1) * V7X_LANES]
        sq = blk * blk
        s_lo = jnp.sum(jnp.where(low, sq, 0.0), axis=-1, keepdims=True) * (1.0 / 64)
        s_hi = jnp.sum(jnp.where(low, 0.0, sq), axis=-1, keepdims=True) * (1.0 / 64)
        out.append(blk * jnp.where(low, lax.rsqrt(s_lo + EPS), lax.rsqrt(s_hi + EPS)))
    return jnp.concatenate(out, axis=-1)


def _rmsnorm_kernel(x_ref, g_ref, o_ref):
    o_ref[...] = (_row_rms(x_ref[...]) * g_ref[...]).astype(o_ref.dtype)


def rmsnorm_cast(x, g, tm):
    n, d = x.shape
    return pl.pallas_call(
        _rmsnorm_kernel, out_shape=jax.ShapeDtypeStruct((n, d), MXU_DTYPE), grid=(n // tm,),
        in_specs=[pl.BlockSpec((tm, d), lambda i: (i, 0)), pl.BlockSpec((1, d), lambda i: (0, 0))],
        out_specs=pl.BlockSpec((tm, d), lambda i: (i, 0)),
        compiler_params=_cparams("parallel"), name="rmsnorm_cast")(x, g.reshape(1, d))


def _matmul_nt_kernel(a_ref, wt_ref, o_ref):
    o_ref[...] = _mm_nt(a_ref[...], wt_ref[...])


def matmul_nt(a, wt, tm, tn):
    n, k = a.shape
    c = wt.shape[0]
    return pl.pallas_call(
        _matmul_nt_kernel, out_shape=jax.ShapeDtypeStruct((n, c), F32), grid=(c // tn, n // tm),
        in_specs=[pl.BlockSpec((tm, k), lambda j, i: (i, 0)), pl.BlockSpec((tn, k), lambda j, i: (j, 0))],
        out_specs=pl.BlockSpec((tm, tn), lambda j, i: (i, j)),
        compiler_params=_cparams("parallel", "parallel"), name="in_proj_matmul")(a, wt)


def _merge_kernel(x_ref, g_ref, ya_ref, ob_ref, yc_ref, od_ref, bg_ref, wa_ref, wb_ref, wc_ref, wd_ref, wo_ref, o_ref,
                  *, bd_transposed):
    def gate(k):
        return jax.nn.sigmoid(g_ref[:, k * D_MODEL:(k + 1) * D_MODEL] + bg_ref[:, k * D_MODEL:(k + 1) * D_MODEL])

    def attn_out(a_ref, w_ref):
        return _mm_tn(a_ref[0], w_ref[...]) if bd_transposed else _mm(a_ref[...], w_ref[...])

    merged = gate(0) * _mm(ya_ref[...], wa_ref[...])
    merged = merged + gate(1) * attn_out(ob_ref, wb_ref)
    merged = merged + gate(2) * _mm(yc_ref[...], wc_ref[...])
    merged = merged + gate(3) * attn_out(od_ref, wd_ref)
    o_ref[...] = x_ref[...] + _mm(merged, wo_ref[...])


def merge(x, proj, ya, ob, yc, od, b_gate, wa, wb, wc, wd, wo, tm):
    n, d = x.shape
    row = lambda w: pl.BlockSpec((tm, w), lambda i: (i, 0))
    full = lambda a: pl.BlockSpec(a.shape, lambda i: (0, 0))
    bd_transposed = ob.ndim == 3
    if bd_transposed:
        per_seq = ob.shape[2] // tm
        attn = pl.BlockSpec((1, ob.shape[1], tm), lambda i: (i // per_seq, 0, i % per_seq))
    else:
        attn = row(512)
    bg = b_gate.reshape(1, N_BRANCH * d)
    return pl.pallas_call(
        functools.partial(_merge_kernel, bd_transposed=bd_transposed),
        out_shape=jax.ShapeDtypeStruct((n, d), F32), grid=(n // tm,),
        in_specs=[row(d), pl.BlockSpec((tm, N_BRANCH * d), lambda i: (i, 0)), row(512), attn, row(512), attn,
                  full(bg), full(wa), full(wb), full(wc), full(wd), full(wo)],
        out_specs=row(d), compiler_params=_cparams("parallel"), name="merge")(
            x, proj, ya, ob, yc, od, bg, wa, wb, wc, wd, wo)


def _ffn_kernel(x_ref, g_ref, wg_ref, wv_ref, wd_ref, o_ref, h_s, acc_s):
    k = pl.program_id(1)

    @pl.when(k == 0)
    def _():
        h_s[...] = (_row_rms(x_ref[...]) * g_ref[...]).astype(h_s.dtype)
        acc_s[...] = jnp.zeros_like(acc_s)

    h = h_s[...]
    gate = jnp.dot(h, wg_ref[...], preferred_element_type=F32)
    val = jnp.dot(h, wv_ref[...], preferred_element_type=F32)
    acc_s[...] += _mm(_silu(gate) * val, wd_ref[...])

    @pl.when(k == pl.num_programs(1) - 1)
    def _():
        o_ref[...] = x_ref[...] + acc_s[...]


def ffn(x, g, w_up, w_down, tm, tk):
    n, d = x.shape
    nk = FF_HIDDEN // tk
    return pl.pallas_call(
        _ffn_kernel, out_shape=jax.ShapeDtypeStruct((n, d), F32), grid=(n // tm, nk),
        in_specs=[pl.BlockSpec((tm, d), lambda i, k: (i, 0)), pl.BlockSpec((1, d), lambda i, k: (0, 0)),
                  pl.BlockSpec((d, tk), lambda i, k: (0, k)), pl.BlockSpec((d, tk), lambda i, k: (0, nk + k)),
                  pl.BlockSpec((tk, d), lambda i, k: (k, 0))],
        out_specs=pl.BlockSpec((tm, d), lambda i, k: (i, 0)),
        scratch_shapes=[pltpu.VMEM((tm, d), MXU_DTYPE), pltpu.VMEM((tm, d), F32)],
        compiler_params=_cparams("parallel", "arbitrary"), name="ffn")(x, g.reshape(1, d), w_up, w_up, w_down)


def _ssd_kernel(z_ref, x_ref, b_ref, c_ref, dt_ref, conv0_ref, h0_ref, cw_ref, cbias_ref, dtb_ref, alog_ref,
                dfull_ref, ng_ref, tri_ref, expand_ref, y_ref, convn_ref, hn_ref, xp_s, h_s):
    L = CHUNK
    rows = z_ref.shape[0]
    c = pl.program_id(1)

    @pl.when(c == 0)
    def _():
        xp_s[0:8, :] = conv0_ref[0]
        h_s[...] = h0_ref[0]

    xp_s[8:8 + rows, 0:512] = x_ref[...]
    xp_s[8:8 + rows, 512:768] = b_ref[...]
    xp_s[8:8 + rows, 768:1024] = c_ref[...]
    base = 8 - (SSD_CONV - 1)
    acc = xp_s[base:base + rows, :] * cw_ref[0:1, :]
    for i in range(1, SSD_CONV):
        acc = acc + xp_s[base + i:base + i + rows, :] * cw_ref[i:i + 1, :]
    xbc = _silu(acc + cbias_ref[...])
    tail = xp_s[rows:rows + 8, :]
    convn_ref[0] = tail
    xp_s[0:8, :] = tail

    xs = xbc[:, 0:512]
    dtr = dt_ref[...] + dtb_ref[...]
    dt = jnp.maximum(dtr, 0.0) + jnp.log1p(jnp.exp(-jnp.abs(dtr)))
    dt_full = _mm_exact_rhs01(dt, expand_ref[...])
    da_full = dt_full * (-jnp.exp(alog_ref[...]))
    da_parts = _split3(da_full)
    dot = functools.partial(jnp.dot, preferred_element_type=F32)
    tri = tri_ref[...].astype(MXU_DTYPE)
    acs = dot(tri, da_parts[0]) + dot(tri, da_parts[1]) + dot(tri, da_parts[2])
    x_dt = xs * dt_full
    exp_acs = jnp.exp(acs)

    row = lax.broadcasted_iota(jnp.int32, (L, V7X_LANES), 0)
    lane = _lane_iota(L)
    lane_s = jnp.where(lane < 64, lane, lane - 64)
    diag2 = row == lane_s
    tril2 = lane_s <= row
    row2 = lax.broadcasted_iota(jnp.int32, (2 * L, V7X_LANES), 0)
    lane2 = lax.broadcasted_iota(jnp.int32, (2 * L, V7X_LANES), 1)
    blockdiag = (row2 < L) == (lane2 < 64)
    ones_ln = jnp.ones((L, SSD_STATE), MXU_DTYPE)
    dot_tn = functools.partial(lax.dot_general, dimension_numbers=(((0,), (0,)), ((), ())), preferred_element_type=F32)

    h = [h_s[256 * g:256 * (g + 1), :] for g in range(SSD_GROUPS)]
    y_chunks = []
    for k in range(rows // L):
        r = slice(k * L, (k + 1) * L)
        acs_k = acs[r]
        xd_end = x_dt[r] * jnp.exp(acs_k[L - 1:L, :] - acs_k)
        y_parts = []
        for g in range(SSD_GROUPS):
            cols = slice(256 * g, 256 * (g + 1))
            bg = xbc[r, 512 + 128 * g:512 + 128 * (g + 1)]
            cg = xbc[r, 768 + 128 * g:768 + 128 * (g + 1)]
            cb2 = _mm_nt(cg, jnp.concatenate([bg, bg], axis=0))
            y_off = _mm_nt(cg, h[g]) * exp_acs[r, cols]
            for jp in range(2):
                sl = slice(128 * (2 * g + jp), 128 * (2 * g + jp + 1))
                blk = acs_k[:, sl]
                at_s = jnp.sum(jnp.where(diag2, blk, 0.0), axis=0, keepdims=True)
                decay = jnp.where(tril2, jnp.exp(jnp.minimum(blk - at_s, 0.0)), 0.0)
                xpair = x_dt[r, sl]
                xblk = jnp.where(blockdiag, jnp.concatenate([xpair, xpair], axis=0), 0.0)
                y_parts.append(_mm(cb2 * decay, xblk) + y_off[:, 128 * jp:128 * (jp + 1)])
            state = _mm_tn(xd_end[:, cols], bg)
            total = sum(dot_tn(part[r, cols], ones_ln) for part in da_parts)
            h[g] = jnp.exp(total) * h[g] + state
        y_chunks.append(jnp.concatenate(y_parts, axis=-1))
    y = jnp.concatenate(y_chunks, axis=0) + dfull_ref[...] * xs
    gated = y * _silu(z_ref[...])
    y_ref[...] = (_row_rms(gated) * ng_ref[...]).astype(y_ref.dtype)
    for g in range(SSD_GROUPS):
        h_s[256 * g:256 * (g + 1), :] = h[g]
        hn_ref[0, 256 * g:256 * (g + 1), :] = h[g]


def ssd_branch(proj, bsz, t, conv_prev, ssm_prev, p):
    rows = math.gcd(t, 4 * CHUNK)
    nb = t // rows
    n = bsz * t
    col = lambda name: SEG[name][0] // SEG[name][1]
    seg = lambda name: pl.BlockSpec((rows, SEG[name][1]), lambda b, c, _j=col(name): (b * nb + c, _j))
    full = lambda a: pl.BlockSpec(a.shape, lambda b, c: (0,) * a.ndim)
    conv0 = jnp.pad(conv_prev, ((0, 0), (8 - (SSD_CONV - 1), 0), (0, 0)))
    h0 = ssm_prev.reshape(bsz, SSD_INNER, SSD_STATE)
    tri = jnp.kron(jnp.eye(rows // CHUNK, dtype=F32), jnp.tril(jnp.ones((CHUNK, CHUNK), F32)))
    consts = [p["ssd_conv_w"], p["ssd_conv_b"], p["ssd_dt_bias"], p["ssd_a_log"], p["ssd_d"], p["ssd_norm_g"], tri,
              p["ssd_expand"]]
    y, convn, hn = pl.pallas_call(
        _ssd_kernel,
        out_shape=[jax.ShapeDtypeStruct((n, SSD_INNER), MXU_DTYPE), jax.ShapeDtypeStruct((bsz, 8, SSD_CONV_DIM), F32),
                   jax.ShapeDtypeStruct((bsz, SSD_INNER, SSD_STATE), F32)],
        grid=(bsz, nb),
        in_specs=[seg("a_z"), seg("a_x"), seg("a_b"), seg("a_c"), seg("a_dt"),
                  pl.BlockSpec((1, 8, SSD_CONV_DIM), lambda b, c: (b, 0, 0)),
                  pl.BlockSpec((1, SSD_INNER, SSD_STATE), lambda b, c: (b, 0, 0))] + [full(a) for a in consts],
        out_specs=[pl.BlockSpec((rows, SSD_INNER), lambda b, c: (b * nb + c, 0)),
                   pl.BlockSpec((1, 8, SSD_CONV_DIM), lambda b, c: (b, 0, 0)),
                   pl.BlockSpec((1, SSD_INNER, SSD_STATE), lambda b, c: (b, 0, 0))],
        scratch_shapes=[pltpu.VMEM((rows + 8, SSD_CONV_DIM), F32), pltpu.VMEM((SSD_INNER, SSD_STATE), F32)],
        compiler_params=_cparams("parallel", "arbitrary"), name="ssd_branch")(
            proj, proj, proj, proj, proj, conv0, h0, *consts)
    return y, hn.reshape(bsz, SSD_HEADS, SSD_HEAD_DIM, SSD_STATE), convn[:, 8 - (SSD_CONV - 1):]


def _sconv_kernel(cb_ref, cc_ref, cx_ref, st0_ref, w_ref, o_ref, stn_ref, up_s):
    rows = cb_ref.shape[0]

    @pl.when(pl.program_id(1) == 0)
    def _():
        up_s[0:8, :] = st0_ref[0]

    up_s[8:8 + rows, :] = cc_ref[...] * cx_ref[...]
    base = 8 - (SCONV_K - 1)
    uc = up_s[base:base + rows, :] * w_ref[0:1, :]
    for i in range(1, SCONV_K):
        uc = uc + up_s[base + i:base + i + rows, :] * w_ref[i:i + 1, :]
    o_ref[...] = (cb_ref[...] * uc).astype(o_ref.dtype)
    tail = up_s[rows:rows + 8, :]
    stn_ref[0] = tail
    up_s[0:8, :] = tail


def sconv_branch(proj, bsz, t, prev, w):
    rows = min(t, 512)
    nt = t // rows
    col = lambda name: SEG[name][0] // SCONV_WIDTH
    seg = lambda name: pl.BlockSpec((rows, SCONV_WIDTH), lambda b, c, _j=col(name): (b * nt + c, _j))
    st0 = jnp.pad(prev, ((0, 0), (8 - (SCONV_K - 1), 0), (0, 0)))
    o, stn = pl.pallas_call(
        _sconv_kernel,
        out_shape=[jax.ShapeDtypeStruct((bsz * t, SCONV_WIDTH), MXU_DTYPE), jax.ShapeDtypeStruct((bsz, 8, SCONV_WIDTH), F32)],
        grid=(bsz, nt),
        in_specs=[seg("c_b"), seg("c_c"), seg("c_x"), pl.BlockSpec((1, 8, SCONV_WIDTH), lambda b, c: (b, 0, 0)),
                  pl.BlockSpec(w.shape, lambda b, c: (0, 0))],
        out_specs=[pl.BlockSpec((rows, SCONV_WIDTH), lambda b, c: (b * nt + c, 0)),
                   pl.BlockSpec((1, 8, SCONV_WIDTH), lambda b, c: (b, 0, 0))],
        scratch_shapes=[pltpu.VMEM((rows + 8, SCONV_WIDTH), F32)],
        compiler_params=_cparams("parallel", "arbitrary"), name="sconv_branch")(proj, proj, proj, st0, w)
    return o, stn[:, 8 - (SCONV_K - 1):]


def _rope(y, cos_t, sin_a, sin_b):
    return y * cos_t + pltpu.roll(y, 16, 1) * sin_a + pltpu.roll(y, V7X_LANES - 16, 1) * sin_b


def _mla_prep_kernel(ql_ref, kvl_ref, kpe_ref, cos_ref, sa_ref, sb_ref, gq_ref, wq_ref, gqh_ref, gkv_ref, gkr_ref,
                     q_ref, ckv_ref, kpeo_ref):
    rows = ql_ref.shape[0]
    cos_t, sin_a, sin_b = cos_ref[...], sa_ref[...], sb_ref[...]
    q = _mm(_row_rms(ql_ref[...]) * gq_ref[...], wq_ref[...])
    lane = _lane_iota(rows)
    nope = lane < MLA_NOPE
    scale = (MLA_NOPE + MLA_ROPE) ** -0.5 * math.log2(math.e)
    for h in range(MLA_HEADS):
        sl = slice(h * V7X_LANES, (h + 1) * V7X_LANES)
        blk = q[:, sl]
        sq = blk * blk
        s_n = jnp.sum(jnp.where(nope, sq, 0.0), axis=-1, keepdims=True) * (1.0 / MLA_NOPE)
        s_r = jnp.sum(jnp.where(nope, 0.0, sq), axis=-1, keepdims=True) * (1.0 / MLA_ROPE)
        y = blk * jnp.where(nope, lax.rsqrt(s_n + EPS), lax.rsqrt(s_r + EPS)) * gqh_ref[:, sl]
        q_ref[:, sl] = (_rope(y, cos_t, sin_a, sin_b) * scale).astype(q_ref.dtype)
    ckv_ref[...] = _row_rms(kvl_ref[...]) * gkv_ref[...]
    kp = kpe_ref[...]
    ms = jnp.sum(kp * kp, axis=-1, keepdims=True) * (1.0 / MLA_ROPE)
    kpeo_ref[...] = _rope(kp * lax.rsqrt(ms + EPS) * gkr_ref[...], cos_t, sin_a, sin_b)


def mla_prep(proj, n, tm, tabs, p):
    cos_t, sin_a, sin_b = tabs
    ntab = cos_t.shape[0] // tm
    seg = lambda name: pl.BlockSpec((tm, SEG[name][1]), lambda i, _j=SEG[name][0] // SEG[name][1]: (i, _j))
    tab = pl.BlockSpec((tm, V7X_LANES), lambda i: (i % ntab, 0))
    full = lambda a: pl.BlockSpec(a.shape, lambda i: (0, 0))
    consts = [p["mla_q_norm_g"], p["mla_wq"], p["mla_gq_head"], p["mla_kv_norm_g"], p["mla_gkr"]]
    return pl.pallas_call(
        _mla_prep_kernel,
        out_shape=[jax.ShapeDtypeStruct((n, MLA_HEADS * V7X_LANES), MXU_DTYPE),
                   jax.ShapeDtypeStruct((n, MLA_KV_RANK), F32), jax.ShapeDtypeStruct((n, V7X_LANES), F32)],
        grid=(n // tm,),
        in_specs=[seg("b_ql"), seg("b_kvl"), seg("b_kpe"), tab, tab, tab] + [full(a) for a in consts],
        out_specs=[pl.BlockSpec((tm, MLA_HEADS * V7X_LANES), lambda i: (i, 0)),
                   pl.BlockSpec((tm, MLA_KV_RANK), lambda i: (i, 0)), pl.BlockSpec((tm, V7X_LANES), lambda i: (i, 0))],
        compiler_params=_cparams("parallel"), name="mla_prep")(proj, proj, proj, cos_t, sin_a, sin_b, *consts)


def _mla_kv_kernel(ckv_ref, kpe_ref, wk_ref, wvt_ref, gk_ref, k_ref, vt_ref):
    c = ckv_ref[0].astype(MXU_DTYPE)
    kk = jnp.dot(c, wk_ref[...], preferred_element_type=F32)
    vt_ref[0, 0] = _mm_nt(wvt_ref[...], c).astype(vt_ref.dtype)
    kpe = kpe_ref[0]
    for h in range(MLA_HEADS):
        sl = slice(h * V7X_LANES, (h + 1) * V7X_LANES)
        blk = kk[:, sl]
        ms = jnp.sum(blk * blk, axis=-1, keepdims=True) * (1.0 / MLA_NOPE)
        k_ref[0, :, sl] = (blk * lax.rsqrt(ms + EPS) * gk_ref[...] + kpe).astype(k_ref.dtype)


def mla_kv(ckv_all, kpe_all, ts, p):
    bsz, s, _ = ckv_all.shape
    full = lambda a: pl.BlockSpec(a.shape, lambda b, i: (0, 0))
    consts = [p["mla_wk"], p["mla_wvt"], p["mla_gk"]]
    return pl.pallas_call(
        _mla_kv_kernel,
        out_shape=[jax.ShapeDtypeStruct((bsz, s, MLA_HEADS * V7X_LANES), MXU_DTYPE),
                   jax.ShapeDtypeStruct((bsz, s // ts, MLA_WIDTH, ts), MXU_DTYPE)],
        grid=(bsz, s // ts),
        in_specs=[pl.BlockSpec((1, ts, MLA_KV_RANK), lambda b, i: (b, i, 0)),
                  pl.BlockSpec((1, ts, V7X_LANES), lambda b, i: (b, i, 0))] + [full(a) for a in consts],
        out_specs=[pl.BlockSpec((1, ts, MLA_HEADS * V7X_LANES), lambda b, i: (b, i, 0)),
                   pl.BlockSpec((1, 1, MLA_WIDTH, ts), lambda b, i: (b, i, 0, 0))],
        compiler_params=_cparams("parallel", "parallel"), name="mla_kv")(ckv_all, kpe_all, *consts)


def _mla_kv_cached_kernel(ckv_old_ref, ckv_new_ref, kpe_old_ref, kpe_new_ref, wk_ref, wv_ref, gk_ref, k_ref, v_ref):
    c = jnp.concatenate([ckv_old_ref[0, 0], ckv_new_ref[0]], axis=0).astype(MXU_DTYPE)
    old = kpe_old_ref[0, 0]
    tail = V7X_LANES - KPE_LANE - MLA_ROPE
    old = jnp.concatenate([jnp.zeros((old.shape[0], KPE_LANE), F32), old, jnp.zeros((old.shape[0], tail), F32)], axis=1)
    kpe = jnp.concatenate([old, kpe_new_ref[0]], axis=0)
    kk = jnp.dot(c, wk_ref[...], preferred_element_type=F32)
    v_ref[0] = jnp.dot(c, wv_ref[...], preferred_element_type=F32).astype(v_ref.dtype)
    for h in range(MLA_HEADS):
        sl = slice(h * V7X_LANES, (h + 1) * V7X_LANES)
        blk = kk[:, sl]
        ms = jnp.sum(blk * blk, axis=-1, keepdims=True) * (1.0 / MLA_NOPE)
        k_ref[0, :, sl] = (blk * lax.rsqrt(ms + EPS) * gk_ref[...] + kpe).astype(k_ref.dtype)


def mla_kv_cached(ckv_cache, kpe_cache, layer, ckv_new, kpe_new, p):
    _, bsz, past, _ = ckv_cache.shape
    t = ckv_new.shape[1]
    s = past + t
    full = lambda a: pl.BlockSpec(a.shape, lambda b: (0, 0))
    consts = [p["mla_wk"], p["mla_wv"], p["mla_gk"]]
    return pl.pallas_call(
        _mla_kv_cached_kernel,
        out_shape=[jax.ShapeDtypeStruct((bsz, s, MLA_HEADS * V7X_LANES), MXU_DTYPE),
                   jax.ShapeDtypeStruct((bsz, s, MLA_WIDTH), MXU_DTYPE)],
        grid=(bsz,),
        in_specs=[pl.BlockSpec((1, 1, past, MLA_KV_RANK), lambda b: (layer, b, 0, 0)),
                  pl.BlockSpec((1, t, MLA_KV_RANK), lambda b: (b, 0, 0)),
                  pl.BlockSpec((1, 1, past, MLA_ROPE), lambda b: (layer, b, 0, 0)),
                  pl.BlockSpec((1, t, V7X_LANES), lambda b: (b, 0, 0))] + [full(a) for a in consts],
        out_specs=[pl.BlockSpec((1, s, MLA_HEADS * V7X_LANES), lambda b: (b, 0, 0)),
                   pl.BlockSpec((1, s, MLA_WIDTH), lambda b: (b, 0, 0))],
        compiler_params=_cparams("parallel"), name="mla_kv_cached")(ckv_cache, ckv_new, kpe_cache, kpe_new, *consts)


def _mla_attn_rows_kernel(q_ref, k_ref, v_ref, o_ref, *, q_off):
    t, s = q_ref.shape[1], k_ref.shape[1]
    low = _lane_iota(t) < MLA_V
    needs_mask = (s - 1) // CHUNK > q_off // CHUNK
    if needs_mask:
        q_chunk = (q_off + lax.broadcasted_iota(jnp.int32, (t, s), 0)) // CHUNK
        visible = lax.broadcasted_iota(jnp.int32, (t, s), 1) // CHUNK <= q_chunk
    for hp in range(MLA_HEADS // 2):
        v_pair = v_ref[0, :, hp * V7X_LANES:(hp + 1) * V7X_LANES]
        outs = []
        for sub in range(2):
            sl = slice((2 * hp + sub) * V7X_LANES, (2 * hp + sub + 1) * V7X_LANES)
            sc = _mm_nt(q_ref[0, :, sl], k_ref[0, :, sl])
            if needs_mask:
                sc = jnp.where(visible, sc, NEG_BIG)
            e = jnp.exp2(sc - jnp.max(sc, axis=-1, keepdims=True))
            outs.append(_mm(e, v_pair) / jnp.sum(e, axis=-1, keepdims=True))
        o_ref[0, :, hp * V7X_LANES:(hp + 1) * V7X_LANES] = jnp.where(low, outs[0], outs[1]).astype(o_ref.dtype)


def mla_attention_rows(q, k, v, q_off):
    bsz, t, _ = q.shape
    s = k.shape[1]
    return pl.pallas_call(
        functools.partial(_mla_attn_rows_kernel, q_off=q_off),
        out_shape=jax.ShapeDtypeStruct((bsz, t, MLA_WIDTH), MXU_DTYPE), grid=(bsz,),
        in_specs=[pl.BlockSpec((1, t, MLA_HEADS * V7X_LANES), lambda b: (b, 0, 0)),
                  pl.BlockSpec((1, s, MLA_HEADS * V7X_LANES), lambda b: (b, 0, 0)),
                  pl.BlockSpec((1, s, MLA_WIDTH), lambda b: (b, 0, 0))],
        out_specs=pl.BlockSpec((1, t, MLA_WIDTH), lambda b: (b, 0, 0)),
        compiler_params=_cparams("parallel"), name="mla_attention_rows")(q, k, v)


def _mla_attn_kernel(q_ref, k_ref, vt_ref, ot_ref, sa_s, sb_s, *, tq, tkc, q_off, nh):
    q_start = q_off + pl.program_id(2) * tq
    n_full = (q_start + CHUNK) // tkc
    n_total = (q_start + tq + tkc - 1) // tkc
    lanes = lambda h: slice(h * V7X_LANES, (h + 1) * V7X_LANES)
    qs = [q_ref[0, :, lanes(h)] for h in range(nh)]
    ones = jnp.ones((8, tkc), MXU_DTYPE)

    def produce(c, buf):
        k0 = pl.multiple_of(c * tkc, tkc)
        for h in range(nh):
            buf[h] = _mm_nt(k_ref[0, pl.ds(k0, tkc), lanes(h)], qs[h])

    def consume(c, buf, stats, masked):
        if masked:
            k_chunk = (c * tkc + lax.broadcasted_iota(jnp.int32, (tkc, tq), 0)) // CHUNK
            q_chunk = (q_start + lax.broadcasted_iota(jnp.int32, (tkc, tq), 1)) // CHUNK
            visible = k_chunk <= q_chunk
        out = []
        for h in range(nh):
            m, acc = stats[h]
            s = jnp.where(visible, buf[h], NEG_BIG) if masked else buf[h]
            m_new = jnp.maximum(m, jnp.max(s, axis=0, keepdims=True))
            alpha = jnp.exp2(m - m_new)
            p = jnp.exp2(s - m_new).astype(MXU_DTYPE)
            v_ext = jnp.concatenate([vt_ref[0, c, h * MLA_V:(h + 1) * MLA_V, :], ones], axis=0)
            out.append((m_new, alpha * acc + jnp.dot(v_ext, p, preferred_element_type=F32)))
        return tuple(out)

    def pair(i, stats):
        c = 2 * i
        produce(c + 1, sb_s)
        stats = consume(c, sa_s, stats, False)
        produce(jnp.minimum(c + 2, n_total - 1), sa_s)
        return consume(c + 1, sb_s, stats, False)

    def single(c, stats):
        produce(c, sb_s)
        return consume(c, sb_s, stats, True)

    init = tuple((jnp.full((1, tq), NEG_BIG, F32), jnp.zeros((MLA_V + 8, tq), F32)) for _ in range(nh))
    produce(0, sa_s)
    n_pairs = n_full // 2
    stats = lax.fori_loop(0, n_pairs, pair, init)
    stats = consume(2 * n_pairs, sa_s, stats, True)
    stats = lax.fori_loop(2 * n_pairs + 1, n_total, single, stats)
    for h in range(nh):
        acc = stats[h][1]
        ot_ref[0, h * MLA_V:(h + 1) * MLA_V, :] = (acc[:MLA_V] / acc[MLA_V:MLA_V + 1]).astype(ot_ref.dtype)


def mla_attention(q, k, vt, q_off, tq, nh):
    bsz, t, _ = q.shape
    _, nchunk, _, tkc = vt.shape
    s = k.shape[1]
    kern = functools.partial(_mla_attn_kernel, tq=tq, tkc=tkc, q_off=q_off, nh=nh)
    return pl.pallas_call(
        kern, out_shape=jax.ShapeDtypeStruct((bsz, MLA_WIDTH, t), MXU_DTYPE),
        grid=(bsz, MLA_HEADS // nh, t // tq),
        in_specs=[pl.BlockSpec((1, tq, nh * V7X_LANES), lambda b, h, qi: (b, qi, h)),
                  pl.BlockSpec((1, s, nh * V7X_LANES), lambda b, h, qi: (b, 0, h), pipeline_mode=pl.Buffered(1)),
                  pl.BlockSpec((1, nchunk, nh * MLA_V, tkc), lambda b, h, qi: (b, 0, h, 0),
                               pipeline_mode=pl.Buffered(1))],
        out_specs=pl.BlockSpec((1, nh * MLA_V, tq), lambda b, h, qi: (b, h, qi)),
        scratch_shapes=[pltpu.VMEM((nh, tkc, tq), F32), pltpu.VMEM((nh, tkc, tq), F32)],
        compiler_params=_cparams("parallel", "parallel", "parallel"), name="mla_attention")(q, k, vt)


def _band_kernel(q_ref, ka_ref, kb_ref, va_ref, vb_ref, bias_ref, gq_ref, gk_ref, ot_ref, kn_ref, q_s, k_s, vt_s,
                 *, a_is_cache):
    tb = q_ref.shape[0]
    rows_q = q_s.shape[0]
    pad = rows_q - tb
    qn = _head64_rms(q_ref[...]) * gq_ref[...] * (BAND_HEAD_DIM ** -0.5 * math.log2(math.e))
    kb = _head64_rms(kb_ref[...]) * gk_ref[...]
    kn_ref[...] = kb
    if a_is_cache:
        def heads_to_lanes(ref):
            per_head = [ref[0, 0, pl.ds(h, BAND_ROWS, stride=BAND_HEADS), :] for h in range(BAND_HEADS)]
            return jnp.concatenate(per_head, axis=1)

        ka, va = heads_to_lanes(ka_ref), heads_to_lanes(va_ref)
    else:
        ka, va = _head64_rms(ka_ref[...]) * gk_ref[...], va_ref[...]
    vb = vb_ref[...]
    if pad:
        zeros = jnp.zeros((pad, BAND_WIDTH), F32)
        qn, kb, vb = (jnp.concatenate([a, zeros], axis=0) for a in (qn, kb, vb))
    q_s[...] = qn.astype(q_s.dtype)
    k_s[0:BAND_ROWS, :] = ka.astype(k_s.dtype)
    k_s[BAND_ROWS:, :] = kb.astype(k_s.dtype)
    vt_s[:, 0:BAND_ROWS] = va.T.astype(vt_s.dtype)
    vt_s[:, BAND_ROWS:] = vb.T.astype(vt_s.dtype)

    win = BAND_ROWS + V7X_LANES
    low = _lane_iota(V7X_LANES) < BAND_HEAD_DIM
    ones = jnp.ones((8, win), MXU_DTYPE)
    win_row = lax.broadcasted_iota(jnp.int32, (win, V7X_LANES), 0)

    def tiles(first):
        for j in range(rows_q // V7X_LANES):
            r0 = j * V7X_LANES
            qj = q_s[r0:r0 + V7X_LANES, :]
            kw = k_s[r0:r0 + win, :]
            ss = []
            for hp in range(BAND_HEADS // 2):
                sl = slice(hp * V7X_LANES, (hp + 1) * V7X_LANES)
                zero = jnp.zeros((), q_s.dtype)
                q2 = jnp.concatenate([jnp.where(low, qj[:, sl], zero), jnp.where(low, zero, qj[:, sl])], axis=0)
                ss.append(_mm_nt(kw[:, sl], q2))
            for hp in range(BAND_HEADS // 2):
                sl = slice(hp * V7X_LANES, (hp + 1) * V7X_LANES)
                v_ext = jnp.concatenate([vt_s[sl, r0:r0 + win], ones], axis=0)
                s = ss[hp] + jnp.concatenate([bias_ref[2 * hp], bias_ref[2 * hp + 1]], axis=1)
                if first:
                    s = jnp.where(jnp.concatenate([win_row, win_row], axis=1) + r0 < BAND_ROWS, NEG_BIG, s)
                p = jnp.exp2(s - jnp.max(s, axis=0, keepdims=True)).astype(MXU_DTYPE)
                res = jnp.dot(v_ext, p, preferred_element_type=F32)
                den = res[V7X_LANES:V7X_LANES + 1]
                halves = [res[0:BAND_HEAD_DIM, 0:V7X_LANES] / den[:, 0:V7X_LANES],
                          res[BAND_HEAD_DIM:V7X_LANES, V7X_LANES:] / den[:, V7X_LANES:]]
                o_pair = jnp.concatenate(halves, axis=0).astype(ot_ref.dtype)
                if pad:
                    ot_ref[0, sl, :] = o_pair[:, 0:tb]
                else:
                    ot_ref[0, sl, r0:r0 + V7X_LANES] = o_pair

    if a_is_cache:
        tiles(False)
    else:
        pl.when(pl.program_id(1) == 0)(lambda: tiles(True))
        pl.when(pl.program_id(1) > 0)(lambda: tiles(False))


def band_branch(q_src, k_prev, v_prev, layer, bsz, t, bias, gq, gk):
    a_is_cache = k_prev is not None
    tb = min(t, BAND_ROWS)
    nt = t // tb
    rows_q = -(-tb // V7X_LANES) * V7X_LANES
    col = lambda name: SEG[name][0] // BAND_WIDTH
    cur = lambda name: pl.BlockSpec((tb, BAND_WIDTH), lambda b, i, _j=col(name): (b * nt + i, _j))
    if a_is_cache:
        prev = lambda name: pl.BlockSpec((1, 1, BAND_ROWS * BAND_HEADS, BAND_HEAD_DIM), lambda b, i: (layer, b, 0, 0))
        ka, va = k_prev, v_prev
    else:
        prev = lambda name: pl.BlockSpec((BAND_ROWS, BAND_WIDTH), lambda b, i, _j=col(name): (b * nt + jnp.maximum(i - 1, 0), _j))
        ka, va = q_src, q_src
    full = lambda a: pl.BlockSpec(a.shape, lambda b, i: (0,) * a.ndim)
    kern = functools.partial(_band_kernel, a_is_cache=a_is_cache)
    return pl.pallas_call(
        kern,
        out_shape=[jax.ShapeDtypeStruct((bsz, BAND_WIDTH, t), MXU_DTYPE), jax.ShapeDtypeStruct((bsz * t, BAND_WIDTH), F32)],
        grid=(bsz, nt),
        in_specs=[cur("d_q"), prev("d_k"), cur("d_k"), prev("d_v"), cur("d_v"), full(bias), full(gq), full(gk)],
        out_specs=[pl.BlockSpec((1, BAND_WIDTH, tb), lambda b, i: (b, 0, i)),
                   pl.BlockSpec((tb, BAND_WIDTH), lambda b, i: (b * nt + i, 0))],
        scratch_shapes=[pltpu.VMEM((rows_q, BAND_WIDTH), MXU_DTYPE), pltpu.VMEM((BAND_ROWS + rows_q, BAND_WIDTH), MXU_DTYPE),
                        pltpu.VMEM((BAND_WIDTH, BAND_ROWS + rows_q), MXU_DTYPE)],
        compiler_params=_cparams("parallel", "parallel"), name="band_branch")(
            q_src, ka, q_src, va, q_src, bias, gq, gk)


def _head_blocks(w, per_head, used):
    k = w.shape[0]
    w = w.reshape(k, -1, per_head)[:, :, :used]
    return jnp.pad(w, ((0, 0), (0, 0), (0, V7X_LANES - used))).reshape(k, -1)


def _lane_pad(v, offset, width=V7X_LANES):
    return jnp.pad(v, (offset, width - offset - v.shape[0])).reshape(1, width)


def _layer_params(l, w):
    src = {}
    off = 0
    for name, width in (("a_z", 512), ("a_x", 512), ("a_b", 256), ("a_c", 256), ("a_dt", 8), ("b_ql", 384),
                        ("b_kvl", 256), ("b_kpe", 32), ("c_b", 512), ("c_c", 512), ("c_x", 512),
                        ("d_q", 512), ("d_k", 512), ("d_v", 512), ("gate", 4096)):
        src[name] = (off, width)
        off += width
    w_in_t = w["w_in"][l].T.astype(MXU_DTYPE)
    pieces = []
    for name, (dst, dwidth) in sorted(SEG.items(), key=lambda kv: kv[1][0]):
        s0, sw = src[name]
        lead = KPE_LANE if name == "b_kpe" else 0
        for width in (lead, None, dwidth - sw - lead):
            if width is None:
                pieces.append(w_in_t[s0:s0 + sw])
            elif width:
                pieces.append(jnp.zeros((width, D_MODEL), MXU_DTYPE))
    p = {"w_in_t": jnp.concatenate(pieces, axis=0)}
    cast = lambda a: a.astype(MXU_DTYPE)
    row = lambda a: a.reshape(1, -1)
    p["norm_mix_g"], p["norm_ffn_g"], p["b_gate"] = w["norm_mix_g"][l], w["norm_ffn_g"][l], w["b_gate"][l]
    for name in ("w_a_out", "w_b_out", "w_c_out", "w_d_out", "w_o", "w_ffn_up", "w_ffn_down"):
        p[name] = cast(w[name][l])
    p["ssd_conv_w"] = w["ssd_conv_w"][l]
    p["ssd_conv_b"] = row(w["ssd_conv_b"][l])
    p["ssd_dt_bias"] = _lane_pad(w["ssd_dt_bias"][l], 0)
    p["ssd_a_log"] = row(jnp.repeat(w["ssd_a_log"][l], SSD_HEAD_DIM))
    p["ssd_d"] = row(jnp.repeat(w["ssd_d"][l], SSD_HEAD_DIM))
    p["ssd_norm_g"] = row(w["ssd_norm_g"][l])
    head_of_lane = jnp.arange(SSD_INNER) // SSD_HEAD_DIM
    p["ssd_expand"] = (jnp.arange(V7X_LANES)[:, None] == head_of_lane[None, :]).astype(F32)
    p["mla_q_norm_g"] = row(w["mla_q_norm_g"][l])
    p["mla_wq"] = cast(_head_blocks(w["mla_w_q_up"][l], MLA_NOPE + MLA_ROPE, MLA_NOPE + MLA_ROPE))
    gqh = jnp.concatenate([w["mla_qn_g"][l], w["mla_qr_g"][l], jnp.zeros((V7X_LANES - MLA_NOPE - MLA_ROPE,), F32)])
    p["mla_gq_head"] = row(jnp.tile(gqh, MLA_HEADS))
    p["mla_kv_norm_g"] = row(w["mla_kv_norm_g"][l])
    p["mla_gkr"] = _lane_pad(w["mla_kr_g"][l], KPE_LANE)
    p["mla_wk"] = cast(_head_blocks(w["mla_w_kv_up"][l], MLA_NOPE + MLA_V, MLA_NOPE))
    p["mla_wv"] = cast(w["mla_w_kv_up"][l].reshape(MLA_KV_RANK, MLA_HEADS, MLA_NOPE + MLA_V)[:, :, MLA_NOPE:]
                       .reshape(MLA_KV_RANK, MLA_WIDTH))
    p["mla_wvt"] = p["mla_wv"].T
    p["mla_gk"] = _lane_pad(w["mla_kn_g"][l], 0)
    p["sconv_w"] = w["sconv_w"][l]
    p["band_gq"] = row(jnp.tile(w["band_qn_g"][l], BAND_HEADS))
    p["band_gk"] = row(jnp.tile(w["band_kn_g"][l], BAND_HEADS))
    nkey, nqry = BAND_ROWS + V7X_LANES, V7X_LANES
    key = np.arange(nkey)[:, None]
    qry = np.arange(nqry)[None, :]
    in_band = np.logical_and(key // CHUNK >= qry // CHUNK, key // CHUNK <= qry // CHUNK + BAND_PAST_CHUNKS)
    diff = np.arange(nqry - 1 + BAND_ROWS, -nkey + BAND_ROWS, -1)
    by_diff = w["band_rel_bias"][l][:, np.clip(diff, -REL_CLIP, REL_CLIP) + REL_CLIP]
    skew = jnp.tile(jnp.pad(by_diff, ((0, 0), (0, 1)))[:, None, :], (1, nqry, 1)).reshape(BAND_HEADS, -1)
    skew = skew[:, :nqry * (nkey + nqry - 1)].reshape(BAND_HEADS, nqry, nkey + nqry - 1)
    bias = jnp.swapaxes(skew[:, :, nqry - 1:], 1, 2) * math.log2(math.e)
    p["band_bias"] = jnp.where(in_band[None], bias, NEG_BIG)
    return p


def _rope_tables(pos, rows):
    inv = ROPE_THETA ** (-jnp.arange(0, MLA_ROPE, 2, dtype=F32) / MLA_ROPE)
    ang = pos.astype(F32)[:, None] * inv[None, :]
    cos, sin = jnp.cos(ang), jnp.sin(ang)
    half = MLA_ROPE // 2
    z = lambda w: jnp.zeros((pos.shape[0], w), F32)
    tail = V7X_LANES - MLA_NOPE - MLA_ROPE
    cos_t = jnp.concatenate([jnp.ones((pos.shape[0], MLA_NOPE), F32), cos, cos, z(tail)], axis=1)
    sin_a = jnp.concatenate([z(MLA_NOPE + half), sin, z(tail)], axis=1)
    sin_b = jnp.concatenate([z(MLA_NOPE), -sin, z(half + tail)], axis=1)
    reps = max(rows // pos.shape[0], 1)
    return tuple(jnp.tile(a, (reps, 1)) for a in (cos_t, sin_a, sin_b))


def _layer(x, bsz, t, past, p, tabs, q_off, layer):
    n = bsz * t
    tm = min(n, 512)
    h = rmsnorm_cast(x, p["norm_mix_g"], tm)
    proj = matmul_nt(h, p["w_in_t"], min(n, 1024), 1920)
    ya, ssm_new, ssd_conv_new = ssd_branch(proj, bsz, t, past["ssd_conv"], past["ssm"], p)
    yc, sconv_new = sconv_branch(proj, bsz, t, past["sconv"], p["sconv_w"])
    q, ckv, kpe_pad = mla_prep(proj, n, tm, tabs, p)
    q3, ckv3, kpe3 = q.reshape(bsz, t, -1), ckv.reshape(bsz, t, MLA_KV_RANK), kpe_pad.reshape(bsz, t, V7X_LANES)
    if past["mla_ckv"] is not None:
        assert t <= V7X_LANES
        kk, vv = mla_kv_cached(past["mla_ckv"], past["mla_kpe"], layer, ckv3, kpe3, p)
        ob = mla_attention_rows(q3, kk, vv, q_off).reshape(n, MLA_WIDTH)
    else:
        tkc = next(c for c in (256, 128, t) if t % c == 0)
        kk, vt = mla_kv(ckv3, kpe3, tkc, p)
        ob = mla_attention(q3, kk, vt, q_off, min(t, 512), 8)
    od, kn = band_branch(proj, past["band_k"], past["band_v"], layer, bsz, t, p["band_bias"], p["band_gq"], p["band_gk"])
    if ob.ndim != od.ndim or t % tm:
        ob = ob if ob.ndim == 2 else jnp.swapaxes(ob, 1, 2).reshape(n, MLA_WIDTH)
        od = jnp.swapaxes(od, 1, 2).reshape(n, BAND_WIDTH)
    x = merge(x, proj, ya, ob, yc, od, p["b_gate"], p["w_a_out"], p["w_b_out"], p["w_c_out"], p["w_d_out"], p["w_o"], tm)
    x = ffn(x, p["norm_ffn_g"], p["w_ffn_up"], p["w_ffn_down"], tm, FF_HIDDEN // 2)
    dv0 = SEG["d_v"][0]
    keep = min(BAND_ROWS, t)
    heads = lambda a: a.reshape(bsz, keep, BAND_HEADS, BAND_HEAD_DIM)
    kn = heads(kn.reshape(bsz, t, BAND_WIDTH)[:, t - keep:])
    vn = heads(proj.reshape(bsz, t, PROJ_COLS)[:, t - keep:, dv0:dv0 + BAND_WIDTH])
    new = {"ssm": ssm_new, "ssd_conv": ssd_conv_new, "sconv": sconv_new,
           "mla_ckv": ckv.reshape(bsz, t, MLA_KV_RANK),
           "mla_kpe": kpe_pad[:, KPE_LANE:KPE_LANE + MLA_ROPE].reshape(bsz, t, MLA_ROPE),
           "band_k": kn, "band_v": vn}
    return x, new


def kernel(x_prompt, x_sample, state_ssm, state_ssd_conv, cache_mla_ckv, cache_mla_kpe, state_sconv, cache_band_k, cache_band_v, norm_mix_g, w_in, b_gate, ssd_conv_w, ssd_conv_b, ssd_dt_bias, ssd_a_log, ssd_d, ssd_norm_g, w_a_out, mla_q_norm_g, mla_w_q_up, mla_kv_norm_g, mla_w_kv_up, mla_qn_g, mla_kn_g, mla_qr_g, mla_kr_g, w_b_out, sconv_w, w_c_out, band_qn_g, band_kn_g, band_rel_bias, w_d_out, w_o, norm_ffn_g, w_ffn_up, w_ffn_down):
    weights = dict(norm_mix_g=norm_mix_g, w_in=w_in, b_gate=b_gate, ssd_conv_w=ssd_conv_w, ssd_conv_b=ssd_conv_b,
                   ssd_dt_bias=ssd_dt_bias, ssd_a_log=ssd_a_log, ssd_d=ssd_d, ssd_norm_g=ssd_norm_g, w_a_out=w_a_out,
                   mla_q_norm_g=mla_q_norm_g, mla_w_q_up=mla_w_q_up, mla_kv_norm_g=mla_kv_norm_g,
                   mla_w_kv_up=mla_w_kv_up, mla_qn_g=mla_qn_g, mla_kn_g=mla_kn_g, mla_qr_g=mla_qr_g,
                   mla_kr_g=mla_kr_g, w_b_out=w_b_out, sconv_w=sconv_w, w_c_out=w_c_out, band_qn_g=band_qn_g,
                   band_kn_g=band_kn_g, band_rel_bias=band_rel_bias, w_d_out=w_d_out, w_o=w_o,
                   norm_ffn_g=norm_ffn_g, w_ffn_up=w_ffn_up, w_ffn_down=w_ffn_down)
    depth = w_in.shape[0]
    b_p, t_p, d = x_prompt.shape
    b_s, t_s, _ = x_sample.shape
    past_len = cache_mla_ckv.shape[2]
    assert d == D_MODEL and t_p % BAND_ROWS == 0 and t_s == CHUNK and cache_band_k.shape[2] == BAND_ROWS
    assert past_len % CHUNK == 0
    tabs_p = _rope_tables(jnp.arange(t_p, dtype=jnp.int32), min(b_p * t_p, 512))
    tabs_s = _rope_tables(past_len + jnp.arange(t_s, dtype=jnp.int32), min(b_s * t_s, 512))
    y_p = x_prompt.reshape(b_p * t_p, d)
    y_s = x_sample.reshape(b_s * t_s, d)
    new_p, new_s = [], []
    band_k_rows = cache_band_k.reshape(depth, b_s, BAND_ROWS * BAND_HEADS, BAND_HEAD_DIM)
    band_v_rows = cache_band_v.reshape(depth, b_s, BAND_ROWS * BAND_HEADS, BAND_HEAD_DIM)
    for l in range(depth):
        p = _layer_params(l, weights)
        init_p = {"ssm": jnp.zeros((b_p, SSD_HEADS, SSD_HEAD_DIM, SSD_STATE), F32),
                  "ssd_conv": jnp.zeros((b_p, SSD_CONV - 1, SSD_CONV_DIM), F32),
                  "sconv": jnp.zeros((b_p, SCONV_K - 1, SCONV_WIDTH), F32),
                  "mla_ckv": None, "mla_kpe": None, "band_k": None, "band_v": None}
        y_p, st_p = _layer(y_p, b_p, t_p, init_p, p, tabs_p, 0, l)
        new_p.append(st_p)
        past_s = {"ssm": state_ssm[l], "ssd_conv": state_ssd_conv[l], "sconv": state_sconv[l],
                  "mla_ckv": cache_mla_ckv, "mla_kpe": cache_mla_kpe,
                  "band_k": band_k_rows, "band_v": band_v_rows}
        y_s, st_s = _layer(y_s, b_s, t_s, past_s, p, tabs_s, past_len, l)
        new_s.append(st_s)

    def stack(states, name):
        return jnp.stack([s[name] for s in states], axis=0)

    out = [y_p.reshape(b_p, t_p, d), y_s.reshape(b_s, t_s, d)]
    for name in ("ssm", "ssd_conv", "mla_ckv", "mla_kpe", "sconv"):
        out += [stack(new_p, name), stack(new_s, name)]
    for name, cache in (("band_k", cache_band_k), ("band_v", cache_band_v)):
        out += [stack(new_p, name), jnp.concatenate([cache[:, :, t_s:], stack(new_s, name)], axis=2)]
    return tuple(out)
```

```python
import functools
import math

import jax
import jax.numpy as jnp
import numpy as np
from jax import lax
from jax.experimental import pallas as pl
from jax.experimental.pallas import tpu as pltpu

F32 = jnp.float32
MXU_DTYPE = jnp.bfloat16
EPS = 1e-6
NEG_BIG = -1e30

V7X_LANES = 128
V7X_VMEM_LIMIT_BYTES = 56 * 1024 * 1024

D_MODEL = 1024
CHUNK = 64
N_BRANCH = 4
SSD_HEADS, SSD_HEAD_DIM, SSD_GROUPS, SSD_STATE, SSD_CONV = 8, 64, 2, 128, 4
SSD_INNER = SSD_HEADS * SSD_HEAD_DIM
SSD_CONV_DIM = SSD_INNER + 2 * SSD_GROUPS * SSD_STATE
MLA_HEADS, MLA_Q_RANK, MLA_KV_RANK, MLA_NOPE, MLA_ROPE, MLA_V = 8, 384, 256, 64, 32, 64
MLA_WIDTH = MLA_HEADS * MLA_V
ROPE_THETA = 10000.0
SCONV_WIDTH, SCONV_K = 512, 3
BAND_HEADS, BAND_HEAD_DIM, BAND_PAST_CHUNKS, REL_CLIP = 8, 64, 8, 128
BAND_WIDTH = BAND_HEADS * BAND_HEAD_DIM
BAND_ROWS = BAND_PAST_CHUNKS * CHUNK
BAND_SPAN = BAND_ROWS + CHUNK
FF_HIDDEN = ((8 * D_MODEL // 3 + 255) // 256) * 256

SEG = {
    "gate": (0, 4096), "c_b": (4096, 512), "c_c": (4608, 512), "c_x": (5120, 512),
    "d_q": (5632, 512), "d_k": (6144, 512), "d_v": (6656, 512),
    "a_z": (7168, 512), "a_x": (7680, 512), "a_b": (8192, 256), "a_c": (8448, 256),
    "b_kvl": (8704, 256), "a_dt": (8960, 128), "b_kpe": (9088, 128), "b_ql": (9216, 384),
}
PROJ_COLS = 9600
KPE_LANE = 64


def _cparams(*sem):
    return pltpu.CompilerParams(dimension_semantics=sem, vmem_limit_bytes=V7X_VMEM_LIMIT_BYTES)


def _mm(a, b):
    return jnp.dot(a.astype(MXU_DTYPE), b.astype(MXU_DTYPE), preferred_element_type=F32)


def _mm_nt(a, b):
    return lax.dot_general(a.astype(MXU_DTYPE), b.astype(MXU_DTYPE), (((1,), (1,)), ((), ())),
                           preferred_element_type=F32)


def _mm_tn(a, b):
    return lax.dot_general(a.astype(MXU_DTYPE), b.astype(MXU_DTYPE), (((0,), (0,)), ((), ())),
                           preferred_element_type=F32)


def _split3(x):
    hi = x.astype(MXU_DTYPE)
    r1 = x - hi.astype(F32)
    mid = r1.astype(MXU_DTYPE)
    lo = (r1 - mid.astype(F32)).astype(MXU_DTYPE)
    return hi, mid, lo


def _mm_exact_rhs01(x, sel):
    hi, mid, lo = _split3(x)
    sel = sel.astype(MXU_DTYPE)
    d = functools.partial(jnp.dot, preferred_element_type=F32)
    return d(hi, sel) + d(mid, sel) + d(lo, sel)


def _silu(x):
    return x * jax.nn.sigmoid(x)


def _row_rms(x):
    return x * lax.rsqrt(jnp.mean(x * x, axis=-1, keepdims=True) + EPS)


def _lane_iota(rows):
    return lax.broadcasted_iota(jnp.int32, (rows, V7X_LANES), 1)


def _head64_rms(x):
    rows, width = x.shape
    low = _lane_iota(rows) < 64
    out = []
    for j in range(width // V7X_LANES):
        blk = x[:, j * V7X_LANES:(j + 1) * V7X_LANES]
        sq = blk * blk
        s_lo = jnp.sum(jnp.where(low, sq, 0.0), axis=-1, keepdims=True) * (1.0 / 64)
        s_hi = jnp.sum(jnp.where(low, 0.0, sq), axis=-1, keepdims=True) * (1.0 / 64)
        out.append(blk * jnp.where(low, lax.rsqrt(s_lo + EPS), lax.rsqrt(s_hi + EPS)))
    return jnp.concatenate(out, axis=-1)


def _norm_matmul_nt_kernel(x_ref, g_ref, wt_ref, o_ref, h_s):
    @pl.when(pl.program_id(1) == 0)
    def _():
        h_s[...] = (_row_rms(x_ref[...]) * g_ref[...]).astype(h_s.dtype)

    o_ref[...] = _mm_nt(h_s[...], wt_ref[...])


def norm_matmul_nt(x, g, wt, tm, tn):
    n, k = x.shape
    c = wt.shape[0]
    return pl.pallas_call(
        _norm_matmul_nt_kernel, out_shape=jax.ShapeDtypeStruct((n, c), F32), grid=(n // tm, c // tn),
        in_specs=[pl.BlockSpec((tm, k), lambda i, j: (i, 0)), pl.BlockSpec((1, k), lambda i, j: (0, 0)),
                  pl.BlockSpec((tn, k), lambda i, j: (j, 0))],
        out_specs=pl.BlockSpec((tm, tn), lambda i, j: (i, j)),
        scratch_shapes=[pltpu.VMEM((tm, k), MXU_DTYPE)],
        compiler_params=_cparams("parallel", "arbitrary"), name="in_proj_matmul")(x, g.reshape(1, k), wt)


def _merge_kernel(x_ref, g_ref, ya_ref, ob_ref, yc_ref, od_ref, bg_ref, wa_ref, wb_ref, wc_ref, wd_ref, wo_ref, o_ref,
                  *, bd_transposed):
    def gate(k):
        return jax.nn.sigmoid(g_ref[:, k * D_MODEL:(k + 1) * D_MODEL] + bg_ref[:, k * D_MODEL:(k + 1) * D_MODEL])

    def attn_out(a_ref, w_ref):
        return _mm_tn(a_ref[0], w_ref[...]) if bd_transposed else _mm(a_ref[...], w_ref[...])

    merged = gate(0) * _mm(ya_ref[...], wa_ref[...])
    merged = merged + gate(1) * attn_out(ob_ref, wb_ref)
    merged = merged + gate(2) * _mm(yc_ref[...], wc_ref[...])
    merged = merged + gate(3) * attn_out(od_ref, wd_ref)
    o_ref[...] = x_ref[...] + _mm(merged, wo_ref[...])


def merge(x, proj, ya, ob, yc, od, b_gate, wa, wb, wc, wd, wo, tm):
    n, d = x.shape
    row = lambda w: pl.BlockSpec((tm, w), lambda i: (i, 0))
    full = lambda a: pl.BlockSpec(a.shape, lambda i: (0, 0))
    bd_transposed = ob.ndim == 3
    if bd_transposed:
        per_seq = ob.shape[2] // tm
        attn = pl.BlockSpec((1, ob.shape[1], tm), lambda i: (i // per_seq, 0, i % per_seq))
    else:
        attn = row(512)
    bg = b_gate.reshape(1, N_BRANCH * d)
    return pl.pallas_call(
        functools.partial(_merge_kernel, bd_transposed=bd_transposed),
        out_shape=jax.ShapeDtypeStruct((n, d), F32), grid=(n // tm,),
        in_specs=[row(d), pl.BlockSpec((tm, N_BRANCH * d), lambda i: (i, 0)), row(512), attn, row(512), attn,
                  full(bg), full(wa), full(wb), full(wc), full(wd), full(wo)],
        out_specs=row(d), compiler_params=_cparams("parallel"), name="merge")(
            x, proj, ya, ob, yc, od, bg, wa, wb, wc, wd, wo)


def _ffn_kernel(x_ref, g_ref, wg_ref, wv_ref, wd_ref, o_ref, h_s, acc_s):
    k = pl.program_id(1)

    @pl.when(k == 0)
    def _():
        h_s[...] = (_row_rms(x_ref[...]) * g_ref[...]).astype(h_s.dtype)
        acc_s[...] = jnp.zeros_like(acc_s)

    h = h_s[...]
    gate = jnp.dot(h, wg_ref[...], preferred_element_type=F32)
    val = jnp.dot(h, wv_ref[...], preferred_element_type=F32)
    acc_s[...] += _mm(_silu(gate) * val, wd_ref[...])

    @pl.when(k == pl.num_programs(1) - 1)
    def _():
        o_ref[...] = x_ref[...] + acc_s[...]


def ffn(x, g, w_up, w_down, tm, tk):
    n, d = x.shape
    nk = FF_HIDDEN // tk
    return pl.pallas_call(
        _ffn_kernel, out_shape=jax.ShapeDtypeStruct((n, d), F32), grid=(n // tm, nk),
        in_specs=[pl.BlockSpec((tm, d), lambda i, k: (i, 0)), pl.BlockSpec((1, d), lambda i, k: (0, 0)),
                  pl.BlockSpec((d, tk), lambda i, k: (0, k)), pl.BlockSpec((d, tk), lambda i, k: (0, nk + k)),
                  pl.BlockSpec((tk, d), lambda i, k: (k, 0))],
        out_specs=pl.BlockSpec((tm, d), lambda i, k: (i, 0)),
        scratch_shapes=[pltpu.VMEM((tm, d), MXU_DTYPE), pltpu.VMEM((tm, d), F32)],
        compiler_params=_cparams("parallel", "arbitrary"), name="ffn")(x, g.reshape(1, d), w_up, w_up, w_down)


def _ssd_kernel(z_ref, x_ref, b_ref, c_ref, dt_ref, conv0_ref, h0_ref, cw_ref, cbias_ref, dtb_ref, alog_ref,
                dfull_ref, ng_ref, tri_ref, expand_ref, y_ref, convn_ref, hn_ref, xp_s, h_s):
    L = CHUNK
    rows = z_ref.shape[0]
    c = pl.program_id(1)

    @pl.when(c == 0)
    def _():
        xp_s[0:8, :] = conv0_ref[0]
        h_s[...] = h0_ref[0]

    xp_s[8:8 + rows, 0:512] = x_ref[...]
    xp_s[8:8 + rows, 512:768] = b_ref[...]
    xp_s[8:8 + rows, 768:1024] = c_ref[...]
    base = 8 - (SSD_CONV - 1)
    acc = xp_s[base:base + rows, :] * cw_ref[0:1, :]
    for i in range(1, SSD_CONV):
        acc = acc + xp_s[base + i:base + i + rows, :] * cw_ref[i:i + 1, :]
    xbc = _silu(acc + cbias_ref[...])
    tail = xp_s[rows:rows + 8, :]
    convn_ref[0] = tail
    xp_s[0:8, :] = tail

    xs = xbc[:, 0:512]
    dtr = dt_ref[...] + dtb_ref[...]
    dt = jnp.maximum(dtr, 0.0) + jnp.log1p(jnp.exp(-jnp.abs(dtr)))
    dt_full = _mm_exact_rhs01(dt, expand_ref[...])
    da_full = dt_full * (-jnp.exp(alog_ref[...]))
    da_parts = _split3(da_full)
    dot = functools.partial(jnp.dot, preferred_element_type=F32)
    tri = tri_ref[...].astype(MXU_DTYPE)
    acs = dot(tri, da_parts[0]) + dot(tri, da_parts[1]) + dot(tri, da_parts[2])
    x_dt = xs * dt_full
    exp_acs = jnp.exp(acs)

    row = lax.broadcasted_iota(jnp.int32, (L, V7X_LANES), 0)
    lane = _lane_iota(L)
    lane_s = jnp.where(lane < 64, lane, lane - 64)
    diag2 = row == lane_s
    tril2 = lane_s <= row
    row2 = lax.broadcasted_iota(jnp.int32, (2 * L, V7X_LANES), 0)
    lane2 = lax.broadcasted_iota(jnp.int32, (2 * L, V7X_LANES), 1)
    blockdiag = (row2 < L) == (lane2 < 64)
    ones_ln = jnp.ones((L, SSD_STATE), MXU_DTYPE)
    dot_tn = functools.partial(lax.dot_general, dimension_numbers=(((0,), (0,)), ((), ())), preferred_element_type=F32)

    h = [h_s[256 * g:256 * (g + 1), :] for g in range(SSD_GROUPS)]
    y_chunks = []
    for k in range(rows // L):
        r = slice(k * L, (k + 1) * L)
        acs_k = acs[r]
        xd_end = x_dt[r] * jnp.exp(acs_k[L - 1:L, :] - acs_k)
        y_parts = []
        for g in range(SSD_GROUPS):
            cols = slice(256 * g, 256 * (g + 1))
            bg = xbc[r, 512 + 128 * g:512 + 128 * (g + 1)]
            cg = xbc[r, 768 + 128 * g:768 + 128 * (g + 1)]
            cb2 = _mm_nt(cg, jnp.concatenate([bg, bg], axis=0))
            y_off = _mm_nt(cg, h[g]) * exp_acs[r, cols]
            for jp in range(2):
                sl = slice(128 * (2 * g + jp), 128 * (2 * g + jp + 1))
                blk = acs_k[:, sl]
                at_s = jnp.sum(jnp.where(diag2, blk, 0.0), axis=0, keepdims=True)
                decay = jnp.where(tril2, jnp.exp(jnp.minimum(blk - at_s, 0.0)), 0.0)
                xpair = x_dt[r, sl]
                xblk = jnp.where(blockdiag, jnp.concatenate([xpair, xpair], axis=0), 0.0)
                y_parts.append(_mm(cb2 * decay, xblk) + y_off[:, 128 * jp:128 * (jp + 1)])
            state = _mm_tn(xd_end[:, cols], bg)
            total = sum(dot_tn(part[r, cols], ones_ln) for part in da_parts)
            h[g] = jnp.exp(total) * h[g] + state
        y_chunks.append(jnp.concatenate(y_parts, axis=-1))
    y = jnp.concatenate(y_chunks, axis=0) + dfull_ref[...] * xs
    gated = y * _silu(z_ref[...])
    y_ref[...] = (_row_rms(gated) * ng_ref[...]).astype(y_ref.dtype)
    for g in range(SSD_GROUPS):
        h_s[256 * g:256 * (g + 1), :] = h[g]
        hn_ref[0, 256 * g:256 * (g + 1), :] = h[g]


def ssd_branch(proj, bsz, t, conv_prev, ssm_prev, p):
    rows = math.gcd(t, 4 * CHUNK)
    nb = t // rows
    n = bsz * t
    col = lambda name: SEG[name][0] // SEG[name][1]
    seg = lambda name: pl.BlockSpec((rows, SEG[name][1]), lambda b, c, _j=col(name): (b * nb + c, _j))
    full = lambda a: pl.BlockSpec(a.shape, lambda b, c: (0,) * a.ndim)
    conv0 = jnp.pad(conv_prev, ((0, 0), (8 - (SSD_CONV - 1), 0), (0, 0)))
    h0 = ssm_prev.reshape(bsz, SSD_INNER, SSD_STATE)
    tri = jnp.kron(jnp.eye(rows // CHUNK, dtype=F32), jnp.tril(jnp.ones((CHUNK, CHUNK), F32)))
    consts = [p["ssd_conv_w"], p["ssd_conv_b"], p["ssd_dt_bias"], p["ssd_a_log"], p["ssd_d"], p["ssd_norm_g"], tri,
              p["ssd_expand"]]
    y, convn, hn = pl.pallas_call(
        _ssd_kernel,
        out_shape=[jax.ShapeDtypeStruct((n, SSD_INNER), MXU_DTYPE), jax.ShapeDtypeStruct((bsz, 8, SSD_CONV_DIM), F32),
                   jax.ShapeDtypeStruct((bsz, SSD_INNER, SSD_STATE), F32)],
        grid=(bsz, nb),
        in_specs=[seg("a_z"), seg("a_x"), seg("a_b"), seg("a_c"), seg("a_dt"),
                  pl.BlockSpec((1, 8, SSD_CONV_DIM), lambda b, c: (b, 0, 0)),
                  pl.BlockSpec((1, SSD_INNER, SSD_STATE), lambda b, c: (b, 0, 0))] + [full(a) for a in consts],
        out_specs=[pl.BlockSpec((rows, SSD_INNER), lambda b, c: (b * nb + c, 0)),
                   pl.BlockSpec((1, 8, SSD_CONV_DIM), lambda b, c: (b, 0, 0)),
                   pl.BlockSpec((1, SSD_INNER, SSD_STATE), lambda b, c: (b, 0, 0))],
        scratch_shapes=[pltpu.VMEM((rows + 8, SSD_CONV_DIM), F32), pltpu.VMEM((SSD_INNER, SSD_STATE), F32)],
        compiler_params=_cparams("parallel", "arbitrary"), name="ssd_branch")(
            proj, proj, proj, proj, proj, conv0, h0, *consts)
    return y, hn.reshape(bsz, SSD_HEADS, SSD_HEAD_DIM, SSD_STATE), convn[:, 8 - (SSD_CONV - 1):]


def _sconv_kernel(cb_ref, cc_ref, cx_ref, st0_ref, w_ref, o_ref, stn_ref, up_s):
    rows = cb_ref.shape[0]

    @pl.when(pl.program_id(1) == 0)
    def _():
        up_s[0:8, :] = st0_ref[0]

    up_s[8:8 + rows, :] = cc_ref[...] * cx_ref[...]
    base = 8 - (SCONV_K - 1)
    uc = up_s[base:base + rows, :] * w_ref[0:1, :]
    for i in range(1, SCONV_K):
        uc = uc + up_s[base + i:base + i + rows, :] * w_ref[i:i + 1, :]
    o_ref[...] = (cb_ref[...] * uc).astype(o_ref.dtype)
    tail = up_s[rows:rows + 8, :]
    stn_ref[0] = tail
    up_s[0:8, :] = tail


def sconv_branch(proj, bsz, t, prev, w):
    rows = min(t, 512)
    nt = t // rows
    col = lambda name: SEG[name][0] // SCONV_WIDTH
    seg = lambda name: pl.BlockSpec((rows, SCONV_WIDTH), lambda b, c, _j=col(name): (b * nt + c, _j))
    st0 = jnp.pad(prev, ((0, 0), (8 - (SCONV_K - 1), 0), (0, 0)))
    o, stn = pl.pallas_call(
        _sconv_kernel,
        out_shape=[jax.ShapeDtypeStruct((bsz * t, SCONV_WIDTH), MXU_DTYPE), jax.ShapeDtypeStruct((bsz, 8, SCONV_WIDTH), F32)],
        grid=(bsz, nt),
        in_specs=[seg("c_b"), seg("c_c"), seg("c_x"), pl.BlockSpec((1, 8, SCONV_WIDTH), lambda b, c: (b, 0, 0)),
                  pl.BlockSpec(w.shape, lambda b, c: (0, 0))],
        out_specs=[pl.BlockSpec((rows, SCONV_WIDTH), lambda b, c: (b * nt + c, 0)),
                   pl.BlockSpec((1, 8, SCONV_WIDTH), lambda b, c: (b, 0, 0))],
        scratch_shapes=[pltpu.VMEM((rows + 8, SCONV_WIDTH), F32)],
        compiler_params=_cparams("parallel", "arbitrary"), name="sconv_branch")(proj, proj, proj, st0, w)
    return o, stn[:, 8 - (SCONV_K - 1):]


def _rope(y, cos_t, sin_a, sin_b):
    return y * cos_t + pltpu.roll(y, 16, 1) * sin_a + pltpu.roll(y, V7X_LANES - 16, 1) * sin_b


def _mla_prep_kernel(ql_ref, kvl_ref, kpe_ref, cos_ref, sa_ref, sb_ref, gq_ref, wq_ref, gqh_ref, gkv_ref, gkr_ref,
                     q_ref, ckv_ref, kpeo_ref):
    rows = ql_ref.shape[0]
    cos_t, sin_a, sin_b = cos_ref[...], sa_ref[...], sb_ref[...]
    q = _mm(_row_rms(ql_ref[...]) * gq_ref[...], wq_ref[...])
    lane = _lane_iota(rows)
    nope = lane < MLA_NOPE
    scale = (MLA_NOPE + MLA_ROPE) ** -0.5 * math.log2(math.e)
    for h in range(MLA_HEADS):
        sl = slice(h * V7X_LANES, (h + 1) * V7X_LANES)
        blk = q[:, sl]
        sq = blk * blk
        s_n = jnp.sum(jnp.where(nope, sq, 0.0), axis=-1, keepdims=True) * (1.0 / MLA_NOPE)
        s_r = jnp.sum(jnp.where(nope, 0.0, sq), axis=-1, keepdims=True) * (1.0 / MLA_ROPE)
        y = blk * jnp.where(nope, lax.rsqrt(s_n + EPS), lax.rsqrt(s_r + EPS)) * gqh_ref[:, sl]
        q_ref[:, sl] = (_rope(y, cos_t, sin_a, sin_b) * scale).astype(q_ref.dtype)
    ckv_ref[...] = _row_rms(kvl_ref[...]) * gkv_ref[...]
    kp = kpe_ref[...]
    ms = jnp.sum(kp * kp, axis=-1, keepdims=True) * (1.0 / MLA_ROPE)
    kpeo_ref[...] = _rope(kp * lax.rsqrt(ms + EPS) * gkr_ref[...], cos_t, sin_a, sin_b)


def mla_prep(proj, n, tm, tabs, p):
    cos_t, sin_a, sin_b = tabs
    ntab = cos_t.shape[0] // tm
    seg = lambda name: pl.BlockSpec((tm, SEG[name][1]), lambda i, _j=SEG[name][0] // SEG[name][1]: (i, _j))
    tab = pl.BlockSpec((tm, V7X_LANES), lambda i: (i % ntab, 0))
    full = lambda a: pl.BlockSpec(a.shape, lambda i: (0, 0))
    consts = [p["mla_q_norm_g"], p["mla_wq"], p["mla_gq_head"], p["mla_kv_norm_g"], p["mla_gkr"]]
    return pl.pallas_call(
        _mla_prep_kernel,
        out_shape=[jax.ShapeDtypeStruct((n, MLA_HEADS * V7X_LANES), MXU_DTYPE),
                   jax.ShapeDtypeStruct((n, MLA_KV_RANK), F32), jax.ShapeDtypeStruct((n, V7X_LANES), F32)],
        grid=(n // tm,),
        in_specs=[seg("b_ql"), seg("b_kvl"), seg("b_kpe"), tab, tab, tab] + [full(a) for a in consts],
        out_specs=[pl.BlockSpec((tm, MLA_HEADS * V7X_LANES), lambda i: (i, 0)),
                   pl.BlockSpec((tm, MLA_KV_RANK), lambda i: (i, 0)), pl.BlockSpec((tm, V7X_LANES), lambda i: (i, 0))],
        compiler_params=_cparams("parallel"), name="mla_prep")(proj, proj, proj, cos_t, sin_a, sin_b, *consts)


def _mla_kv_kernel(ckv_ref, kpe_ref, wk_ref, wvt_ref, gk_ref, k_ref, vt_ref):
    c = ckv_ref[0].astype(MXU_DTYPE)
    kk = jnp.dot(c, wk_ref[...], preferred_element_type=F32)
    vt_ref[0, 0] = _mm_nt(wvt_ref[...], c).astype(vt_ref.dtype)
    kpe = kpe_ref[0]
    for h in range(MLA_HEADS):
        sl = slice(h * V7X_LANES, (h + 1) * V7X_LANES)
        blk = kk[:, sl]
        ms = jnp.sum(blk * blk, axis=-1, keepdims=True) * (1.0 / MLA_NOPE)
        k_ref[0, :, sl] = (blk * lax.rsqrt(ms + EPS) * gk_ref[...] + kpe).astype(k_ref.dtype)


def mla_kv(ckv_all, kpe_all, ts, p):
    bsz, s, _ = ckv_all.shape
    full = lambda a: pl.BlockSpec(a.shape, lambda b, i: (0, 0))
    consts = [p["mla_wk"], p["mla_wvt"], p["mla_gk"]]
    return pl.pallas_call(
        _mla_kv_kernel,
        out_shape=[jax.ShapeDtypeStruct((bsz, s, MLA_HEADS * V7X_LANES), MXU_DTYPE),
                   jax.ShapeDtypeStruct((bsz, s // ts, MLA_WIDTH, ts), MXU_DTYPE)],
        grid=(bsz, s // ts),
        in_specs=[pl.BlockSpec((1, ts, MLA_KV_RANK), lambda b, i: (b, i, 0)),
                  pl.BlockSpec((1, ts, V7X_LANES), lambda b, i: (b, i, 0))] + [full(a) for a in consts],
        out_specs=[pl.BlockSpec((1, ts, MLA_HEADS * V7X_LANES), lambda b, i: (b, i, 0)),
                   pl.BlockSpec((1, 1, MLA_WIDTH, ts), lambda b, i: (b, i, 0, 0))],
        compiler_params=_cparams("parallel", "parallel"), name="mla_kv")(ckv_all, kpe_all, *consts)


def _mla_kv_cached_kernel(ckv_old_ref, ckv_new_ref, kpe_old_ref, kpe_new_ref, wk_ref, wv_ref, gk_ref, k_ref, v_ref):
    c = jnp.concatenate([ckv_old_ref[0, 0], ckv_new_ref[0]], axis=0).astype(MXU_DTYPE)
    old_t = kpe_old_ref[0, 0]
    tail = V7X_LANES - KPE_LANE - MLA_ROPE
    old_t = jnp.concatenate([jnp.zeros((KPE_LANE, old_t.shape[1]), F32), old_t, jnp.zeros((tail, old_t.shape[1]), F32)], axis=0)
    kpe = jnp.concatenate([old_t.T, kpe_new_ref[0]], axis=0)
    kk = jnp.dot(c, wk_ref[...], preferred_element_type=F32)
    v_ref[0] = jnp.dot(c, wv_ref[...], preferred_element_type=F32).astype(v_ref.dtype)
    for h in range(MLA_HEADS):
        sl = slice(h * V7X_LANES, (h + 1) * V7X_LANES)
        blk = kk[:, sl]
        ms = jnp.sum(blk * blk, axis=-1, keepdims=True) * (1.0 / MLA_NOPE)
        k_ref[0, :, sl] = (blk * lax.rsqrt(ms + EPS) * gk_ref[...] + kpe).astype(k_ref.dtype)


def mla_kv_cached(ckv_cache, kpe_cache, layer, ckv_new, kpe_new, p):
    _, bsz, past, _ = ckv_cache.shape
    t = ckv_new.shape[1]
    s = past + t
    full = lambda a: pl.BlockSpec(a.shape, lambda b: (0, 0))
    consts = [p["mla_wk"], p["mla_wv"], p["mla_gk"]]
    return pl.pallas_call(
        _mla_kv_cached_kernel,
        out_shape=[jax.ShapeDtypeStruct((bsz, s, MLA_HEADS * V7X_LANES), MXU_DTYPE),
                   jax.ShapeDtypeStruct((bsz, s, MLA_WIDTH), MXU_DTYPE)],
        grid=(bsz,),
        in_specs=[pl.BlockSpec((1, 1, past, MLA_KV_RANK), lambda b: (layer, b, 0, 0)),
                  pl.BlockSpec((1, t, MLA_KV_RANK), lambda b: (b, 0, 0)),
                  pl.BlockSpec((1, 1, MLA_ROPE, past), lambda b: (layer, b, 0, 0)),
                  pl.BlockSpec((1, t, V7X_LANES), lambda b: (b, 0, 0))] + [full(a) for a in consts],
        out_specs=[pl.BlockSpec((1, s, MLA_HEADS * V7X_LANES), lambda b: (b, 0, 0)),
                   pl.BlockSpec((1, s, MLA_WIDTH), lambda b: (b, 0, 0))],
        compiler_params=_cparams("parallel"), name="mla_kv_cached")(ckv_cache, ckv_new, kpe_cache, kpe_new, *consts)


def _mla_attn_rows_kernel(q_ref, k_ref, v_ref, o_ref, *, q_off):
    t, s = q_ref.shape[1], k_ref.shape[1]
    low = _lane_iota(t) < MLA_V
    needs_mask = (s - 1) // CHUNK > q_off // CHUNK
    if needs_mask:
        q_chunk = (q_off + lax.broadcasted_iota(jnp.int32, (t, s), 0)) // CHUNK
        visible = lax.broadcasted_iota(jnp.int32, (t, s), 1) // CHUNK <= q_chunk
    for hp in range(MLA_HEADS // 2):
        v_pair = v_ref[0, :, hp * V7X_LANES:(hp + 1) * V7X_LANES]
        outs = []
        for sub in range(2):
            sl = slice((2 * hp + sub) * V7X_LANES, (2 * hp + sub + 1) * V7X_LANES)
            sc = _mm_nt(q_ref[0, :, sl], k_ref[0, :, sl])
            if needs_mask:
                sc = jnp.where(visible, sc, NEG_BIG)
            e = jnp.exp2(sc - jnp.max(sc, axis=-1, keepdims=True))
            outs.append(_mm(e, v_pair) / jnp.sum(e, axis=-1, keepdims=True))
        o_ref[0, :, hp * V7X_LANES:(hp + 1) * V7X_LANES] = jnp.where(low, outs[0], outs[1]).astype(o_ref.dtype)


def mla_attention_rows(q, k, v, q_off):
    bsz, t, _ = q.shape
    s = k.shape[1]
    return pl.pallas_call(
        functools.partial(_mla_attn_rows_kernel, q_off=q_off),
        out_shape=jax.ShapeDtypeStruct((bsz, t, MLA_WIDTH), MXU_DTYPE), grid=(bsz,),
        in_specs=[pl.BlockSpec((1, t, MLA_HEADS * V7X_LANES), lambda b: (b, 0, 0)),
                  pl.BlockSpec((1, s, MLA_HEADS * V7X_LANES), lambda b: (b, 0, 0)),
                  pl.BlockSpec((1, s, MLA_WIDTH), lambda b: (b, 0, 0))],
        out_specs=pl.BlockSpec((1, t, MLA_WIDTH), lambda b: (b, 0, 0)),
        compiler_params=_cparams("parallel"), name="mla_attention_rows")(q, k, v)


def _mla_attn_kernel(q_ref, k_ref, vt_ref, ot_ref, sa_s, sb_s, *, tq, tkc, q_off, nh):
    q_start = q_off + pl.program_id(2) * tq
    n_full = (q_start + CHUNK) // tkc
    n_total = (q_start + tq + tkc - 1) // tkc
    lanes = lambda h: slice(h * V7X_LANES, (h + 1) * V7X_LANES)
    qs = [q_ref[0, :, lanes(h)] for h in range(nh)]
    ones = jnp.ones((8, tkc), MXU_DTYPE)

    def produce(c, buf):
        k0 = pl.multiple_of(c * tkc, tkc)
        for h in range(nh):
            buf[h] = _mm_nt(k_ref[0, pl.ds(k0, tkc), lanes(h)], qs[h])

    def consume(c, buf, stats, masked):
        if masked:
            k_chunk = (c * tkc + lax.broadcasted_iota(jnp.int32, (tkc, tq), 0)) // CHUNK
            q_chunk = (q_start + lax.broadcasted_iota(jnp.int32, (tkc, tq), 1)) // CHUNK
            visible = k_chunk <= q_chunk
        out = []
        for h in range(nh):
            m, acc = stats[h]
            s = jnp.where(visible, buf[h], NEG_BIG) if masked else buf[h]
            m_new = jnp.maximum(m, jnp.max(s, axis=0, keepdims=True))
            alpha = jnp.exp2(m - m_new)
            p = jnp.exp2(s - m_new).astype(MXU_DTYPE)
            v_ext = jnp.concatenate([vt_ref[0, c, h * MLA_V:(h + 1) * MLA_V, :], ones], axis=0)
            out.append((m_new, alpha * acc + jnp.dot(v_ext, p, preferred_element_type=F32)))
        return tuple(out)

    def pair(i, stats):
        c = 2 * i
        produce(c + 1, sb_s)
        stats = consume(c, sa_s, stats, False)
        produce(jnp.minimum(c + 2, n_total - 1), sa_s)
        return consume(c + 1, sb_s, stats, False)

    def single(c, stats):
        produce(c, sb_s)
        return consume(c, sb_s, stats, True)

    init = tuple((jnp.full((1, tq), NEG_BIG, F32), jnp.zeros((MLA_V + 8, tq), F32)) for _ in range(nh))
    produce(0, sa_s)
    n_pairs = n_full // 2
    stats = lax.fori_loop(0, n_pairs, pair, init)
    stats = consume(2 * n_pairs, sa_s, stats, True)
    stats = lax.fori_loop(2 * n_pairs + 1, n_total, single, stats)
    for h in range(nh):
        acc = stats[h][1]
        ot_ref[0, h * MLA_V:(h + 1) * MLA_V, :] = (acc[:MLA_V] / acc[MLA_V:MLA_V + 1]).astype(ot_ref.dtype)


def mla_attention(q, k, vt, q_off, tq, nh):
    bsz, t, _ = q.shape
    _, nchunk, _, tkc = vt.shape
    s = k.shape[1]
    kern = functools.partial(_mla_attn_kernel, tq=tq, tkc=tkc, q_off=q_off, nh=nh)
    return pl.pallas_call(
        kern, out_shape=jax.ShapeDtypeStruct((bsz, MLA_WIDTH, t), MXU_DTYPE),
        grid=(bsz, MLA_HEADS // nh, t // tq),
        in_specs=[pl.BlockSpec((1, tq, nh * V7X_LANES), lambda b, h, qi: (b, qi, h)),
                  pl.BlockSpec((1, s, nh * V7X_LANES), lambda b, h, qi: (b, 0, h), pipeline_mode=pl.Buffered(1)),
                  pl.BlockSpec((1, nchunk, nh * MLA_V, tkc), lambda b, h, qi: (b, 0, h, 0),
                               pipeline_mode=pl.Buffered(1))],
        out_specs=pl.BlockSpec((1, nh * MLA_V, tq), lambda b, h, qi: (b, h, qi)),
        scratch_shapes=[pltpu.VMEM((nh, tkc, tq), F32), pltpu.VMEM((nh, tkc, tq), F32)],
        compiler_params=_cparams("parallel", "parallel", "parallel"), name="mla_attention")(q, k, vt)


def _band_kernel(q_ref, ka_ref, kb_ref, va_ref, vb_ref, bias_ref, gq_ref, gk_ref, ot_ref, kn_ref, q_s, k_s, vt_s,
                 *, a_is_cache):
    tb = q_ref.shape[0]
    rows_q = q_s.shape[0]
    pad = rows_q - tb
    qn = _head64_rms(q_ref[...]) * gq_ref[...] * (BAND_HEAD_DIM ** -0.5 * math.log2(math.e))
    kb = _head64_rms(kb_ref[...]) * gk_ref[...]
    kn_ref[...] = kb
    if a_is_cache:
        k_s[0:BAND_ROWS, :] = ka_ref[0, 0].T.astype(k_s.dtype)
        vt_s[:, 0:BAND_ROWS] = va_ref[0, 0].astype(vt_s.dtype)
    else:
        k_s[0:BAND_ROWS, :] = (_head64_rms(ka_ref[...]) * gk_ref[...]).astype(k_s.dtype)
        vt_s[:, 0:BAND_ROWS] = va_ref[...].T.astype(vt_s.dtype)
    vb = vb_ref[...]
    if pad:
        zeros = jnp.zeros((pad, BAND_WIDTH), F32)
        qn, kb, vb = (jnp.concatenate([a, zeros], axis=0) for a in (qn, kb, vb))
    q_s[...] = qn.astype(q_s.dtype)
    k_s[BAND_ROWS:, :] = kb.astype(k_s.dtype)
    vt_s[:, BAND_ROWS:] = vb.T.astype(vt_s.dtype)

    win = BAND_ROWS + V7X_LANES
    low = _lane_iota(V7X_LANES) < BAND_HEAD_DIM
    ones = jnp.ones((8, win), MXU_DTYPE)
    win_row = lax.broadcasted_iota(jnp.int32, (win, V7X_LANES), 0)

    def tiles(first):
        for j in range(rows_q // V7X_LANES):
            r0 = j * V7X_LANES
            qj = q_s[r0:r0 + V7X_LANES, :]
            kw = k_s[r0:r0 + win, :]
            ss = []
            for hp in range(BAND_HEADS // 2):
                sl = slice(hp * V7X_LANES, (hp + 1) * V7X_LANES)
                zero = jnp.zeros((), q_s.dtype)
                q2 = jnp.concatenate([jnp.where(low, qj[:, sl], zero), jnp.where(low, zero, qj[:, sl])], axis=0)
                ss.append(_mm_nt(kw[:, sl], q2))
            for hp in range(BAND_HEADS // 2):
                sl = slice(hp * V7X_LANES, (hp + 1) * V7X_LANES)
                v_ext = jnp.concatenate([vt_s[sl, r0:r0 + win], ones], axis=0)
                s = ss[hp] + jnp.concatenate([bias_ref[2 * hp], bias_ref[2 * hp + 1]], axis=1)
                if first:
                    s = jnp.where(jnp.concatenate([win_row, win_row], axis=1) + r0 < BAND_ROWS, NEG_BIG, s)
                p = jnp.exp2(s - jnp.max(s, axis=0, keepdims=True)).astype(MXU_DTYPE)
                res = jnp.dot(v_ext, p, preferred_element_type=F32)
                den = res[V7X_LANES:V7X_LANES + 1]
                halves = [res[0:BAND_HEAD_DIM, 0:V7X_LANES] / den[:, 0:V7X_LANES],
                          res[BAND_HEAD_DIM:V7X_LANES, V7X_LANES:] / den[:, V7X_LANES:]]
                o_pair = jnp.concatenate(halves, axis=0).astype(ot_ref.dtype)
                if pad:
                    ot_ref[0, sl, :] = o_pair[:, 0:tb]
                else:
                    ot_ref[0, sl, r0:r0 + V7X_LANES] = o_pair

    if a_is_cache:
        tiles(False)
    else:
        pl.when(pl.program_id(1) == 0)(lambda: tiles(True))
        pl.when(pl.program_id(1) > 0)(lambda: tiles(False))


def band_branch(q_src, k_prev, v_prev, layer, bsz, t, bias, gq, gk):
    a_is_cache = k_prev is not None
    tb = min(t, BAND_ROWS)
    nt = t // tb
    rows_q = -(-tb // V7X_LANES) * V7X_LANES
    col = lambda name: SEG[name][0] // BAND_WIDTH
    cur = lambda name: pl.BlockSpec((tb, BAND_WIDTH), lambda b, i, _j=col(name): (b * nt + i, _j))
    if a_is_cache:
        prev = lambda name: pl.BlockSpec((1, 1, BAND_WIDTH, BAND_ROWS), lambda b, i: (layer, b, 0, 0))
        ka, va = k_prev, v_prev
    else:
        prev = lambda name: pl.BlockSpec((BAND_ROWS, BAND_WIDTH), lambda b, i, _j=col(name): (b * nt + jnp.maximum(i - 1, 0), _j))
        ka, va = q_src, q_src
    full = lambda a: pl.BlockSpec(a.shape, lambda b, i: (0,) * a.ndim)
    kern = functools.partial(_band_kernel, a_is_cache=a_is_cache)
    return pl.pallas_call(
        kern,
        out_shape=[jax.ShapeDtypeStruct((bsz, BAND_WIDTH, t), MXU_DTYPE), jax.ShapeDtypeStruct((bsz * t, BAND_WIDTH), F32)],
        grid=(bsz, nt),
        in_specs=[cur("d_q"), prev("d_k"), cur("d_k"), prev("d_v"), cur("d_v"), full(bias), full(gq), full(gk)],
        out_specs=[pl.BlockSpec((1, BAND_WIDTH, tb), lambda b, i: (b, 0, i)),
                   pl.BlockSpec((tb, BAND_WIDTH), lambda b, i: (b * nt + i, 0))],
        scratch_shapes=[pltpu.VMEM((rows_q, BAND_WIDTH), MXU_DTYPE), pltpu.VMEM((BAND_ROWS + rows_q, BAND_WIDTH), MXU_DTYPE),
                        pltpu.VMEM((BAND_WIDTH, BAND_ROWS + rows_q), MXU_DTYPE)],
        compiler_params=_cparams("parallel", "parallel"), name="band_branch")(
            q_src, ka, q_src, va, q_src, bias, gq, gk)


def _head_blocks(w, per_head, used):
    k = w.shape[0]
    w = w.reshape(k, -1, per_head)[:, :, :used]
    return jnp.pad(w, ((0, 0), (0, 0), (0, V7X_LANES - used))).reshape(k, -1)


def _lane_pad(v, offset, width=V7X_LANES):
    return jnp.pad(v, (offset, width - offset - v.shape[0])).reshape(1, width)


def _layer_params(l, w):
    src = {}
    off = 0
    for name, width in (("a_z", 512), ("a_x", 512), ("a_b", 256), ("a_c", 256), ("a_dt", 8), ("b_ql", 384),
                        ("b_kvl", 256), ("b_kpe", 32), ("c_b", 512), ("c_c", 512), ("c_x", 512),
                        ("d_q", 512), ("d_k", 512), ("d_v", 512), ("gate", 4096)):
        src[name] = (off, width)
        off += width
    w_in_t = w["w_in"][l].T.astype(MXU_DTYPE)
    pieces = []
    for name, (dst, dwidth) in sorted(SEG.items(), key=lambda kv: kv[1][0]):
        s0, sw = src[name]
        lead = KPE_LANE if name == "b_kpe" else 0
        for width in (lead, None, dwidth - sw - lead):
            if width is None:
                pieces.append(w_in_t[s0:s0 + sw])
            elif width:
                pieces.append(jnp.zeros((width, D_MODEL), MXU_DTYPE))
    p = {"w_in_t": jnp.concatenate(pieces, axis=0)}
    cast = lambda a: a.astype(MXU_DTYPE)
    row = lambda a: a.reshape(1, -1)
    p["norm_mix_g"], p["norm_ffn_g"], p["b_gate"] = w["norm_mix_g"][l], w["norm_ffn_g"][l], w["b_gate"][l]
    for name in ("w_a_out", "w_b_out", "w_c_out", "w_d_out", "w_o", "w_ffn_up", "w_ffn_down"):
        p[name] = cast(w[name][l])
    p["ssd_conv_w"] = w["ssd_conv_w"][l]
    p["ssd_conv_b"] = row(w["ssd_conv_b"][l])
    p["ssd_dt_bias"] = _lane_pad(w["ssd_dt_bias"][l], 0)
    p["ssd_a_log"] = row(jnp.repeat(w["ssd_a_log"][l], SSD_HEAD_DIM))
    p["ssd_d"] = row(jnp.repeat(w["ssd_d"][l], SSD_HEAD_DIM))
    p["ssd_norm_g"] = row(w["ssd_norm_g"][l])
    head_of_lane = jnp.arange(SSD_INNER) // SSD_HEAD_DIM
    p["ssd_expand"] = (jnp.arange(V7X_LANES)[:, None] == head_of_lane[None, :]).astype(F32)
    p["mla_q_norm_g"] = row(w["mla_q_norm_g"][l])
    p["mla_wq"] = cast(_head_blocks(w["mla_w_q_up"][l], MLA_NOPE + MLA_ROPE, MLA_NOPE + MLA_ROPE))
    gqh = jnp.concatenate([w["mla_qn_g"][l], w["mla_qr_g"][l], jnp.zeros((V7X_LANES - MLA_NOPE - MLA_ROPE,), F32)])
    p["mla_gq_head"] = row(jnp.tile(gqh, MLA_HEADS))
    p["mla_kv_norm_g"] = row(w["mla_kv_norm_g"][l])
    p["mla_gkr"] = _lane_pad(w["mla_kr_g"][l], KPE_LANE)
    p["mla_wk"] = cast(_head_blocks(w["mla_w_kv_up"][l], MLA_NOPE + MLA_V, MLA_NOPE))
    p["mla_wv"] = cast(w["mla_w_kv_up"][l].reshape(MLA_KV_RANK, MLA_HEADS, MLA_NOPE + MLA_V)[:, :, MLA_NOPE:]
                       .reshape(MLA_KV_RANK, MLA_WIDTH))
    p["mla_wvt"] = p["mla_wv"].T
    p["mla_gk"] = _lane_pad(w["mla_kn_g"][l], 0)
    p["sconv_w"] = w["sconv_w"][l]
    p["band_gq"] = row(jnp.tile(w["band_qn_g"][l], BAND_HEADS))
    p["band_gk"] = row(jnp.tile(w["band_kn_g"][l], BAND_HEADS))
    nkey, nqry = BAND_ROWS + V7X_LANES, V7X_LANES
    key = np.arange(nkey)[:, None]
    qry = np.arange(nqry)[None, :]
    in_band = np.logical_and(key // CHUNK >= qry // CHUNK, key // CHUNK <= qry // CHUNK + BAND_PAST_CHUNKS)
    diff = np.arange(nqry - 1 + BAND_ROWS, -nkey + BAND_ROWS, -1)
    by_diff = w["band_rel_bias"][l][:, np.clip(diff, -REL_CLIP, REL_CLIP) + REL_CLIP]
    skew = jnp.tile(jnp.pad(by_diff, ((0, 0), (0, 1)))[:, None, :], (1, nqry, 1)).reshape(BAND_HEADS, -1)
    skew = skew[:, :nqry * (nkey + nqry - 1)].reshape(BAND_HEADS, nqry, nkey + nqry - 1)
    bias = jnp.swapaxes(skew[:, :, nqry - 1:], 1, 2) * math.log2(math.e)
    p["band_bias"] = jnp.where(in_band[None], bias, NEG_BIG)
    return p


def _rope_tables(pos, rows):
    inv = ROPE_THETA ** (-jnp.arange(0, MLA_ROPE, 2, dtype=F32) / MLA_ROPE)
    ang = pos.astype(F32)[:, None] * inv[None, :]
    cos, sin = jnp.cos(ang), jnp.sin(ang)
    half = MLA_ROPE // 2
    z = lambda w: jnp.zeros((pos.shape[0], w), F32)
    tail = V7X_LANES - MLA_NOPE - MLA_ROPE
    cos_t = jnp.concatenate([jnp.ones((pos.shape[0], MLA_NOPE), F32), cos, cos, z(tail)], axis=1)
    sin_a = jnp.concatenate([z(MLA_NOPE + half), sin, z(tail)], axis=1)
    sin_b = jnp.concatenate([z(MLA_NOPE), -sin, z(half + tail)], axis=1)
    reps = max(rows // pos.shape[0], 1)
    return tuple(jnp.tile(a, (reps, 1)) for a in (cos_t, sin_a, sin_b))


def _layer(x, bsz, t, past, p, tabs, q_off, layer):
    n = bsz * t
    tm = min(n, 512)
    proj = norm_matmul_nt(x, p["norm_mix_g"], p["w_in_t"], min(n, 1024), 1920)
    ya, ssm_new, ssd_conv_new = ssd_branch(proj, bsz, t, past["ssd_conv"], past["ssm"], p)
    yc, sconv_new = sconv_branch(proj, bsz, t, past["sconv"], p["sconv_w"])
    q, ckv, kpe_pad = mla_prep(proj, n, tm, tabs, p)
    q3, ckv3, kpe3 = q.reshape(bsz, t, -1), ckv.reshape(bsz, t, MLA_KV_RANK), kpe_pad.reshape(bsz, t, V7X_LANES)
    if past["mla_ckv"] is not None:
        assert t <= V7X_LANES
        kk, vv = mla_kv_cached(past["mla_ckv"], past["mla_kpe"], layer, ckv3, kpe3, p)
        ob = mla_attention_rows(q3, kk, vv, q_off).reshape(n, MLA_WIDTH)
    else:
        tkc = next(c for c in (256, 128, t) if t % c == 0)
        kk, vt = mla_kv(ckv3, kpe3, tkc, p)
        ob = mla_attention(q3, kk, vt, q_off, min(t, 512), 8)
    od, kn = band_branch(proj, past["band_k"], past["band_v"], layer, bsz, t, p["band_bias"], p["band_gq"], p["band_gk"])
    if ob.ndim != od.ndim or t % tm:
        ob = ob if ob.ndim == 2 else jnp.swapaxes(ob, 1, 2).reshape(n, MLA_WIDTH)
        od = jnp.swapaxes(od, 1, 2).reshape(n, BAND_WIDTH)
    x = merge(x, proj, ya, ob, yc, od, p["b_gate"], p["w_a_out"], p["w_b_out"], p["w_c_out"], p["w_d_out"], p["w_o"], tm)
    x = ffn(x, p["norm_ffn_g"], p["w_ffn_up"], p["w_ffn_down"], tm, FF_HIDDEN // 2)
    dv0 = SEG["d_v"][0]
    keep = min(BAND_ROWS, t)
    heads = lambda a: a.reshape(bsz, keep, BAND_HEADS, BAND_HEAD_DIM)
    kn = heads(kn.reshape(bsz, t, BAND_WIDTH)[:, t - keep:])
    vn = heads(proj.reshape(bsz, t, PROJ_COLS)[:, t - keep:, dv0:dv0 + BAND_WIDTH])
    new = {"ssm": ssm_new, "ssd_conv": ssd_conv_new, "sconv": sconv_new,
           "mla_ckv": ckv.reshape(bsz, t, MLA_KV_RANK),
           "mla_kpe": kpe_pad[:, KPE_LANE:KPE_LANE + MLA_ROPE].reshape(bsz, t, MLA_ROPE),
           "band_k": kn, "band_v": vn}
    return x, new


def kernel(x_prompt, x_sample, state_ssm, state_ssd_conv, cache_mla_ckv, cache_mla_kpe, state_sconv, cache_band_k, cache_band_v, norm_mix_g, w_in, b_gate, ssd_conv_w, ssd_conv_b, ssd_dt_bias, ssd_a_log, ssd_d, ssd_norm_g, w_a_out, mla_q_norm_g, mla_w_q_up, mla_kv_norm_g, mla_w_kv_up, mla_qn_g, mla_kn_g, mla_qr_g, mla_kr_g, w_b_out, sconv_w, w_c_out, band_qn_g, band_kn_g, band_rel_bias, w_d_out, w_o, norm_ffn_g, w_ffn_up, w_ffn_down):
    weights = dict(norm_mix_g=norm_mix_g, w_in=w_in, b_gate=b_gate, ssd_conv_w=ssd_conv_w, ssd_conv_b=ssd_conv_b,
                   ssd_dt_bias=ssd_dt_bias, ssd_a_log=ssd_a_log, ssd_d=ssd_d, ssd_norm_g=ssd_norm_g, w_a_out=w_a_out,
                   mla_q_norm_g=mla_q_norm_g, mla_w_q_up=mla_w_q_up, mla_kv_norm_g=mla_kv_norm_g,
                   mla_w_kv_up=mla_w_kv_up, mla_qn_g=mla_qn_g, mla_kn_g=mla_kn_g, mla_qr_g=mla_qr_g,
                   mla_kr_g=mla_kr_g, w_b_out=w_b_out, sconv_w=sconv_w, w_c_out=w_c_out, band_qn_g=band_qn_g,
                   band_kn_g=band_kn_g, band_rel_bias=band_rel_bias, w_d_out=w_d_out, w_o=w_o,
                   norm_ffn_g=norm_ffn_g, w_ffn_up=w_ffn_up, w_ffn_down=w_ffn_down)
    depth = w_in.shape[0]
    b_p, t_p, d = x_prompt.shape
    b_s, t_s, _ = x_sample.shape
    past_len = cache_mla_ckv.shape[2]
    assert d == D_MODEL and t_p % BAND_ROWS == 0 and t_s == CHUNK and cache_band_k.shape[2] == BAND_ROWS
    assert past_len % CHUNK == 0
    tabs_p = _rope_tables(jnp.arange(t_p, dtype=jnp.int32), min(b_p * t_p, 512))
    tabs_s = _rope_tables(past_len + jnp.arange(t_s, dtype=jnp.int32), min(b_s * t_s, 512))
    y_p = x_prompt.reshape(b_p * t_p, d)
    y_s = x_sample.reshape(b_s * t_s, d)
    new_p, new_s = [], []
    band_k_rows = jnp.transpose(cache_band_k, (0, 1, 3, 4, 2)).reshape(depth, b_s, BAND_WIDTH, BAND_ROWS)
    band_v_rows = jnp.transpose(cache_band_v, (0, 1, 3, 4, 2)).reshape(depth, b_s, BAND_WIDTH, BAND_ROWS)
    kpe_cache_t = jnp.swapaxes(cache_mla_kpe, 2, 3)
    for l in range(depth):
        p = _layer_params(l, weights)
        init_p = {"ssm": jnp.zeros((b_p, SSD_HEADS, SSD_HEAD_DIM, SSD_STATE), F32),
                  "ssd_conv": jnp.zeros((b_p, SSD_CONV - 1, SSD_CONV_DIM), F32),
                  "sconv": jnp.zeros((b_p, SCONV_K - 1, SCONV_WIDTH), F32),
                  "mla_ckv": None, "mla_kpe": None, "band_k": None, "band_v": None}
        y_p, st_p = _layer(y_p, b_p, t_p, init_p, p, tabs_p, 0, l)
        new_p.append(st_p)
        past_s = {"ssm": state_ssm[l], "ssd_conv": state_ssd_conv[l], "sconv": state_sconv[l],
                  "mla_ckv": cache_mla_ckv, "mla_kpe": kpe_cache_t,
                  "band_k": band_k_rows, "band_v": band_v_rows}
        y_s, st_s = _layer(y_s, b_s, t_s, past_s, p, tabs_s, past_len, l)
        new_s.append(st_s)

    def stack(states, name):
        return jnp.stack([s[name] for s in states], axis=0)

    out = [y_p.reshape(b_p, t_p, d), y_s.reshape(b_s, t_s, d)]
    for name in ("ssm", "ssd_conv", "mla_ckv", "mla_kpe", "sconv"):
        out += [stack(new_p, name), stack(new_s, name)]
    for name, cache in (("band_k", cache_band_k), ("band_v", cache_band_v)):
        out += [stack(new_p, name), jnp.concatenate([cache[:, :, t_s:], stack(new_s, name)], axis=2)]
    return tuple(out)
```

```python
import functools
import math

import jax
import jax.numpy as jnp
import numpy as np
from jax import lax
from jax.experimental import pallas as pl
from jax.experimental.pallas import tpu as pltpu

F32 = jnp.float32
MXU_DTYPE = jnp.bfloat16
EPS = 1e-6
NEG_BIG = -1e30

V7X_LANES = 128
V7X_VMEM_LIMIT_BYTES = 56 * 1024 * 1024

D_MODEL = 1024
CHUNK = 64
N_BRANCH = 4
SSD_HEADS, SSD_HEAD_DIM, SSD_GROUPS, SSD_STATE, SSD_CONV = 8, 64, 2, 128, 4
SSD_INNER = SSD_HEADS * SSD_HEAD_DIM
SSD_CONV_DIM = SSD_INNER + 2 * SSD_GROUPS * SSD_STATE
MLA_HEADS, MLA_Q_RANK, MLA_KV_RANK, MLA_NOPE, MLA_ROPE, MLA_V = 8, 384, 256, 64, 32, 64
MLA_WIDTH = MLA_HEADS * MLA_V
ROPE_THETA = 10000.0
SCONV_WIDTH, SCONV_K = 512, 3
BAND_HEADS, BAND_HEAD_DIM, BAND_PAST_CHUNKS, REL_CLIP = 8, 64, 8, 128
BAND_WIDTH = BAND_HEADS * BAND_HEAD_DIM
BAND_ROWS = BAND_PAST_CHUNKS * CHUNK
BAND_SPAN = BAND_ROWS + CHUNK
FF_HIDDEN = ((8 * D_MODEL // 3 + 255) // 256) * 256

SEG = {
    "gate": (0, 4096), "c_b": (4096, 512), "c_c": (4608, 512), "c_x": (5120, 512),
    "d_q": (5632, 512), "d_k": (6144, 512), "d_v": (6656, 512),
    "a_z": (7168, 512), "a_x": (7680, 512), "a_b": (8192, 256), "a_c": (8448, 256),
    "b_kvl": (8704, 256), "a_dt": (8960, 128), "b_kpe": (9088, 128), "b_ql": (9216, 384),
}
PROJ_COLS = 9600
KPE_LANE = 64


def _cparams(*sem):
    return pltpu.CompilerParams(dimension_semantics=sem, vmem_limit_bytes=V7X_VMEM_LIMIT_BYTES)


def _mm(a, b):
    return jnp.dot(a.astype(MXU_DTYPE), b.astype(MXU_DTYPE), preferred_element_type=F32)


def _mm_nt(a, b):
    return lax.dot_general(a.astype(MXU_DTYPE), b.astype(MXU_DTYPE), (((1,), (1,)), ((), ())),
                           preferred_element_type=F32)


def _mm_tn(a, b):
    return lax.dot_general(a.astype(MXU_DTYPE), b.astype(MXU_DTYPE), (((0,), (0,)), ((), ())),
                           preferred_element_type=F32)


def _split3(x):
    hi = x.astype(MXU_DTYPE)
    r1 = x - hi.astype(F32)
    mid = r1.astype(MXU_DTYPE)
    lo = (r1 - mid.astype(F32)).astype(MXU_DTYPE)
    return hi, mid, lo


def _mm_exact_rhs01(x, sel):
    hi, mid, lo = _split3(x)
    sel = sel.astype(MXU_DTYPE)
    d = functools.partial(jnp.dot, preferred_element_type=F32)
    return d(hi, sel) + d(mid, sel) + d(lo, sel)


def _silu(x):
    return x * jax.nn.sigmoid(x)


def _row_rms(x):
    return x * lax.rsqrt(jnp.mean(x * x, axis=-1, keepdims=True) + EPS)


def _lane_iota(rows):
    return lax.broadcasted_iota(jnp.int32, (rows, V7X_LANES), 1)


def _head64_rms(x):
    rows, width = x.shape
    low = _lane_iota(rows) < 64
    out = []
    for j in range(width // V7X_LANES):
        blk = x[:, j * V7X_LANES:(j + 1) * V7X_LANES]
        sq = blk * blk
        s_lo = jnp.sum(jnp.where(low, sq, 0.0), axis=-1, keepdims=True) * (1.0 / 64)
        s_hi = jnp.sum(jnp.where(low, 0.0, sq), axis=-1, keepdims=True) * (1.0 / 64)
        out.append(blk * jnp.where(low, lax.rsqrt(s_lo + EPS), lax.rsqrt(s_hi + EPS)))
    return jnp.concatenate(out, axis=-1)


def _norm_matmul_nt_kernel(x_ref, g_ref, wt_ref, o_ref, h_s):
    @pl.when(pl.program_id(1) == 0)
    def _():
        h_s[...] = (_row_rms(x_ref[...]) * g_ref[...]).astype(h_s.dtype)

    o_ref[...] = _mm_nt(h_s[...], wt_ref[...])


def norm_matmul_nt(x, g, wt, tm, tn):
    n, k = x.shape
    c = wt.shape[0]
    return pl.pallas_call(
        _norm_matmul_nt_kernel, out_shape=jax.ShapeDtypeStruct((n, c), F32), grid=(n // tm, c // tn),
        in_specs=[pl.BlockSpec((tm, k), lambda i, j: (i, 0)), pl.BlockSpec((1, k), lambda i, j: (0, 0)),
                  pl.BlockSpec((tn, k), lambda i, j: (j, 0))],
        out_specs=pl.BlockSpec((tm, tn), lambda i, j: (i, j)),
        scratch_shapes=[pltpu.VMEM((tm, k), MXU_DTYPE)],
        compiler_params=_cparams("parallel", "arbitrary"), name="in_proj_matmul")(x, g.reshape(1, k), wt)


def _merge_kernel(x_ref, g_ref, ya_ref, ob_ref, yc_ref, od_ref, bg_ref, wa_ref, wb_ref, wc_ref, wd_ref, wo_ref, o_ref,
                  *, bd_transposed):
    def gate(k):
        return jax.nn.sigmoid(g_ref[:, k * D_MODEL:(k + 1) * D_MODEL] + bg_ref[:, k * D_MODEL:(k + 1) * D_MODEL])

    def attn_out(a_ref, w_ref):
        return _mm_tn(a_ref[0], w_ref[...]) if bd_transposed else _mm(a_ref[...], w_ref[...])

    merged = gate(0) * _mm(ya_ref[...], wa_ref[...])
    merged = merged + gate(1) * attn_out(ob_ref, wb_ref)
    merged = merged + gate(2) * _mm(yc_ref[...], wc_ref[...])
    merged = merged + gate(3) * attn_out(od_ref, wd_ref)
    o_ref[...] = x_ref[...] + _mm(merged, wo_ref[...])


def merge(x, proj, ya, ob, yc, od, b_gate, wa, wb, wc, wd, wo, tm):
    n, d = x.shape
    row = lambda w: pl.BlockSpec((tm, w), lambda i: (i, 0))
    full = lambda a: pl.BlockSpec(a.shape, lambda i: (0, 0))
    bd_transposed = ob.ndim == 3
    if bd_transposed:
        per_seq = ob.shape[2] // tm
        attn = pl.BlockSpec((1, ob.shape[1], tm), lambda i: (i // per_seq, 0, i % per_seq))
    else:
        attn = row(512)
    bg = b_gate.reshape(1, N_BRANCH * d)
    return pl.pallas_call(
        functools.partial(_merge_kernel, bd_transposed=bd_transposed),
        out_shape=jax.ShapeDtypeStruct((n, d), F32), grid=(n // tm,),
        in_specs=[row(d), pl.BlockSpec((tm, N_BRANCH * d), lambda i: (i, 0)), row(512), attn, row(512), attn,
                  full(bg), full(wa), full(wb), full(wc), full(wd), full(wo)],
        out_specs=row(d), compiler_params=_cparams("parallel"), name="merge")(
            x, proj, ya, ob, yc, od, bg, wa, wb, wc, wd, wo)


def _ffn_kernel(x_ref, g_ref, wg_ref, wv_ref, wd_ref, o_ref, h_s, acc_s):
    k = pl.program_id(1)

    @pl.when(k == 0)
    def _():
        h_s[...] = (_row_rms(x_ref[...]) * g_ref[...]).astype(h_s.dtype)
        acc_s[...] = jnp.zeros_like(acc_s)

    h = h_s[...]
    gate = jnp.dot(h, wg_ref[...], preferred_element_type=F32)
    val = jnp.dot(h, wv_ref[...], preferred_element_type=F32)
    acc_s[...] += _mm(_silu(gate) * val, wd_ref[...])

    @pl.when(k == pl.num_programs(1) - 1)
    def _():
        o_ref[...] = x_ref[...] + acc_s[...]


def ffn(x, g, w_up, w_down, tm, tk):
    n, d = x.shape
    nk = FF_HIDDEN // tk
    return pl.pallas_call(
        _ffn_kernel, out_shape=jax.ShapeDtypeStruct((n, d), F32), grid=(n // tm, nk),
        in_specs=[pl.BlockSpec((tm, d), lambda i, k: (i, 0)), pl.BlockSpec((1, d), lambda i, k: (0, 0)),
                  pl.BlockSpec((d, tk), lambda i, k: (0, k)), pl.BlockSpec((d, tk), lambda i, k: (0, nk + k)),
                  pl.BlockSpec((tk, d), lambda i, k: (k, 0))],
        out_specs=pl.BlockSpec((tm, d), lambda i, k: (i, 0)),
        scratch_shapes=[pltpu.VMEM((tm, d), MXU_DTYPE), pltpu.VMEM((tm, d), F32)],
        compiler_params=_cparams("parallel", "arbitrary"), name="ffn")(x, g.reshape(1, d), w_up, w_up, w_down)


def _ssd_kernel(z_ref, x_ref, b_ref, c_ref, dt_ref, conv0_ref, h0_ref, cw_ref, cbias_ref, dtb_ref, alog_ref,
                dfull_ref, ng_ref, tri_ref, expand_ref, y_ref, convn_ref, hn_ref, xp_s, h_s):
    L = CHUNK
    rows = z_ref.shape[0]
    c = pl.program_id(1)

    @pl.when(c == 0)
    def _():
        xp_s[0:8, :] = conv0_ref[0]
        h_s[...] = h0_ref[0]

    xp_s[8:8 + rows, 0:512] = x_ref[...]
    xp_s[8:8 + rows, 512:768] = b_ref[...]
    xp_s[8:8 + rows, 768:1024] = c_ref[...]
    base = 8 - (SSD_CONV - 1)
    acc = xp_s[base:base + rows, :] * cw_ref[0:1, :]
    for i in range(1, SSD_CONV):
        acc = acc + xp_s[base + i:base + i + rows, :] * cw_ref[i:i + 1, :]
    xbc = _silu(acc + cbias_ref[...])
    tail = xp_s[rows:rows + 8, :]
    convn_ref[0] = tail
    xp_s[0:8, :] = tail

    xs = xbc[:, 0:512]
    dtr = dt_ref[...] + dtb_ref[...]
    dt = jnp.maximum(dtr, 0.0) + jnp.log1p(jnp.exp(-jnp.abs(dtr)))
    dt_full = _mm_exact_rhs01(dt, expand_ref[...])
    da_full = dt_full * (-jnp.exp(alog_ref[...]))
    da_parts = _split3(da_full)
    dot = functools.partial(jnp.dot, preferred_element_type=F32)
    tri = tri_ref[...].astype(MXU_DTYPE)
    acs = dot(tri, da_parts[0]) + dot(tri, da_parts[1]) + dot(tri, da_parts[2])
    x_dt = xs * dt_full
    exp_acs = jnp.exp(acs)

    row = lax.broadcasted_iota(jnp.int32, (L, V7X_LANES), 0)
    lane = _lane_iota(L)
    lane_s = jnp.where(lane < 64, lane, lane - 64)
    diag2 = row == lane_s
    tril2 = lane_s <= row
    row2 = lax.broadcasted_iota(jnp.int32, (2 * L, V7X_LANES), 0)
    lane2 = lax.broadcasted_iota(jnp.int32, (2 * L, V7X_LANES), 1)
    blockdiag = (row2 < L) == (lane2 < 64)
    ones_ln = jnp.ones((L, SSD_STATE), MXU_DTYPE)
    dot_tn = functools.partial(lax.dot_general, dimension_numbers=(((0,), (0,)), ((), ())), preferred_element_type=F32)

    h = [h_s[256 * g:256 * (g + 1), :] for g in range(SSD_GROUPS)]
    y_chunks = []
    for k in range(rows // L):
        r = slice(k * L, (k + 1) * L)
        acs_k = acs[r]
        xd_end = x_dt[r] * jnp.exp(acs_k[L - 1:L, :] - acs_k)
        y_parts = []
        for g in range(SSD_GROUPS):
            cols = slice(256 * g, 256 * (g + 1))
            bg = xbc[r, 512 + 128 * g:512 + 128 * (g + 1)]
            cg = xbc[r, 768 + 128 * g:768 + 128 * (g + 1)]
            cb2 = _mm_nt(cg, jnp.concatenate([bg, bg], axis=0))
            y_off = _mm_nt(cg, h[g]) * exp_acs[r, cols]
            for jp in range(2):
                sl = slice(128 * (2 * g + jp), 128 * (2 * g + jp + 1))
                blk = acs_k[:, sl]
                at_s = jnp.sum(jnp.where(diag2, blk, 0.0), axis=0, keepdims=True)
                decay = jnp.where(tril2, jnp.exp(jnp.minimum(blk - at_s, 0.0)), 0.0)
                xpair = x_dt[r, sl]
                xblk = jnp.where(blockdiag, jnp.concatenate([xpair, xpair], axis=0), 0.0)
                y_parts.append(_mm(cb2 * decay, xblk) + y_off[:, 128 * jp:128 * (jp + 1)])
            state = _mm_tn(xd_end[:, cols], bg)
            total = sum(dot_tn(part[r, cols], ones_ln) for part in da_parts)
            h[g] = jnp.exp(total) * h[g] + state
        y_chunks.append(jnp.concatenate(y_parts, axis=-1))
    y = jnp.concatenate(y_chunks, axis=0) + dfull_ref[...] * xs
    gated = y * _silu(z_ref[...])
    y_ref[...] = (_row_rms(gated) * ng_ref[...]).astype(y_ref.dtype)
    for g in range(SSD_GROUPS):
        h_s[256 * g:256 * (g + 1), :] = h[g]
        hn_ref[0, 256 * g:256 * (g + 1), :] = h[g]


def ssd_branch(proj, bsz, t, conv_prev, ssm_prev, p):
    rows = math.gcd(t, 4 * CHUNK)
    nb = t // rows
    n = bsz * t
    col = lambda name: SEG[name][0] // SEG[name][1]
    seg = lambda name: pl.BlockSpec((rows, SEG[name][1]), lambda b, c, _j=col(name): (b * nb + c, _j))
    full = lambda a: pl.BlockSpec(a.shape, lambda b, c: (0,) * a.ndim)
    conv0 = jnp.pad(conv_prev, ((0, 0), (8 - (SSD_CONV - 1), 0), (0, 0)))
    h0 = ssm_prev.reshape(bsz, SSD_INNER, SSD_STATE)
    tri = jnp.kron(jnp.eye(rows // CHUNK, dtype=F32), jnp.tril(jnp.ones((CHUNK, CHUNK), F32)))
    consts = [p["ssd_conv_w"], p["ssd_conv_b"], p["ssd_dt_bias"], p["ssd_a_log"], p["ssd_d"], p["ssd_norm_g"], tri,
              p["ssd_expand"]]
    y, convn, hn = pl.pallas_call(
        _ssd_kernel,
        out_shape=[jax.ShapeDtypeStruct((n, SSD_INNER), MXU_DTYPE), jax.ShapeDtypeStruct((bsz, 8, SSD_CONV_DIM), F32),
                   jax.ShapeDtypeStruct((bsz, SSD_INNER, SSD_STATE), F32)],
        grid=(bsz, nb),
        in_specs=[seg("a_z"), seg("a_x"), seg("a_b"), seg("a_c"), seg("a_dt"),
                  pl.BlockSpec((1, 8, SSD_CONV_DIM), lambda b, c: (b, 0, 0)),
                  pl.BlockSpec((1, SSD_INNER, SSD_STATE), lambda b, c: (b, 0, 0))] + [full(a) for a in consts],
        out_specs=[pl.BlockSpec((rows, SSD_INNER), lambda b, c: (b * nb + c, 0)),
                   pl.BlockSpec((1, 8, SSD_CONV_DIM), lambda b, c: (b, 0, 0)),
                   pl.BlockSpec((1, SSD_INNER, SSD_STATE), lambda b, c: (b, 0, 0))],
        scratch_shapes=[pltpu.VMEM((rows + 8, SSD_CONV_DIM), F32), pltpu.VMEM((SSD_INNER, SSD_STATE), F32)],
        compiler_params=_cparams("parallel", "arbitrary"), name="ssd_branch")(
            proj, proj, proj, proj, proj, conv0, h0, *consts)
    return y, hn.reshape(bsz, SSD_HEADS, SSD_HEAD_DIM, SSD_STATE), convn[:, 8 - (SSD_CONV - 1):]


def _sconv_kernel(cb_ref, cc_ref, cx_ref, st0_ref, w_ref, o_ref, stn_ref, up_s):
    rows = cb_ref.shape[0]

    @pl.when(pl.program_id(1) == 0)
    def _():
        up_s[0:8, :] = st0_ref[0]

    up_s[8:8 + rows, :] = cc_ref[...] * cx_ref[...]
    base = 8 - (SCONV_K - 1)
    uc = up_s[base:base + rows, :] * w_ref[0:1, :]
    for i in range(1, SCONV_K):
        uc = uc + up_s[base + i:base + i + rows, :] * w_ref[i:i + 1, :]
    o_ref[...] = (cb_ref[...] * uc).astype(o_ref.dtype)
    tail = up_s[rows:rows + 8, :]
    stn_ref[0] = tail
    up_s[0:8, :] = tail


def sconv_branch(proj, bsz, t, prev, w):
    rows = min(t, 512)
    nt = t // rows
    col = lambda name: SEG[name][0] // SCONV_WIDTH
    seg = lambda name: pl.BlockSpec((rows, SCONV_WIDTH), lambda b, c, _j=col(name): (b * nt + c, _j))
    st0 = jnp.pad(prev, ((0, 0), (8 - (SCONV_K - 1), 0), (0, 0)))
    o, stn = pl.pallas_call(
        _sconv_kernel,
        out_shape=[jax.ShapeDtypeStruct((bsz * t, SCONV_WIDTH), MXU_DTYPE), jax.ShapeDtypeStruct((bsz, 8, SCONV_WIDTH), F32)],
        grid=(bsz, nt),
        in_specs=[seg("c_b"), seg("c_c"), seg("c_x"), pl.BlockSpec((1, 8, SCONV_WIDTH), lambda b, c: (b, 0, 0)),
                  pl.BlockSpec(w.shape, lambda b, c: (0, 0))],
        out_specs=[pl.BlockSpec((rows, SCONV_WIDTH), lambda b, c: (b * nt + c, 0)),
                   pl.BlockSpec((1, 8, SCONV_WIDTH), lambda b, c: (b, 0, 0))],
        scratch_shapes=[pltpu.VMEM((rows + 8, SCONV_WIDTH), F32)],
        compiler_params=_cparams("parallel", "arbitrary"), name="sconv_branch")(proj, proj, proj, st0, w)
    return o, stn[:, 8 - (SCONV_K - 1):]


def _rope(y, cos_t, sin_a, sin_b):
    return y * cos_t + pltpu.roll(y, 16, 1) * sin_a + pltpu.roll(y, V7X_LANES - 16, 1) * sin_b


def _mla_prep_kernel(ql_ref, kvl_ref, kpe_ref, cos_ref, sa_ref, sb_ref, gq_ref, wq_ref, gqh_ref, gkv_ref, gkr_ref,
                     q_ref, ckv_ref, kpeo_ref):
    rows = ql_ref.shape[0]
    cos_t, sin_a, sin_b = cos_ref[...], sa_ref[...], sb_ref[...]
    q = _mm(_row_rms(ql_ref[...]) * gq_ref[...], wq_ref[...])
    lane = _lane_iota(rows)
    nope = lane < MLA_NOPE
    scale = (MLA_NOPE + MLA_ROPE) ** -0.5 * math.log2(math.e)
    for h in range(MLA_HEADS):
        sl = slice(h * V7X_LANES, (h + 1) * V7X_LANES)
        blk = q[:, sl]
        sq = blk * blk
        s_n = jnp.sum(jnp.where(nope, sq, 0.0), axis=-1, keepdims=True) * (1.0 / MLA_NOPE)
        s_r = jnp.sum(jnp.where(nope, 0.0, sq), axis=-1, keepdims=True) * (1.0 / MLA_ROPE)
        y = blk * jnp.where(nope, lax.rsqrt(s_n + EPS), lax.rsqrt(s_r + EPS)) * gqh_ref[:, sl]
        q_ref[:, sl] = (_rope(y, cos_t, sin_a, sin_b) * scale).astype(q_ref.dtype)
    ckv_ref[...] = _row_rms(kvl_ref[...]) * gkv_ref[...]
    kp = kpe_ref[...]
    ms = jnp.sum(kp * kp, axis=-1, keepdims=True) * (1.0 / MLA_ROPE)
    kpeo_ref[...] = _rope(kp * lax.rsqrt(ms + EPS) * gkr_ref[...], cos_t, sin_a, sin_b)


def mla_prep(proj, n, tm, tabs, p):
    cos_t, sin_a, sin_b = tabs
    ntab = cos_t.shape[0] // tm
    seg = lambda name: pl.BlockSpec((tm, SEG[name][1]), lambda i, _j=SEG[name][0] // SEG[name][1]: (i, _j))
    tab = pl.BlockSpec((tm, V7X_LANES), lambda i: (i % ntab, 0))
    full = lambda a: pl.BlockSpec(a.shape, lambda i: (0, 0))
    consts = [p["mla_q_norm_g"], p["mla_wq"], p["mla_gq_head"], p["mla_kv_norm_g"], p["mla_gkr"]]
    return pl.pallas_call(
        _mla_prep_kernel,
        out_shape=[jax.ShapeDtypeStruct((n, MLA_HEADS * V7X_LANES), MXU_DTYPE),
                   jax.ShapeDtypeStruct((n, MLA_KV_RANK), F32), jax.ShapeDtypeStruct((n, V7X_LANES), F32)],
        grid=(n // tm,),
        in_specs=[seg("b_ql"), seg("b_kvl"), seg("b_kpe"), tab, tab, tab] + [full(a) for a in consts],
        out_specs=[pl.BlockSpec((tm, MLA_HEADS * V7X_LANES), lambda i: (i, 0)),
                   pl.BlockSpec((tm, MLA_KV_RANK), lambda i: (i, 0)), pl.BlockSpec((tm, V7X_LANES), lambda i: (i, 0))],
        compiler_params=_cparams("parallel"), name="mla_prep")(proj, proj, proj, cos_t, sin_a, sin_b, *consts)


def _mla_kv_kernel(ckv_ref, kpe_ref, wk_ref, wvt_ref, gk_ref, k_ref, vt_ref):
    c = ckv_ref[0].astype(MXU_DTYPE)
    kk = jnp.dot(c, wk_ref[...], preferred_element_type=F32)
    vt_ref[0, 0] = _mm_nt(wvt_ref[...], c).astype(vt_ref.dtype)
    kpe = kpe_ref[0]
    for h in range(MLA_HEADS):
        sl = slice(h * V7X_LANES, (h + 1) * V7X_LANES)
        blk = kk[:, sl]
        ms = jnp.sum(blk * blk, axis=-1, keepdims=True) * (1.0 / MLA_NOPE)
        k_ref[0, :, sl] = (blk * lax.rsqrt(ms + EPS) * gk_ref[...] + kpe).astype(k_ref.dtype)


def mla_kv(ckv_all, kpe_all, ts, p):
    bsz, s, _ = ckv_all.shape
    full = lambda a: pl.BlockSpec(a.shape, lambda b, i: (0, 0))
    consts = [p["mla_wk"], p["mla_wvt"], p["mla_gk"]]
    return pl.pallas_call(
        _mla_kv_kernel,
        out_shape=[jax.ShapeDtypeStruct((bsz, s, MLA_HEADS * V7X_LANES), MXU_DTYPE),
                   jax.ShapeDtypeStruct((bsz, s // ts, MLA_WIDTH, ts), MXU_DTYPE)],
        grid=(bsz, s // ts),
        in_specs=[pl.BlockSpec((1, ts, MLA_KV_RANK), lambda b, i: (b, i, 0)),
                  pl.BlockSpec((1, ts, V7X_LANES), lambda b, i: (b, i, 0))] + [full(a) for a in consts],
        out_specs=[pl.BlockSpec((1, ts, MLA_HEADS * V7X_LANES), lambda b, i: (b, i, 0)),
                   pl.BlockSpec((1, 1, MLA_WIDTH, ts), lambda b, i: (b, i, 0, 0))],
        compiler_params=_cparams("parallel", "parallel"), name="mla_kv")(ckv_all, kpe_all, *consts)


def _mla_kv_cached_kernel(ckv_old_ref, ckv_new_ref, kpe_old_ref, kpe_new_ref, wk_ref, wv_ref, gk_ref, k_ref, v_ref):
    c = jnp.concatenate([ckv_old_ref[0, 0], ckv_new_ref[0]], axis=0).astype(MXU_DTYPE)
    old_t = kpe_old_ref[0, 0]
    tail = V7X_LANES - KPE_LANE - MLA_ROPE
    old_t = jnp.concatenate([jnp.zeros((KPE_LANE, old_t.shape[1]), F32), old_t, jnp.zeros((tail, old_t.shape[1]), F32)], axis=0)
    kpe = jnp.concatenate([old_t.T, kpe_new_ref[0]], axis=0)
    kk = jnp.dot(c, wk_ref[...], preferred_element_type=F32)
    v_ref[0] = jnp.dot(c, wv_ref[...], preferred_element_type=F32).astype(v_ref.dtype)
    for h in range(MLA_HEADS):
        sl = slice(h * V7X_LANES, (h + 1) * V7X_LANES)
        blk = kk[:, sl]
        ms = jnp.sum(blk * blk, axis=-1, keepdims=True) * (1.0 / MLA_NOPE)
        k_ref[0, :, sl] = (blk * lax.rsqrt(ms + EPS) * gk_ref[...] + kpe).astype(k_ref.dtype)


def mla_kv_cached(ckv_cache, kpe_cache, layer, ckv_new, kpe_new, p):
    _, bsz, past, _ = ckv_cache.shape
    t = ckv_new.shape[1]
    s = past + t
    full = lambda a: pl.BlockSpec(a.shape, lambda b: (0, 0))
    consts = [p["mla_wk"], p["mla_wv"], p["mla_gk"]]
    return pl.pallas_call(
        _mla_kv_cached_kernel,
        out_shape=[jax.ShapeDtypeStruct((bsz, s, MLA_HEADS * V7X_LANES), MXU_DTYPE),
                   jax.ShapeDtypeStruct((bsz, s, MLA_WIDTH), MXU_DTYPE)],
        grid=(bsz,),
        in_specs=[pl.BlockSpec((1, 1, past, MLA_KV_RANK), lambda b: (layer, b, 0, 0)),
                  pl.BlockSpec((1, t, MLA_KV_RANK), lambda b: (b, 0, 0)),
                  pl.BlockSpec((1, 1, MLA_ROPE, past), lambda b: (layer, b, 0, 0)),
                  pl.BlockSpec((1, t, V7X_LANES), lambda b: (b, 0, 0))] + [full(a) for a in consts],
        out_specs=[pl.BlockSpec((1, s, MLA_HEADS * V7X_LANES), lambda b: (b, 0, 0)),
                   pl.BlockSpec((1, s, MLA_WIDTH), lambda b: (b, 0, 0))],
        compiler_params=_cparams("parallel"), name="mla_kv_cached")(ckv_cache, ckv_new, kpe_cache, kpe_new, *consts)


def _mla_attn_rows_kernel(q_ref, k_ref, v_ref, o_ref, *, q_off):
    t, s = q_ref.shape[1], k_ref.shape[1]
    low = _lane_iota(t) < MLA_V
    needs_mask = (s - 1) // CHUNK > q_off // CHUNK
    if needs_mask:
        q_chunk = (q_off + lax.broadcasted_iota(jnp.int32, (t, s), 0)) // CHUNK
        visible = lax.broadcasted_iota(jnp.int32, (t, s), 1) // CHUNK <= q_chunk
    for hp in range(MLA_HEADS // 2):
        v_pair = v_ref[0, :, hp * V7X_LANES:(hp + 1) * V7X_LANES]
        outs = []
        for sub in range(2):
            sl = slice((2 * hp + sub) * V7X_LANES, (2 * hp + sub + 1) * V7X_LANES)
            sc = _mm_nt(q_ref[0, :, sl], k_ref[0, :, sl])
            if needs_mask:
                sc = jnp.where(visible, sc, NEG_BIG)
            e = jnp.exp2(sc - jnp.max(sc, axis=-1, keepdims=True))
            outs.append(_mm(e, v_pair) / jnp.sum(e, axis=-1, keepdims=True))
        o_ref[0, :, hp * V7X_LANES:(hp + 1) * V7X_LANES] = jnp.where(low, outs[0], outs[1]).astype(o_ref.dtype)


def mla_attention_rows(q, k, v, q_off):
    bsz, t, _ = q.shape
    s = k.shape[1]
    return pl.pallas_call(
        functools.partial(_mla_attn_rows_kernel, q_off=q_off),
        out_shape=jax.ShapeDtypeStruct((bsz, t, MLA_WIDTH), MXU_DTYPE), grid=(bsz,),
        in_specs=[pl.BlockSpec((1, t, MLA_HEADS * V7X_LANES), lambda b: (b, 0, 0)),
                  pl.BlockSpec((1, s, MLA_HEADS * V7X_LANES), lambda b: (b, 0, 0)),
                  pl.BlockSpec((1, s, MLA_WIDTH), lambda b: (b, 0, 0))],
        out_specs=pl.BlockSpec((1, t, MLA_WIDTH), lambda b: (b, 0, 0)),
        compiler_params=_cparams("parallel"), name="mla_attention_rows")(q, k, v)


def _mla_attn_kernel(q_ref, k_ref, vt_ref, ot_ref, sa_s, sb_s, *, tq, tkc, q_off, nh):
    q_start = q_off + pl.program_id(2) * tq
    n_full = (q_start + CHUNK) // tkc
    n_total = (q_start + tq + tkc - 1) // tkc
    lanes = lambda h: slice(h * V7X_LANES, (h + 1) * V7X_LANES)
    qs = [q_ref[0, :, lanes(h)] for h in range(nh)]
    ones = jnp.ones((8, tkc), MXU_DTYPE)

    def produce(c, buf):
        k0 = pl.multiple_of(c * tkc, tkc)
        for h in range(nh):
            buf[h] = _mm_nt(k_ref[0, pl.ds(k0, tkc), lanes(h)], qs[h])

    def consume(c, buf, stats, masked):
        if masked:
            k_chunk = (c * tkc + lax.broadcasted_iota(jnp.int32, (tkc, tq), 0)) // CHUNK
            q_chunk = (q_start + lax.broadcasted_iota(jnp.int32, (tkc, tq), 1)) // CHUNK
            visible = k_chunk <= q_chunk
        out = []
        for h in range(nh):
            m, acc = stats[h]
            s = jnp.where(visible, buf[h], NEG_BIG) if masked else buf[h]
            m_new = jnp.maximum(m, jnp.max(s, axis=0, keepdims=True))
            alpha = jnp.exp2(m - m_new)
            p = jnp.exp2(s - m_new).astype(MXU_DTYPE)
            v_ext = jnp.concatenate([vt_ref[0, c, h * MLA_V:(h + 1) * MLA_V, :], ones], axis=0)
            out.append((m_new, alpha * acc + jnp.dot(v_ext, p, preferred_element_type=F32)))
        return tuple(out)

    def pair(i, stats):
        c = 2 * i
        produce(c + 1, sb_s)
        stats = consume(c, sa_s, stats, False)
        produce(jnp.minimum(c + 2, n_total - 1), sa_s)
        return consume(c + 1, sb_s, stats, False)

    def single(c, stats):
        produce(c, sb_s)
        return consume(c, sb_s, stats, True)

    init = tuple((jnp.full((1, tq), NEG_BIG, F32), jnp.zeros((MLA_V + 8, tq), F32)) for _ in range(nh))
    produce(0, sa_s)
    n_pairs = n_full // 2
    stats = lax.fori_loop(0, n_pairs, pair, init)
    stats = consume(2 * n_pairs, sa_s, stats, True)
    stats = lax.fori_loop(2 * n_pairs + 1, n_total, single, stats)
    for h in range(nh):
        acc = stats[h][1]
        ot_ref[0, h * MLA_V:(h + 1) * MLA_V, :] = (acc[:MLA_V] / acc[MLA_V:MLA_V + 1]).astype(ot_ref.dtype)


def mla_attention(q, k, vt, q_off, tq, nh):
    bsz, t, _ = q.shape
    _, nchunk, _, tkc = vt.shape
    s = k.shape[1]
    kern = functools.partial(_mla_attn_kernel, tq=tq, tkc=tkc, q_off=q_off, nh=nh)
    return pl.pallas_call(
        kern, out_shape=jax.ShapeDtypeStruct((bsz, MLA_WIDTH, t), MXU_DTYPE),
        grid=(bsz, MLA_HEADS // nh, t // tq),
        in_specs=[pl.BlockSpec((1, tq, nh * V7X_LANES), lambda b, h, qi: (b, qi, h)),
                  pl.BlockSpec((1, s, nh * V7X_LANES), lambda b, h, qi: (b, 0, h), pipeline_mode=pl.Buffered(1)),
                  pl.BlockSpec((1, nchunk, nh * MLA_V, tkc), lambda b, h, qi: (b, 0, h, 0),
                               pipeline_mode=pl.Buffered(1))],
        out_specs=pl.BlockSpec((1, nh * MLA_V, tq), lambda b, h, qi: (b, h, qi)),
        scratch_shapes=[pltpu.VMEM((nh, tkc, tq), F32), pltpu.VMEM((nh, tkc, tq), F32)],
        compiler_params=_cparams("parallel", "parallel", "parallel"), name="mla_attention")(q, k, vt)


def _band_kernel(q_ref, ka_ref, kb_ref, va_ref, vb_ref, bias_ref, gq_ref, gk_ref, ot_ref, kn_ref, q_s, k_s, vt_s,
                 *, a_is_cache):
    tb = q_ref.shape[0]
    rows_q = q_s.shape[0]
    pad = rows_q - tb
    qn = _head64_rms(q_ref[...]) * gq_ref[...] * (BAND_HEAD_DIM ** -0.5 * math.log2(math.e))
    kb = _head64_rms(kb_ref[...]) * gk_ref[...]
    kn_ref[...] = kb
    if a_is_cache:
        k_s[0:BAND_ROWS, :] = ka_ref[0, 0].T.astype(k_s.dtype)
        vt_s[:, 0:BAND_ROWS] = va_ref[0, 0].astype(vt_s.dtype)
    else:
        k_s[0:BAND_ROWS, :] = (_head64_rms(ka_ref[...]) * gk_ref[...]).astype(k_s.dtype)
        vt_s[:, 0:BAND_ROWS] = va_ref[...].T.astype(vt_s.dtype)
    vb = vb_ref[...]
    if pad:
        zeros = jnp.zeros((pad, BAND_WIDTH), F32)
        qn, kb, vb = (jnp.concatenate([a, zeros], axis=0) for a in (qn, kb, vb))
    q_s[...] = qn.astype(q_s.dtype)
    k_s[BAND_ROWS:, :] = kb.astype(k_s.dtype)
    vt_s[:, BAND_ROWS:] = vb.T.astype(vt_s.dtype)

    _, win, tw = bias_ref.shape
    low = _lane_iota(tw) < BAND_HEAD_DIM
    ones = jnp.ones((8, win), MXU_DTYPE)
    win_row = lax.broadcasted_iota(jnp.int32, (win, 2 * tw), 0)

    def tiles(first):
        for j in range(rows_q // tw):
            r0 = j * tw
            qj = q_s[r0:r0 + tw, :]
            kw = k_s[r0:r0 + win, :]
            ss = []
            for hp in range(BAND_HEADS // 2):
                sl = slice(hp * V7X_LANES, (hp + 1) * V7X_LANES)
                zero = jnp.zeros((), q_s.dtype)
                q2 = jnp.concatenate([jnp.where(low, qj[:, sl], zero), jnp.where(low, zero, qj[:, sl])], axis=0)
                ss.append(_mm_nt(kw[:, sl], q2))
            for hp in range(BAND_HEADS // 2):
                sl = slice(hp * V7X_LANES, (hp + 1) * V7X_LANES)
                v_ext = jnp.concatenate([vt_s[sl, r0:r0 + win], ones], axis=0)
                s = ss[hp] + jnp.concatenate([bias_ref[2 * hp], bias_ref[2 * hp + 1]], axis=1)
                if first:
                    s = jnp.where(win_row + r0 < BAND_ROWS, NEG_BIG, s)
                p = jnp.exp2(s - jnp.max(s, axis=0, keepdims=True)).astype(MXU_DTYPE)
                res = jnp.dot(v_ext, p, preferred_element_type=F32)
                den = res[V7X_LANES:V7X_LANES + 1]
                halves = [res[0:BAND_HEAD_DIM, 0:tw] / den[:, 0:tw], res[BAND_HEAD_DIM:V7X_LANES, tw:] / den[:, tw:]]
                o_pair = jnp.concatenate(halves, axis=0).astype(ot_ref.dtype)
                if pad:
                    ot_ref[0, sl, :] = o_pair[:, 0:tb]
                else:
                    ot_ref[0, sl, r0:r0 + tw] = o_pair

    if a_is_cache:
        tiles(False)
    else:
        pl.when(pl.program_id(1) == 0)(lambda: tiles(True))
        pl.when(pl.program_id(1) > 0)(lambda: tiles(False))


def band_branch(q_src, k_prev, v_prev, layer, bsz, t, bias, gq, gk):
    a_is_cache = k_prev is not None
    tb = min(t, BAND_ROWS)
    nt = t // tb
    tw = bias.shape[2]
    rows_q = -(-tb // tw) * tw
    col = lambda name: SEG[name][0] // BAND_WIDTH
    cur = lambda name: pl.BlockSpec((tb, BAND_WIDTH), lambda b, i, _j=col(name): (b * nt + i, _j))
    if a_is_cache:
        prev = lambda name: pl.BlockSpec((1, 1, BAND_WIDTH, BAND_ROWS), lambda b, i: (layer, b, 0, 0))
        ka, va = k_prev, v_prev
    else:
        prev = lambda name: pl.BlockSpec((BAND_ROWS, BAND_WIDTH), lambda b, i, _j=col(name): (b * nt + jnp.maximum(i - 1, 0), _j))
        ka, va = q_src, q_src
    full = lambda a: pl.BlockSpec(a.shape, lambda b, i: (0,) * a.ndim)
    kern = functools.partial(_band_kernel, a_is_cache=a_is_cache)
    return pl.pallas_call(
        kern,
        out_shape=[jax.ShapeDtypeStruct((bsz, BAND_WIDTH, t), MXU_DTYPE), jax.ShapeDtypeStruct((bsz * t, BAND_WIDTH), F32)],
        grid=(bsz, nt),
        in_specs=[cur("d_q"), prev("d_k"), cur("d_k"), prev("d_v"), cur("d_v"), full(bias), full(gq), full(gk)],
        out_specs=[pl.BlockSpec((1, BAND_WIDTH, tb), lambda b, i: (b, 0, i)),
                   pl.BlockSpec((tb, BAND_WIDTH), lambda b, i: (b * nt + i, 0))],
        scratch_shapes=[pltpu.VMEM((rows_q, BAND_WIDTH), MXU_DTYPE), pltpu.VMEM((BAND_ROWS + rows_q, BAND_WIDTH), MXU_DTYPE),
                        pltpu.VMEM((BAND_WIDTH, BAND_ROWS + rows_q), MXU_DTYPE)],
        compiler_params=_cparams("parallel", "parallel"), name="band_branch")(
            q_src, ka, q_src, va, q_src, bias, gq, gk)


def _head_blocks(w, per_head, used):
    k = w.shape[0]
    w = w.reshape(k, -1, per_head)[:, :, :used]
    return jnp.pad(w, ((0, 0), (0, 0), (0, V7X_LANES - used))).reshape(k, -1)


def _lane_pad(v, offset, width=V7X_LANES):
    return jnp.pad(v, (offset, width - offset - v.shape[0])).reshape(1, width)


def _layer_params(l, w):
    src = {}
    off = 0
    for name, width in (("a_z", 512), ("a_x", 512), ("a_b", 256), ("a_c", 256), ("a_dt", 8), ("b_ql", 384),
                        ("b_kvl", 256), ("b_kpe", 32), ("c_b", 512), ("c_c", 512), ("c_x", 512),
                        ("d_q", 512), ("d_k", 512), ("d_v", 512), ("gate", 4096)):
        src[name] = (off, width)
        off += width
    w_in_t = w["w_in"][l].T.astype(MXU_DTYPE)
    pieces = []
    for name, (dst, dwidth) in sorted(SEG.items(), key=lambda kv: kv[1][0]):
        s0, sw = src[name]
        lead = KPE_LANE if name == "b_kpe" else 0
        for width in (lead, None, dwidth - sw - lead):
            if width is None:
                pieces.append(w_in_t[s0:s0 + sw])
            elif width:
                pieces.append(jnp.zeros((width, D_MODEL), MXU_DTYPE))
    p = {"w_in_t": jnp.concatenate(pieces, axis=0)}
    cast = lambda a: a.astype(MXU_DTYPE)
    row = lambda a: a.reshape(1, -1)
    p["norm_mix_g"], p["norm_ffn_g"], p["b_gate"] = w["norm_mix_g"][l], w["norm_ffn_g"][l], w["b_gate"][l]
    for name in ("w_a_out", "w_b_out", "w_c_out", "w_d_out", "w_o", "w_ffn_up", "w_ffn_down"):
        p[name] = cast(w[name][l])
    p["ssd_conv_w"] = w["ssd_conv_w"][l]
    p["ssd_conv_b"] = row(w["ssd_conv_b"][l])
    p["ssd_dt_bias"] = _lane_pad(w["ssd_dt_bias"][l], 0)
    p["ssd_a_log"] = row(jnp.repeat(w["ssd_a_log"][l], SSD_HEAD_DIM))
    p["ssd_d"] = row(jnp.repeat(w["ssd_d"][l], SSD_HEAD_DIM))
    p["ssd_norm_g"] = row(w["ssd_norm_g"][l])
    head_of_lane = jnp.arange(SSD_INNER) // SSD_HEAD_DIM
    p["ssd_expand"] = (jnp.arange(V7X_LANES)[:, None] == head_of_lane[None, :]).astype(F32)
    p["mla_q_norm_g"] = row(w["mla_q_norm_g"][l])
    p["mla_wq"] = cast(_head_blocks(w["mla_w_q_up"][l], MLA_NOPE + MLA_ROPE, MLA_NOPE + MLA_ROPE))
    gqh = jnp.concatenate([w["mla_qn_g"][l], w["mla_qr_g"][l], jnp.zeros((V7X_LANES - MLA_NOPE - MLA_ROPE,), F32)])
    p["mla_gq_head"] = row(jnp.tile(gqh, MLA_HEADS))
    p["mla_kv_norm_g"] = row(w["mla_kv_norm_g"][l])
    p["mla_gkr"] = _lane_pad(w["mla_kr_g"][l], KPE_LANE)
    p["mla_wk"] = cast(_head_blocks(w["mla_w_kv_up"][l], MLA_NOPE + MLA_V, MLA_NOPE))
    p["mla_wv"] = cast(w["mla_w_kv_up"][l].reshape(MLA_KV_RANK, MLA_HEADS, MLA_NOPE + MLA_V)[:, :, MLA_NOPE:]
                       .reshape(MLA_KV_RANK, MLA_WIDTH))
    p["mla_wvt"] = p["mla_wv"].T
    p["mla_gk"] = _lane_pad(w["mla_kn_g"][l], 0)
    p["sconv_w"] = w["sconv_w"][l]
    p["band_gq"] = row(jnp.tile(w["band_qn_g"][l], BAND_HEADS))
    p["band_gk"] = row(jnp.tile(w["band_kn_g"][l], BAND_HEADS))
    p["band_rel_bias"] = w["band_rel_bias"][l]
    return p


def _band_bias_table(rel_bias, nqry):
    nkey = BAND_ROWS + nqry
    key = np.arange(nkey)[:, None]
    qry = np.arange(nqry)[None, :]
    in_band = np.logical_and(key // CHUNK >= qry // CHUNK, key // CHUNK <= qry // CHUNK + BAND_PAST_CHUNKS)
    diff = np.arange(nqry - 1 + BAND_ROWS, -nkey + BAND_ROWS, -1)
    by_diff = rel_bias[:, np.clip(diff, -REL_CLIP, REL_CLIP) + REL_CLIP]
    skew = jnp.tile(jnp.pad(by_diff, ((0, 0), (0, 1)))[:, None, :], (1, nqry, 1)).reshape(BAND_HEADS, -1)
    skew = skew[:, :nqry * (nkey + nqry - 1)].reshape(BAND_HEADS, nqry, nkey + nqry - 1)
    bias = jnp.swapaxes(skew[:, :, nqry - 1:], 1, 2) * math.log2(math.e)
    return jnp.where(in_band[None], bias, NEG_BIG)


def _rope_tables(pos, rows):
    inv = ROPE_THETA ** (-jnp.arange(0, MLA_ROPE, 2, dtype=F32) / MLA_ROPE)
    ang = pos.astype(F32)[:, None] * inv[None, :]
    cos, sin = jnp.cos(ang), jnp.sin(ang)
    half = MLA_ROPE // 2
    z = lambda w: jnp.zeros((pos.shape[0], w), F32)
    tail = V7X_LANES - MLA_NOPE - MLA_ROPE
    cos_t = jnp.concatenate([jnp.ones((pos.shape[0], MLA_NOPE), F32), cos, cos, z(tail)], axis=1)
    sin_a = jnp.concatenate([z(MLA_NOPE + half), sin, z(tail)], axis=1)
    sin_b = jnp.concatenate([z(MLA_NOPE), -sin, z(half + tail)], axis=1)
    reps = max(rows // pos.shape[0], 1)
    return tuple(jnp.tile(a, (reps, 1)) for a in (cos_t, sin_a, sin_b))


def _layer(x, bsz, t, past, p, tabs, q_off, layer):
    n = bsz * t
    tm = min(n, 512)
    proj = norm_matmul_nt(x, p["norm_mix_g"], p["w_in_t"], math.gcd(n, 2048), 640)
    ya, ssm_new, ssd_conv_new = ssd_branch(proj, bsz, t, past["ssd_conv"], past["ssm"], p)
    yc, sconv_new = sconv_branch(proj, bsz, t, past["sconv"], p["sconv_w"])
    q, ckv, kpe_pad = mla_prep(proj, n, tm, tabs, p)
    q3, ckv3, kpe3 = q.reshape(bsz, t, -1), ckv.reshape(bsz, t, MLA_KV_RANK), kpe_pad.reshape(bsz, t, V7X_LANES)
    if past["mla_ckv"] is not None:
        assert t <= V7X_LANES
        kk, vv = mla_kv_cached(past["mla_ckv"], past["mla_kpe"], layer, ckv3, kpe3, p)
        ob = mla_attention_rows(q3, kk, vv, q_off).reshape(n, MLA_WIDTH)
    else:
        tkc = next(c for c in (256, 128, t) if t % c == 0)
        kk, vt = mla_kv(ckv3, kpe3, tkc, p)
        ob = mla_attention(q3, kk, vt, q_off, min(t, 512), 8)
    band_bias = _band_bias_table(p["band_rel_bias"], 2 * V7X_LANES if t % (2 * V7X_LANES) == 0 else V7X_LANES)
    od, kn = band_branch(proj, past["band_k"], past["band_v"], layer, bsz, t, band_bias, p["band_gq"], p["band_gk"])
    if ob.ndim != od.ndim or t % tm:
        ob = ob if ob.ndim == 2 else jnp.swapaxes(ob, 1, 2).reshape(n, MLA_WIDTH)
        od = jnp.swapaxes(od, 1, 2).reshape(n, BAND_WIDTH)
    x = merge(x, proj, ya, ob, yc, od, p["b_gate"], p["w_a_out"], p["w_b_out"], p["w_c_out"], p["w_d_out"], p["w_o"], tm)
    x = ffn(x, p["norm_ffn_g"], p["w_ffn_up"], p["w_ffn_down"], tm, FF_HIDDEN // 2)
    dv0 = SEG["d_v"][0]
    keep = min(BAND_ROWS, t)
    heads = lambda a: a.reshape(bsz, keep, BAND_HEADS, BAND_HEAD_DIM)
    kn = heads(kn.reshape(bsz, t, BAND_WIDTH)[:, t - keep:])
    vn = heads(proj.reshape(bsz, t, PROJ_COLS)[:, t - keep:, dv0:dv0 + BAND_WIDTH])
    new = {"ssm": ssm_new, "ssd_conv": ssd_conv_new, "sconv": sconv_new,
           "mla_ckv": ckv.reshape(bsz, t, MLA_KV_RANK),
           "mla_kpe": kpe_pad[:, KPE_LANE:KPE_LANE + MLA_ROPE].reshape(bsz, t, MLA_ROPE),
           "band_k": kn, "band_v": vn}
    return x, new


def kernel(x_prompt, x_sample, state_ssm, state_ssd_conv, cache_mla_ckv, cache_mla_kpe, state_sconv, cache_band_k, cache_band_v, norm_mix_g, w_in, b_gate, ssd_conv_w, ssd_conv_b, ssd_dt_bias, ssd_a_log, ssd_d, ssd_norm_g, w_a_out, mla_q_norm_g, mla_w_q_up, mla_kv_norm_g, mla_w_kv_up, mla_qn_g, mla_kn_g, mla_qr_g, mla_kr_g, w_b_out, sconv_w, w_c_out, band_qn_g, band_kn_g, band_rel_bias, w_d_out, w_o, norm_ffn_g, w_ffn_up, w_ffn_down):
    weights = dict(norm_mix_g=norm_mix_g, w_in=w_in, b_gate=b_gate, ssd_conv_w=ssd_conv_w, ssd_conv_b=ssd_conv_b,
                   ssd_dt_bias=ssd_dt_bias, ssd_a_log=ssd_a_log, ssd_d=ssd_d, ssd_norm_g=ssd_norm_g, w_a_out=w_a_out,
                   mla_q_norm_g=mla_q_norm_g, mla_w_q_up=mla_w_q_up, mla_kv_norm_g=mla_kv_norm_g,
                   mla_w_kv_up=mla_w_kv_up, mla_qn_g=mla_qn_g, mla_kn_g=mla_kn_g, mla_qr_g=mla_qr_g,
                   mla_kr_g=mla_kr_g, w_b_out=w_b_out, sconv_w=sconv_w, w_c_out=w_c_out, band_qn_g=band_qn_g,
                   band_kn_g=band_kn_g, band_rel_bias=band_rel_bias, w_d_out=w_d_out, w_o=w_o,
                   norm_ffn_g=norm_ffn_g, w_ffn_up=w_ffn_up, w_ffn_down=w_ffn_down)
    depth = w_in.shape[0]
    b_p, t_p, d = x_prompt.shape
    b_s, t_s, _ = x_sample.shape
    past_len = cache_mla_ckv.shape[2]
    assert d == D_MODEL and t_p % BAND_ROWS == 0 and t_s == CHUNK and cache_band_k.shape[2] == BAND_ROWS
    assert past_len % CHUNK == 0
    tabs_p = _rope_tables(jnp.arange(t_p, dtype=jnp.int32), min(b_p * t_p, 512))
    tabs_s = _rope_tables(past_len + jnp.arange(t_s, dtype=jnp.int32), min(b_s * t_s, 512))
    y_p = x_prompt.reshape(b_p * t_p, d)
    y_s = x_sample.reshape(b_s * t_s, d)
    new_p, new_s = [], []
    band_k_rows = jnp.transpose(cache_band_k, (0, 1, 3, 4, 2)).reshape(depth, b_s, BAND_WIDTH, BAND_ROWS)
    band_v_rows = jnp.transpose(cache_band_v, (0, 1, 3, 4, 2)).reshape(depth, b_s, BAND_WIDTH, BAND_ROWS)
    kpe_cache_t = jnp.swapaxes(cache_mla_kpe, 2, 3)
    for l in range(depth):
        p = _layer_params(l, weights)
        init_p = {"ssm": jnp.zeros((b_p, SSD_HEADS, SSD_HEAD_DIM, SSD_STATE), F32),
                  "ssd_conv": jnp.zeros((b_p, SSD_CONV - 1, SSD_CONV_DIM), F32),
                  "sconv": jnp.zeros((b_p, SCONV_K - 1, SCONV_WIDTH), F32),
                  "mla_ckv": None, "mla_kpe": None, "band_k": None, "band_v": None}
        y_p, st_p = _layer(y_p, b_p, t_p, init_p, p, tabs_p, 0, l)
        new_p.append(st_p)
        past_s = {"ssm": state_ssm[l], "ssd_conv": state_ssd_conv[l], "sconv": state_sconv[l],
                  "mla_ckv": cache_mla_ckv, "mla_kpe": kpe_cache_t,
                  "band_k": band_k_rows, "band_v": band_v_rows}
        y_s, st_s = _layer(y_s, b_s, t_s, past_s, p, tabs_s, past_len, l)
        new_s.append(st_s)

    def stack(states, name):
        return jnp.stack([s[name] for s in states], axis=0)

    out = [y_p.reshape(b_p, t_p, d), y_s.reshape(b_s, t_s, d)]
    for name in ("ssm", "ssd_conv", "mla_ckv", "mla_kpe", "sconv"):
        out += [stack(new_p, name), stack(new_s, name)]
    for name, cache in (("band_k", cache_band_k), ("band_v", cache_band_v)):
        out += [stack(new_p, name), jnp.concatenate([cache[:, :, t_s:], stack(new_s, name)], axis=2)]
    return tuple(out)
```

```python
import functools
import math

import jax
import jax.numpy as jnp
import numpy as np
from jax import lax
from jax.experimental import pallas as pl
from jax.experimental.pallas import tpu as pltpu

F32 = jnp.float32
MXU_DTYPE = jnp.bfloat16
EPS = 1e-6
NEG_BIG = -1e30

V7X_LANES = 128
V7X_VMEM_LIMIT_BYTES = 56 * 1024 * 1024

D_MODEL = 1024
CHUNK = 64
N_BRANCH = 4
SSD_HEADS, SSD_HEAD_DIM, SSD_GROUPS, SSD_STATE, SSD_CONV = 8, 64, 2, 128, 4
SSD_INNER = SSD_HEADS * SSD_HEAD_DIM
SSD_CONV_DIM = SSD_INNER + 2 * SSD_GROUPS * SSD_STATE
MLA_HEADS, MLA_Q_RANK, MLA_KV_RANK, MLA_NOPE, MLA_ROPE, MLA_V = 8, 384, 256, 64, 32, 64
MLA_WIDTH = MLA_HEADS * MLA_V
ROPE_THETA = 10000.0
SCONV_WIDTH, SCONV_K = 512, 3
BAND_HEADS, BAND_HEAD_DIM, BAND_PAST_CHUNKS, REL_CLIP = 8, 64, 8, 128
BAND_WIDTH = BAND_HEADS * BAND_HEAD_DIM
BAND_ROWS = BAND_PAST_CHUNKS * CHUNK
BAND_SPAN = BAND_ROWS + CHUNK
FF_HIDDEN = ((8 * D_MODEL // 3 + 255) // 256) * 256

SEG = {
    "gate": (0, 4096), "c_b": (4096, 512), "c_c": (4608, 512), "c_x": (5120, 512),
    "d_q": (5632, 512), "d_k": (6144, 512), "d_v": (6656, 512),
    "a_z": (7168, 512), "a_x": (7680, 512), "a_b": (8192, 256), "a_c": (8448, 256),
    "b_kvl": (8704, 256), "a_dt": (8960, 128), "b_kpe": (9088, 128), "b_ql": (9216, 384),
}
PROJ_COLS = 9600
KPE_LANE = 64


def _cparams(*sem):
    return pltpu.CompilerParams(dimension_semantics=sem, vmem_limit_bytes=V7X_VMEM_LIMIT_BYTES)


def _mm(a, b):
    return jnp.dot(a.astype(MXU_DTYPE), b.astype(MXU_DTYPE), preferred_element_type=F32)


def _mm_nt(a, b):
    return lax.dot_general(a.astype(MXU_DTYPE), b.astype(MXU_DTYPE), (((1,), (1,)), ((), ())),
                           preferred_element_type=F32)


def _mm_tn(a, b):
    return lax.dot_general(a.astype(MXU_DTYPE), b.astype(MXU_DTYPE), (((0,), (0,)), ((), ())),
                           preferred_element_type=F32)


def _split3(x):
    hi = x.astype(MXU_DTYPE)
    r1 = x - hi.astype(F32)
    mid = r1.astype(MXU_DTYPE)
    lo = (r1 - mid.astype(F32)).astype(MXU_DTYPE)
    return hi, mid, lo


def _mm_exact_rhs01(x, sel):
    hi, mid, lo = _split3(x)
    sel = sel.astype(MXU_DTYPE)
    d = functools.partial(jnp.dot, preferred_element_type=F32)
    return d(hi, sel) + d(mid, sel) + d(lo, sel)


def _silu(x):
    return x * jax.nn.sigmoid(x)


def _row_rms(x):
    return x * lax.rsqrt(jnp.mean(x * x, axis=-1, keepdims=True) + EPS)


def _lane_iota(rows):
    return lax.broadcasted_iota(jnp.int32, (rows, V7X_LANES), 1)


def _head64_rms(x):
    rows, width = x.shape
    low = _lane_iota(rows) < 64
    out = []
    for j in range(width // V7X_LANES):
        blk = x[:, j * V7X_LANES:(j + 1) * V7X_LANES]
        sq = blk * blk
        s_lo = jnp.sum(jnp.where(low, sq, 0.0), axis=-1, keepdims=True) * (1.0 / 64)
        s_hi = jnp.sum(jnp.where(low, 0.0, sq), axis=-1, keepdims=True) * (1.0 / 64)
        out.append(blk * jnp.where(low, lax.rsqrt(s_lo + EPS), lax.rsqrt(s_hi + EPS)))
    return jnp.concatenate(out, axis=-1)


def _norm_matmul_nt_kernel(x_ref, g_ref, wt_ref, o_ref, h_s):
    @pl.when(pl.program_id(1) == 0)
    def _():
        h_s[...] = (_row_rms(x_ref[...]) * g_ref[...]).astype(h_s.dtype)

    o_ref[...] = _mm_nt(h_s[...], wt_ref[...])


def norm_matmul_nt(x, g, wt, tm, tn):
    n, k = x.shape
    c = wt.shape[0]
    return pl.pallas_call(
        _norm_matmul_nt_kernel, out_shape=jax.ShapeDtypeStruct((n, c), F32), grid=(n // tm, c // tn),
        in_specs=[pl.BlockSpec((tm, k), lambda i, j: (i, 0)), pl.BlockSpec((1, k), lambda i, j: (0, 0)),
                  pl.BlockSpec((tn, k), lambda i, j: (j, 0))],
        out_specs=pl.BlockSpec((tm, tn), lambda i, j: (i, j)),
        scratch_shapes=[pltpu.VMEM((tm, k), MXU_DTYPE)],
        compiler_params=_cparams("parallel", "arbitrary"), name="in_proj_matmul")(x, g.reshape(1, k), wt)


def _merge_kernel(x_ref, g_ref, ya_ref, ob_ref, yc_ref, od_ref, bg_ref, wa_ref, wb_ref, wc_ref, wd_ref, wo_ref, o_ref,
                  *, bd_transposed):
    def gate(k):
        return jax.nn.sigmoid(g_ref[:, k * D_MODEL:(k + 1) * D_MODEL] + bg_ref[:, k * D_MODEL:(k + 1) * D_MODEL])

    def attn_out(a_ref, w_ref):
        return _mm_tn(a_ref[0], w_ref[...]) if bd_transposed else _mm(a_ref[...], w_ref[...])

    merged = gate(0) * _mm(ya_ref[...], wa_ref[...])
    merged = merged + gate(1) * attn_out(ob_ref, wb_ref)
    merged = merged + gate(2) * _mm(yc_ref[...], wc_ref[...])
    merged = merged + gate(3) * attn_out(od_ref, wd_ref)
    o_ref[...] = x_ref[...] + _mm(merged, wo_ref[...])


def merge(x, proj, ya, ob, yc, od, b_gate, wa, wb, wc, wd, wo, tm):
    n, d = x.shape
    row = lambda w: pl.BlockSpec((tm, w), lambda i: (i, 0))
    full = lambda a: pl.BlockSpec(a.shape, lambda i: (0, 0))
    bd_transposed = ob.ndim == 3
    if bd_transposed:
        per_seq = ob.shape[2] // tm
        attn = pl.BlockSpec((1, ob.shape[1], tm), lambda i: (i // per_seq, 0, i % per_seq))
    else:
        attn = row(512)
    bg = b_gate.reshape(1, N_BRANCH * d)
    return pl.pallas_call(
        functools.partial(_merge_kernel, bd_transposed=bd_transposed),
        out_shape=jax.ShapeDtypeStruct((n, d), F32), grid=(n // tm,),
        in_specs=[row(d), pl.BlockSpec((tm, N_BRANCH * d), lambda i: (i, 0)), row(512), attn, row(512), attn,
                  full(bg), full(wa), full(wb), full(wc), full(wd), full(wo)],
        out_specs=row(d), compiler_params=_cparams("parallel"), name="merge")(
            x, proj, ya, ob, yc, od, bg, wa, wb, wc, wd, wo)


def _ffn_kernel(x_ref, g_ref, wg_ref, wv_ref, wd_ref, o_ref, h_s, acc_s):
    k = pl.program_id(1)

    @pl.when(k == 0)
    def _():
        h_s[...] = (_row_rms(x_ref[...]) * g_ref[...]).astype(h_s.dtype)
        acc_s[...] = jnp.zeros_like(acc_s)

    h = h_s[...]
    gate = jnp.dot(h, wg_ref[...], preferred_element_type=F32)
    val = jnp.dot(h, wv_ref[...], preferred_element_type=F32)
    acc_s[...] += _mm(_silu(gate) * val, wd_ref[...])

    @pl.when(k == pl.num_programs(1) - 1)
    def _():
        o_ref[...] = x_ref[...] + acc_s[...]


def ffn(x, g, w_up, w_down, tm, tk):
    n, d = x.shape
    nk = FF_HIDDEN // tk
    return pl.pallas_call(
        _ffn_kernel, out_shape=jax.ShapeDtypeStruct((n, d), F32), grid=(n // tm, nk),
        in_specs=[pl.BlockSpec((tm, d), lambda i, k: (i, 0)), pl.BlockSpec((1, d), lambda i, k: (0, 0)),
                  pl.BlockSpec((d, tk), lambda i, k: (0, k)), pl.BlockSpec((d, tk), lambda i, k: (0, nk + k)),
                  pl.BlockSpec((tk, d), lambda i, k: (k, 0))],
        out_specs=pl.BlockSpec((tm, d), lambda i, k: (i, 0)),
        scratch_shapes=[pltpu.VMEM((tm, d), MXU_DTYPE), pltpu.VMEM((tm, d), F32)],
        compiler_params=_cparams("parallel", "arbitrary"), name="ffn")(x, g.reshape(1, d), w_up, w_up, w_down)


def _ssd_kernel(z_ref, x_ref, b_ref, c_ref, dt_ref, conv0_ref, h0_ref, cw_ref, cbias_ref, dtb_ref, alog_ref,
                dfull_ref, ng_ref, tri_ref, expand_ref, y_ref, convn_ref, hn_ref, xp_s, h_s):
    L = CHUNK
    rows = z_ref.shape[0]
    c = pl.program_id(1)

    @pl.when(c == 0)
    def _():
        xp_s[0:8, :] = conv0_ref[0]
        h_s[...] = h0_ref[0]

    xp_s[8:8 + rows, 0:512] = x_ref[...]
    xp_s[8:8 + rows, 512:768] = b_ref[...]
    xp_s[8:8 + rows, 768:1024] = c_ref[...]
    base = 8 - (SSD_CONV - 1)
    acc = xp_s[base:base + rows, :] * cw_ref[0:1, :]
    for i in range(1, SSD_CONV):
        acc = acc + xp_s[base + i:base + i + rows, :] * cw_ref[i:i + 1, :]
    xbc = _silu(acc + cbias_ref[...])
    tail = xp_s[rows:rows + 8, :]
    convn_ref[0] = tail
    xp_s[0:8, :] = tail

    xs = xbc[:, 0:512]
    dtr = dt_ref[...] + dtb_ref[...]
    dt = jnp.maximum(dtr, 0.0) + jnp.log1p(jnp.exp(-jnp.abs(dtr)))
    dt_full = _mm_exact_rhs01(dt, expand_ref[...])
    da_full = dt_full * (-jnp.exp(alog_ref[...]))
    da_parts = _split3(da_full)
    dot = functools.partial(jnp.dot, preferred_element_type=F32)
    tri = tri_ref[...].astype(MXU_DTYPE)
    acs = dot(tri, da_parts[0]) + dot(tri, da_parts[1]) + dot(tri, da_parts[2])
    x_dt = xs * dt_full
    exp_acs = jnp.exp(acs)

    row = lax.broadcasted_iota(jnp.int32, (L, V7X_LANES), 0)
    lane = _lane_iota(L)
    lane_s = jnp.where(lane < 64, lane, lane - 64)
    diag2 = row == lane_s
    tril2 = lane_s <= row
    row2 = lax.broadcasted_iota(jnp.int32, (2 * L, V7X_LANES), 0)
    lane2 = lax.broadcasted_iota(jnp.int32, (2 * L, V7X_LANES), 1)
    blockdiag = (row2 < L) == (lane2 < 64)
    ones_ln = jnp.ones((L, SSD_STATE), MXU_DTYPE)
    dot_tn = functools.partial(lax.dot_general, dimension_numbers=(((0,), (0,)), ((), ())), preferred_element_type=F32)

    h = [h_s[256 * g:256 * (g + 1), :] for g in range(SSD_GROUPS)]
    y_chunks = []
    for k in range(rows // L):
        r = slice(k * L, (k + 1) * L)
        acs_k = acs[r]
        xd_end = x_dt[r] * jnp.exp(acs_k[L - 1:L, :] - acs_k)
        y_parts = []
        for g in range(SSD_GROUPS):
            cols = slice(256 * g, 256 * (g + 1))
            bg = xbc[r, 512 + 128 * g:512 + 128 * (g + 1)]
            cg = xbc[r, 768 + 128 * g:768 + 128 * (g + 1)]
            cb2 = _mm_nt(cg, jnp.concatenate([bg, bg], axis=0))
            y_off = _mm_nt(cg, h[g]) * exp_acs[r, cols]
            for jp in range(2):
                sl = slice(128 * (2 * g + jp), 128 * (2 * g + jp + 1))
                blk = acs_k[:, sl]
                at_s = jnp.sum(jnp.where(diag2, blk, 0.0), axis=0, keepdims=True)
                decay = jnp.where(tril2, jnp.exp(jnp.minimum(blk - at_s, 0.0)), 0.0)
                xpair = x_dt[r, sl]
                xblk = jnp.where(blockdiag, jnp.concatenate([xpair, xpair], axis=0), 0.0)
                y_parts.append(_mm(cb2 * decay, xblk) + y_off[:, 128 * jp:128 * (jp + 1)])
            state = _mm_tn(xd_end[:, cols], bg)
            total = sum(dot_tn(part[r, cols], ones_ln) for part in da_parts)
            h[g] = jnp.exp(total) * h[g] + state
        y_chunks.append(jnp.concatenate(y_parts, axis=-1))
    y = jnp.concatenate(y_chunks, axis=0) + dfull_ref[...] * xs
    gated = y * _silu(z_ref[...])
    y_ref[...] = (_row_rms(gated) * ng_ref[...]).astype(y_ref.dtype)
    for g in range(SSD_GROUPS):
        h_s[256 * g:256 * (g + 1), :] = h[g]
        hn_ref[0, 256 * g:256 * (g + 1), :] = h[g]


def ssd_branch(proj, bsz, t, conv_prev, ssm_prev, p):
    rows = math.gcd(t, 4 * CHUNK)
    nb = t // rows
    n = bsz * t
    col = lambda name: SEG[name][0] // SEG[name][1]
    seg = lambda name: pl.BlockSpec((rows, SEG[name][1]), lambda b, c, _j=col(name): (b * nb + c, _j))
    full = lambda a: pl.BlockSpec(a.shape, lambda b, c: (0,) * a.ndim)
    conv0 = jnp.pad(conv_prev, ((0, 0), (8 - (SSD_CONV - 1), 0), (0, 0)))
    h0 = ssm_prev.reshape(bsz, SSD_INNER, SSD_STATE)
    tri = jnp.kron(jnp.eye(rows // CHUNK, dtype=F32), jnp.tril(jnp.ones((CHUNK, CHUNK), F32)))
    consts = [p["ssd_conv_w"], p["ssd_conv_b"], p["ssd_dt_bias"], p["ssd_a_log"], p["ssd_d"], p["ssd_norm_g"], tri,
              p["ssd_expand"]]
    y, convn, hn = pl.pallas_call(
        _ssd_kernel,
        out_shape=[jax.ShapeDtypeStruct((n, SSD_INNER), MXU_DTYPE), jax.ShapeDtypeStruct((bsz, 8, SSD_CONV_DIM), F32),
                   jax.ShapeDtypeStruct((bsz, SSD_INNER, SSD_STATE), F32)],
        grid=(bsz, nb),
        in_specs=[seg("a_z"), seg("a_x"), seg("a_b"), seg("a_c"), seg("a_dt"),
                  pl.BlockSpec((1, 8, SSD_CONV_DIM), lambda b, c: (b, 0, 0)),
                  pl.BlockSpec((1, SSD_INNER, SSD_STATE), lambda b, c: (b, 0, 0))] + [full(a) for a in consts],
        out_specs=[pl.BlockSpec((rows, SSD_INNER), lambda b, c: (b * nb + c, 0)),
                   pl.BlockSpec((1, 8, SSD_CONV_DIM), lambda b, c: (b, 0, 0)),
                   pl.BlockSpec((1, SSD_INNER, SSD_STATE), lambda b, c: (b, 0, 0))],
        scratch_shapes=[pltpu.VMEM((rows + 8, SSD_CONV_DIM), F32), pltpu.VMEM((SSD_INNER, SSD_STATE), F32)],
        compiler_params=_cparams("parallel", "arbitrary"), name="ssd_branch")(
            proj, proj, proj, proj, proj, conv0, h0, *consts)
    return y, hn.reshape(bsz, SSD_HEADS, SSD_HEAD_DIM, SSD_STATE), convn[:, 8 - (SSD_CONV - 1):]


def _sconv_kernel(cb_ref, cc_ref, cx_ref, st0_ref, w_ref, o_ref, stn_ref, up_s):
    rows = cb_ref.shape[0]

    @pl.when(pl.program_id(1) == 0)
    def _():
        up_s[0:8, :] = st0_ref[0]

    up_s[8:8 + rows, :] = cc_ref[...] * cx_ref[...]
    base = 8 - (SCONV_K - 1)
    uc = up_s[base:base + rows, :] * w_ref[0:1, :]
    for i in range(1, SCONV_K):
        uc = uc + up_s[base + i:base + i + rows, :] * w_ref[i:i + 1, :]
    o_ref[...] = (cb_ref[...] * uc).astype(o_ref.dtype)
    tail = up_s[rows:rows + 8, :]
    stn_ref[0] = tail
    up_s[0:8, :] = tail


def sconv_branch(proj, bsz, t, prev, w):
    rows = min(t, 512)
    nt = t // rows
    col = lambda name: SEG[name][0] // SCONV_WIDTH
    seg = lambda name: pl.BlockSpec((rows, SCONV_WIDTH), lambda b, c, _j=col(name): (b * nt + c, _j))
    st0 = jnp.pad(prev, ((0, 0), (8 - (SCONV_K - 1), 0), (0, 0)))
    o, stn = pl.pallas_call(
        _sconv_kernel,
        out_shape=[jax.ShapeDtypeStruct((bsz * t, SCONV_WIDTH), MXU_DTYPE), jax.ShapeDtypeStruct((bsz, 8, SCONV_WIDTH), F32)],
        grid=(bsz, nt),
        in_specs=[seg("c_b"), seg("c_c"), seg("c_x"), pl.BlockSpec((1, 8, SCONV_WIDTH), lambda b, c: (b, 0, 0)),
                  pl.BlockSpec(w.shape, lambda b, c: (0, 0))],
        out_specs=[pl.BlockSpec((rows, SCONV_WIDTH), lambda b, c: (b * nt + c, 0)),
                   pl.BlockSpec((1, 8, SCONV_WIDTH), lambda b, c: (b, 0, 0))],
        scratch_shapes=[pltpu.VMEM((rows + 8, SCONV_WIDTH), F32)],
        compiler_params=_cparams("parallel", "arbitrary"), name="sconv_branch")(proj, proj, proj, st0, w)
    return o, stn[:, 8 - (SCONV_K - 1):]


def _rope(y, cos_t, sin_a, sin_b):
    return y * cos_t + pltpu.roll(y, 16, 1) * sin_a + pltpu.roll(y, V7X_LANES - 16, 1) * sin_b


def _mla_prep_kernel(ql_ref, kvl_ref, kpe_ref, cos_ref, sa_ref, sb_ref, gq_ref, wq_ref, wqs_ref, gqh_ref, gqhs_ref,
                     gkv_ref, gkr_ref, q_ref, ckv_ref, kpeo_ref):
    rows = ql_ref.shape[0]
    cos_t, sin_a, sin_b = cos_ref[...], sa_ref[...], sb_ref[...]
    qn = (_row_rms(ql_ref[...]) * gq_ref[...]).astype(MXU_DTYPE)
    q = jnp.dot(qn, wq_ref[...], preferred_element_type=F32)
    q_swap = jnp.dot(qn, wqs_ref[...], preferred_element_type=F32)
    lane = _lane_iota(rows)
    nope = lane < MLA_NOPE
    sin_t = sin_a + sin_b
    scale = (MLA_NOPE + MLA_ROPE) ** -0.5 * math.log2(math.e)
    for h in range(MLA_HEADS):
        sl = slice(h * V7X_LANES, (h + 1) * V7X_LANES)
        blk = q[:, sl]
        sq = blk * blk
        s_n = jnp.sum(jnp.where(nope, sq, 0.0), axis=-1, keepdims=True) * (1.0 / MLA_NOPE)
        s_r = jnp.sum(jnp.where(nope, 0.0, sq), axis=-1, keepdims=True) * (1.0 / MLA_ROPE)
        r = jnp.where(nope, lax.rsqrt(s_n + EPS), lax.rsqrt(s_r + EPS))
        y = blk * r * gqh_ref[:, sl]
        y_swap = q_swap[:, sl] * r * gqhs_ref[:, sl]
        q_ref[:, sl] = ((y * cos_t + y_swap * sin_t) * scale).astype(q_ref.dtype)
    ckv_ref[...] = _row_rms(kvl_ref[...]) * gkv_ref[...]
    kp = kpe_ref[...]
    ms = jnp.sum(kp * kp, axis=-1, keepdims=True) * (1.0 / MLA_ROPE)
    kpeo_ref[...] = _rope(kp * lax.rsqrt(ms + EPS) * gkr_ref[...], cos_t, sin_a, sin_b)


def mla_prep(proj, n, tm, tabs, p):
    cos_t, sin_a, sin_b = tabs
    ntab = cos_t.shape[0] // tm
    seg = lambda name: pl.BlockSpec((tm, SEG[name][1]), lambda i, _j=SEG[name][0] // SEG[name][1]: (i, _j))
    tab = pl.BlockSpec((tm, V7X_LANES), lambda i: (i % ntab, 0))
    full = lambda a: pl.BlockSpec(a.shape, lambda i: (0, 0))
    consts = [p["mla_q_norm_g"], p["mla_wq"], p["mla_wq_swap"], p["mla_gq_head"], p["mla_gq_head_swap"],
              p["mla_kv_norm_g"], p["mla_gkr"]]
    return pl.pallas_call(
        _mla_prep_kernel,
        out_shape=[jax.ShapeDtypeStruct((n, MLA_HEADS * V7X_LANES), MXU_DTYPE),
                   jax.ShapeDtypeStruct((n, MLA_KV_RANK), F32), jax.ShapeDtypeStruct((n, V7X_LANES), F32)],
        grid=(n // tm,),
        in_specs=[seg("b_ql"), seg("b_kvl"), seg("b_kpe"), tab, tab, tab] + [full(a) for a in consts],
        out_specs=[pl.BlockSpec((tm, MLA_HEADS * V7X_LANES), lambda i: (i, 0)),
                   pl.BlockSpec((tm, MLA_KV_RANK), lambda i: (i, 0)), pl.BlockSpec((tm, V7X_LANES), lambda i: (i, 0))],
        compiler_params=_cparams("parallel"), name="mla_prep")(proj, proj, proj, cos_t, sin_a, sin_b, *consts)


def _mla_kv_kernel(ckv_ref, kpe_ref, wk_ref, wvt_ref, gk_ref, k_ref, vt_ref):
    c = ckv_ref[0].astype(MXU_DTYPE)
    kk = jnp.dot(c, wk_ref[...], preferred_element_type=F32)
    vt_ref[0, 0] = _mm_nt(wvt_ref[...], c).astype(vt_ref.dtype)
    kpe = kpe_ref[0]
    for h in range(MLA_HEADS):
        sl = slice(h * V7X_LANES, (h + 1) * V7X_LANES)
        blk = kk[:, sl]
        ms = jnp.sum(blk * blk, axis=-1, keepdims=True) * (1.0 / MLA_NOPE)
        k_ref[0, :, sl] = (blk * lax.rsqrt(ms + EPS) * gk_ref[...] + kpe).astype(k_ref.dtype)


def mla_kv(ckv_all, kpe_all, ts, p):
    bsz, s, _ = ckv_all.shape
    full = lambda a: pl.BlockSpec(a.shape, lambda b, i: (0, 0))
    consts = [p["mla_wk"], p["mla_wvt"], p["mla_gk"]]
    return pl.pallas_call(
        _mla_kv_kernel,
        out_shape=[jax.ShapeDtypeStruct((bsz, s, MLA_HEADS * V7X_LANES), MXU_DTYPE),
                   jax.ShapeDtypeStruct((bsz, s // ts, MLA_WIDTH, ts), MXU_DTYPE)],
        grid=(bsz, s // ts),
        in_specs=[pl.BlockSpec((1, ts, MLA_KV_RANK), lambda b, i: (b, i, 0)),
                  pl.BlockSpec((1, ts, V7X_LANES), lambda b, i: (b, i, 0))] + [full(a) for a in consts],
        out_specs=[pl.BlockSpec((1, ts, MLA_HEADS * V7X_LANES), lambda b, i: (b, i, 0)),
                   pl.BlockSpec((1, 1, MLA_WIDTH, ts), lambda b, i: (b, i, 0, 0))],
        compiler_params=_cparams("parallel", "parallel"), name="mla_kv")(ckv_all, kpe_all, *consts)


def _mla_kv_cached_kernel(ckv_old_ref, ckv_new_ref, kpe_old_ref, kpe_new_ref, wk_ref, wv_ref, gk_ref, k_ref, v_ref):
    c = jnp.concatenate([ckv_old_ref[0, 0], ckv_new_ref[0]], axis=0).astype(MXU_DTYPE)
    old_t = kpe_old_ref[0, 0]
    tail = V7X_LANES - KPE_LANE - MLA_ROPE
    old_t = jnp.concatenate([jnp.zeros((KPE_LANE, old_t.shape[1]), F32), old_t, jnp.zeros((tail, old_t.shape[1]), F32)], axis=0)
    kpe = jnp.concatenate([old_t.T, kpe_new_ref[0]], axis=0)
    kk = jnp.dot(c, wk_ref[...], preferred_element_type=F32)
    v_ref[0] = jnp.dot(c, wv_ref[...], preferred_element_type=F32).astype(v_ref.dtype)
    for h in range(MLA_HEADS):
        sl = slice(h * V7X_LANES, (h + 1) * V7X_LANES)
        blk = kk[:, sl]
        ms = jnp.sum(blk * blk, axis=-1, keepdims=True) * (1.0 / MLA_NOPE)
        k_ref[0, :, sl] = (blk * lax.rsqrt(ms + EPS) * gk_ref[...] + kpe).astype(k_ref.dtype)


def mla_kv_cached(ckv_cache, kpe_cache, layer, ckv_new, kpe_new, p):
    _, bsz, past, _ = ckv_cache.shape
    t = ckv_new.shape[1]
    s = past + t
    full = lambda a: pl.BlockSpec(a.shape, lambda b: (0, 0))
    consts = [p["mla_wk"], p["mla_wv"], p["mla_gk"]]
    return pl.pallas_call(
        _mla_kv_cached_kernel,
        out_shape=[jax.ShapeDtypeStruct((bsz, s, MLA_HEADS * V7X_LANES), MXU_DTYPE),
                   jax.ShapeDtypeStruct((bsz, s, MLA_WIDTH), MXU_DTYPE)],
        grid=(bsz,),
        in_specs=[pl.BlockSpec((1, 1, past, MLA_KV_RANK), lambda b: (layer, b, 0, 0)),
                  pl.BlockSpec((1, t, MLA_KV_RANK), lambda b: (b, 0, 0)),
                  pl.BlockSpec((1, 1, MLA_ROPE, past), lambda b: (layer, b, 0, 0)),
                  pl.BlockSpec((1, t, V7X_LANES), lambda b: (b, 0, 0))] + [full(a) for a in consts],
        out_specs=[pl.BlockSpec((1, s, MLA_HEADS * V7X_LANES), lambda b: (b, 0, 0)),
                   pl.BlockSpec((1, s, MLA_WIDTH), lambda b: (b, 0, 0))],
        compiler_params=_cparams("parallel"), name="mla_kv_cached")(ckv_cache, ckv_new, kpe_cache, kpe_new, *consts)


def _mla_attn_rows_kernel(q_ref, k_ref, v_ref, o_ref, *, q_off):
    t, s = q_ref.shape[1], k_ref.shape[1]
    low = _lane_iota(t) < MLA_V
    needs_mask = (s - 1) // CHUNK > q_off // CHUNK
    if needs_mask:
        q_chunk = (q_off + lax.broadcasted_iota(jnp.int32, (t, s), 0)) // CHUNK
        visible = lax.broadcasted_iota(jnp.int32, (t, s), 1) // CHUNK <= q_chunk
    for hp in range(MLA_HEADS // 2):
        v_pair = v_ref[0, :, hp * V7X_LANES:(hp + 1) * V7X_LANES]
        outs = []
        for sub in range(2):
            sl = slice((2 * hp + sub) * V7X_LANES, (2 * hp + sub + 1) * V7X_LANES)
            sc = _mm_nt(q_ref[0, :, sl], k_ref[0, :, sl])
            if needs_mask:
                sc = jnp.where(visible, sc, NEG_BIG)
            e = jnp.exp2(sc - jnp.max(sc, axis=-1, keepdims=True))
            outs.append(_mm(e, v_pair) / jnp.sum(e, axis=-1, keepdims=True))
        o_ref[0, :, hp * V7X_LANES:(hp + 1) * V7X_LANES] = jnp.where(low, outs[0], outs[1]).astype(o_ref.dtype)


def mla_attention_rows(q, k, v, q_off):
    bsz, t, _ = q.shape
    s = k.shape[1]
    return pl.pallas_call(
        functools.partial(_mla_attn_rows_kernel, q_off=q_off),
        out_shape=jax.ShapeDtypeStruct((bsz, t, MLA_WIDTH), MXU_DTYPE), grid=(bsz,),
        in_specs=[pl.BlockSpec((1, t, MLA_HEADS * V7X_LANES), lambda b: (b, 0, 0)),
                  pl.BlockSpec((1, s, MLA_HEADS * V7X_LANES), lambda b: (b, 0, 0)),
                  pl.BlockSpec((1, s, MLA_WIDTH), lambda b: (b, 0, 0))],
        out_specs=pl.BlockSpec((1, t, MLA_WIDTH), lambda b: (b, 0, 0)),
        compiler_params=_cparams("parallel"), name="mla_attention_rows")(q, k, v)


def _mla_attn_kernel(q_ref, k_ref, vt_ref, ot_ref, sa_s, sb_s, *, tq, tkc, q_off, nh):
    assert tq % (2 * tkc) == 0 and q_off % (2 * tkc) == 0 and tkc % CHUNK == 0
    q_start = q_off + pl.program_id(2) * tq
    n_full = q_start // tkc
    n_own = tq // tkc
    lanes = lambda h: slice(h * V7X_LANES, (h + 1) * V7X_LANES)
    qs = [q_ref[0, :, lanes(h)] for h in range(nh)]
    ones = jnp.ones((8, tkc), MXU_DTYPE)

    def produce(c, buf):
        k0 = pl.multiple_of(c * tkc, tkc)
        for h in range(nh):
            buf[h] = _mm_nt(k_ref[0, pl.ds(k0, tkc), lanes(h)], qs[h])

    def consume(c, buf, stats, own=None):
        if own is not None:
            k_chunk = (own * tkc + lax.broadcasted_iota(jnp.int32, (tkc, tq), 0)) // CHUNK
            visible = k_chunk <= lax.broadcasted_iota(jnp.int32, (tkc, tq), 1) // CHUNK
        out = []
        for h in range(nh):
            m, acc = stats[h]
            s = buf[h] if own is None else jnp.where(visible, buf[h], NEG_BIG)
            m_new = jnp.maximum(m, jnp.max(s, axis=0, keepdims=True))
            alpha = jnp.exp2(m - m_new)
            p = jnp.exp2(s - m_new).astype(MXU_DTYPE)
            v_ext = jnp.concatenate([vt_ref[0, c, h * MLA_V:(h + 1) * MLA_V, :], ones], axis=0)
            out.append((m_new, alpha * acc + jnp.dot(v_ext, p, preferred_element_type=F32)))
        return tuple(out)

    def pair(i, stats):
        c = 2 * i
        produce(c + 1, sb_s)
        stats = consume(c, sa_s, stats)
        produce(c + 2, sa_s)
        return consume(c + 1, sb_s, stats)

    init = tuple((jnp.full((1, tq), NEG_BIG, F32), jnp.zeros((MLA_V + 8, tq), F32)) for _ in range(nh))
    produce(0, sa_s)
    stats = lax.fori_loop(0, n_full // 2, pair, init)
    bufs = (sa_s, sb_s)
    for own in range(n_own):
        if own + 1 < n_own:
            produce(n_full + own + 1, bufs[(own + 1) % 2])
        stats = consume(n_full + own, bufs[own % 2], stats, own)
    for h in range(nh):
        acc = stats[h][1]
        ot_ref[0, h * MLA_V:(h + 1) * MLA_V, :] = (acc[:MLA_V] / acc[MLA_V:MLA_V + 1]).astype(ot_ref.dtype)


def mla_attention(q, k, vt, q_off, tq, nh):
    bsz, t, _ = q.shape
    _, nchunk, _, tkc = vt.shape
    s = k.shape[1]
    kern = functools.partial(_mla_attn_kernel, tq=tq, tkc=tkc, q_off=q_off, nh=nh)
    return pl.pallas_call(
        kern, out_shape=jax.ShapeDtypeStruct((bsz, MLA_WIDTH, t), MXU_DTYPE),
        grid=(bsz, MLA_HEADS // nh, t // tq),
        in_specs=[pl.BlockSpec((1, tq, nh * V7X_LANES), lambda b, h, qi: (b, qi, h)),
                  pl.BlockSpec((1, s, nh * V7X_LANES), lambda b, h, qi: (b, 0, h), pipeline_mode=pl.Buffered(1)),
                  pl.BlockSpec((1, nchunk, nh * MLA_V, tkc), lambda b, h, qi: (b, 0, h, 0),
                               pipeline_mode=pl.Buffered(1))],
        out_specs=pl.BlockSpec((1, nh * MLA_V, tq), lambda b, h, qi: (b, h, qi)),
        scratch_shapes=[pltpu.VMEM((nh, tkc, tq), F32), pltpu.VMEM((nh, tkc, tq), F32)],
        compiler_params=_cparams("parallel", "parallel", "parallel"), name="mla_attention")(q, k, vt)


def _band_kernel(q_ref, ka_ref, kb_ref, va_ref, vb_ref, bias_ref, gq_ref, gk_ref, ot_ref, kn_ref, q_s, k_s, vt_s,
                 *, a_is_cache):
    tb = q_ref.shape[0]
    rows_q = q_s.shape[0]
    pad = rows_q - tb
    qn = _head64_rms(q_ref[...]) * gq_ref[...] * (BAND_HEAD_DIM ** -0.5 * math.log2(math.e))
    kb = _head64_rms(kb_ref[...]) * gk_ref[...]
    kn_ref[...] = kb
    if a_is_cache:
        k_s[0:BAND_ROWS, :] = ka_ref[0, 0].T.astype(k_s.dtype)
        vt_s[:, 0:BAND_ROWS] = va_ref[0, 0].astype(vt_s.dtype)
    else:
        k_s[0:BAND_ROWS, :] = (_head64_rms(ka_ref[...]) * gk_ref[...]).astype(k_s.dtype)
        vt_s[:, 0:BAND_ROWS] = va_ref[...].T.astype(vt_s.dtype)
    vb = vb_ref[...]
    if pad:
        zeros = jnp.zeros((pad, BAND_WIDTH), F32)
        qn, kb, vb = (jnp.concatenate([a, zeros], axis=0) for a in (qn, kb, vb))
    q_s[...] = qn.astype(q_s.dtype)
    k_s[BAND_ROWS:, :] = kb.astype(k_s.dtype)
    vt_s[:, BAND_ROWS:] = vb.T.astype(vt_s.dtype)

    _, win, tw = bias_ref.shape
    low = _lane_iota(tw) < BAND_HEAD_DIM
    ones = jnp.ones((8, win), MXU_DTYPE)
    win_row = lax.broadcasted_iota(jnp.int32, (win, 2 * tw), 0)

    def tiles(first):
        for j in range(rows_q // tw):
            r0 = j * tw
            qj = q_s[r0:r0 + tw, :]
            kw = k_s[r0:r0 + win, :]
            ss = []
            for hp in range(BAND_HEADS // 2):
                sl = slice(hp * V7X_LANES, (hp + 1) * V7X_LANES)
                zero = jnp.zeros((), q_s.dtype)
                q2 = jnp.concatenate([jnp.where(low, qj[:, sl], zero), jnp.where(low, zero, qj[:, sl])], axis=0)
                ss.append(_mm_nt(kw[:, sl], q2))
            for hp in range(BAND_HEADS // 2):
                sl = slice(hp * V7X_LANES, (hp + 1) * V7X_LANES)
                v_ext = jnp.concatenate([vt_s[sl, r0:r0 + win], ones], axis=0)
                s = ss[hp] + jnp.concatenate([bias_ref[2 * hp], bias_ref[2 * hp + 1]], axis=1)
                if first:
                    s = jnp.where(win_row + r0 < BAND_ROWS, NEG_BIG, s)
                p = jnp.exp2(s - jnp.max(s, axis=0, keepdims=True)).astype(MXU_DTYPE)
                res = jnp.dot(v_ext, p, preferred_element_type=F32)
                den = res[V7X_LANES:V7X_LANES + 1]
                halves = [res[0:BAND_HEAD_DIM, 0:tw] / den[:, 0:tw], res[BAND_HEAD_DIM:V7X_LANES, tw:] / den[:, tw:]]
                o_pair = jnp.concatenate(halves, axis=0).astype(ot_ref.dtype)
                if pad:
                    ot_ref[0, sl, :] = o_pair[:, 0:tb]
                else:
                    ot_ref[0, sl, r0:r0 + tw] = o_pair

    if a_is_cache:
        tiles(False)
    else:
        pl.when(pl.program_id(1) == 0)(lambda: tiles(True))
        pl.when(pl.program_id(1) > 0)(lambda: tiles(False))


def band_branch(q_src, k_prev, v_prev, layer, bsz, t, bias, gq, gk):
    a_is_cache = k_prev is not None
    tb = min(t, BAND_ROWS)
    nt = t // tb
    tw = bias.shape[2]
    rows_q = -(-tb // tw) * tw
    col = lambda name: SEG[name][0] // BAND_WIDTH
    cur = lambda name: pl.BlockSpec((tb, BAND_WIDTH), lambda b, i, _j=col(name): (b * nt + i, _j))
    if a_is_cache:
        prev = lambda name: pl.BlockSpec((1, 1, BAND_WIDTH, BAND_ROWS), lambda b, i: (layer, b, 0, 0))
        ka, va = k_prev, v_prev
    else:
        prev = lambda name: pl.BlockSpec((BAND_ROWS, BAND_WIDTH), lambda b, i, _j=col(name): (b * nt + jnp.maximum(i - 1, 0), _j))
        ka, va = q_src, q_src
    full = lambda a: pl.BlockSpec(a.shape, lambda b, i: (0,) * a.ndim)
    kern = functools.partial(_band_kernel, a_is_cache=a_is_cache)
    return pl.pallas_call(
        kern,
        out_shape=[jax.ShapeDtypeStruct((bsz, BAND_WIDTH, t), MXU_DTYPE), jax.ShapeDtypeStruct((bsz * t, BAND_WIDTH), F32)],
        grid=(bsz, nt),
        in_specs=[cur("d_q"), prev("d_k"), cur("d_k"), prev("d_v"), cur("d_v"), full(bias), full(gq), full(gk)],
        out_specs=[pl.BlockSpec((1, BAND_WIDTH, tb), lambda b, i: (b, 0, i)),
                   pl.BlockSpec((tb, BAND_WIDTH), lambda b, i: (b * nt + i, 0))],
        scratch_shapes=[pltpu.VMEM((rows_q, BAND_WIDTH), MXU_DTYPE), pltpu.VMEM((BAND_ROWS + rows_q, BAND_WIDTH), MXU_DTYPE),
                        pltpu.VMEM((BAND_WIDTH, BAND_ROWS + rows_q), MXU_DTYPE)],
        compiler_params=_cparams("parallel", "parallel"), name="band_branch")(
            q_src, ka, q_src, va, q_src, bias, gq, gk)


def _head_blocks(w, per_head, used):
    k = w.shape[0]
    w = w.reshape(k, -1, per_head)[:, :, :used]
    return jnp.pad(w, ((0, 0), (0, 0), (0, V7X_LANES - used))).reshape(k, -1)


def _lane_pad(v, offset, width=V7X_LANES):
    return jnp.pad(v, (offset, width - offset - v.shape[0])).reshape(1, width)


def _layer_params(l, w):
    src = {}
    off = 0
    for name, width in (("a_z", 512), ("a_x", 512), ("a_b", 256), ("a_c", 256), ("a_dt", 8), ("b_ql", 384),
                        ("b_kvl", 256), ("b_kpe", 32), ("c_b", 512), ("c_c", 512), ("c_x", 512),
                        ("d_q", 512), ("d_k", 512), ("d_v", 512), ("gate", 4096)):
        src[name] = (off, width)
        off += width
    w_in_t = w["w_in"][l].T.astype(MXU_DTYPE)
    pieces = []
    for name, (dst, dwidth) in sorted(SEG.items(), key=lambda kv: kv[1][0]):
        s0, sw = src[name]
        lead = KPE_LANE if name == "b_kpe" else 0
        for width in (lead, None, dwidth - sw - lead):
            if width is None:
                pieces.append(w_in_t[s0:s0 + sw])
            elif width:
                pieces.append(jnp.zeros((width, D_MODEL), MXU_DTYPE))
    p = {"w_in_t": jnp.concatenate(pieces, axis=0)}
    cast = lambda a: a.astype(MXU_DTYPE)
    row = lambda a: a.reshape(1, -1)
    p["norm_mix_g"], p["norm_ffn_g"], p["b_gate"] = w["norm_mix_g"][l], w["norm_ffn_g"][l], w["b_gate"][l]
    for name in ("w_a_out", "w_b_out", "w_c_out", "w_d_out", "w_o", "w_ffn_up", "w_ffn_down"):
        p[name] = cast(w[name][l])
    p["ssd_conv_w"] = w["ssd_conv_w"][l]
    p["ssd_conv_b"] = row(w["ssd_conv_b"][l])
    p["ssd_dt_bias"] = _lane_pad(w["ssd_dt_bias"][l], 0)
    p["ssd_a_log"] = row(jnp.repeat(w["ssd_a_log"][l], SSD_HEAD_DIM))
    p["ssd_d"] = row(jnp.repeat(w["ssd_d"][l], SSD_HEAD_DIM))
    p["ssd_norm_g"] = row(w["ssd_norm_g"][l])
    head_of_lane = jnp.arange(SSD_INNER) // SSD_HEAD_DIM
    p["ssd_expand"] = (jnp.arange(V7X_LANES)[:, None] == head_of_lane[None, :]).astype(F32)
    p["mla_q_norm_g"] = row(w["mla_q_norm_g"][l])
    p["mla_wq"] = cast(_head_blocks(w["mla_w_q_up"][l], MLA_NOPE + MLA_ROPE, MLA_NOPE + MLA_ROPE))
    gqh = jnp.concatenate([w["mla_qn_g"][l], w["mla_qr_g"][l], jnp.zeros((V7X_LANES - MLA_NOPE - MLA_ROPE,), F32)])
    p["mla_gq_head"] = row(jnp.tile(gqh, MLA_HEADS))
    half = MLA_ROPE // 2
    partner = np.concatenate([np.arange(MLA_NOPE + half, MLA_NOPE + MLA_ROPE), np.arange(MLA_NOPE, MLA_NOPE + half)])
    wq_heads = w["mla_w_q_up"][l].reshape(MLA_Q_RANK, MLA_HEADS, MLA_NOPE + MLA_ROPE)
    wq_swap = jnp.pad(wq_heads[:, :, partner], ((0, 0), (0, 0), (MLA_NOPE, V7X_LANES - MLA_NOPE - MLA_ROPE)))
    p["mla_wq_swap"] = cast(wq_swap.reshape(MLA_Q_RANK, MLA_HEADS * V7X_LANES))
    gqh_swap = jnp.pad(w["mla_qr_g"][l][partner - MLA_NOPE], (MLA_NOPE, V7X_LANES - MLA_NOPE - MLA_ROPE))
    p["mla_gq_head_swap"] = row(jnp.tile(gqh_swap, MLA_HEADS))
    p["mla_kv_norm_g"] = row(w["mla_kv_norm_g"][l])
    p["mla_gkr"] = _lane_pad(w["mla_kr_g"][l], KPE_LANE)
    p["mla_wk"] = cast(_head_blocks(w["mla_w_kv_up"][l], MLA_NOPE + MLA_V, MLA_NOPE))
    p["mla_wv"] = cast(w["mla_w_kv_up"][l].reshape(MLA_KV_RANK, MLA_HEADS, MLA_NOPE + MLA_V)[:, :, MLA_NOPE:]
                       .reshape(MLA_KV_RANK, MLA_WIDTH))
    p["mla_wvt"] = p["mla_wv"].T
    p["mla_gk"] = _lane_pad(w["mla_kn_g"][l], 0)
    p["sconv_w"] = w["sconv_w"][l]
    p["band_gq"] = row(jnp.tile(w["band_qn_g"][l], BAND_HEADS))
    p["band_gk"] = row(jnp.tile(w["band_kn_g"][l], BAND_HEADS))
    p["band_rel_bias"] = w["band_rel_bias"][l]
    return p


def _band_bias_table(rel_bias, nqry):
    nkey = BAND_ROWS + nqry
    key = np.arange(nkey)[:, None]
    qry = np.arange(nqry)[None, :]
    in_band = np.logical_and(key // CHUNK >= qry // CHUNK, key // CHUNK <= qry // CHUNK + BAND_PAST_CHUNKS)
    diff = np.arange(nqry - 1 + BAND_ROWS, -nkey + BAND_ROWS, -1)
    by_diff = rel_bias[:, np.clip(diff, -REL_CLIP, REL_CLIP) + REL_CLIP]
    skew = jnp.tile(jnp.pad(by_diff, ((0, 0), (0, 1)))[:, None, :], (1, nqry, 1)).reshape(BAND_HEADS, -1)
    skew = skew[:, :nqry * (nkey + nqry - 1)].reshape(BAND_HEADS, nqry, nkey + nqry - 1)
    bias = jnp.swapaxes(skew[:, :, nqry - 1:], 1, 2) * math.log2(math.e)
    return jnp.where(in_band[None], bias, NEG_BIG)


def _rope_tables(pos, rows):
    inv = ROPE_THETA ** (-jnp.arange(0, MLA_ROPE, 2, dtype=F32) / MLA_ROPE)
    ang = pos.astype(F32)[:, None] * inv[None, :]
    cos, sin = jnp.cos(ang), jnp.sin(ang)
    half = MLA_ROPE // 2
    z = lambda w: jnp.zeros((pos.shape[0], w), F32)
    tail = V7X_LANES - MLA_NOPE - MLA_ROPE
    cos_t = jnp.concatenate([jnp.ones((pos.shape[0], MLA_NOPE), F32), cos, cos, z(tail)], axis=1)
    sin_a = jnp.concatenate([z(MLA_NOPE + half), sin, z(tail)], axis=1)
    sin_b = jnp.concatenate([z(MLA_NOPE), -sin, z(half + tail)], axis=1)
    reps = max(rows // pos.shape[0], 1)
    return tuple(jnp.tile(a, (reps, 1)) for a in (cos_t, sin_a, sin_b))


def _layer(x, bsz, t, past, p, tabs, q_off, layer):
    n = bsz * t
    tm = min(n, 512)
    proj = norm_matmul_nt(x, p["norm_mix_g"], p["w_in_t"], min(n, 1024), 1920)
    ya, ssm_new, ssd_conv_new = ssd_branch(proj, bsz, t, past["ssd_conv"], past["ssm"], p)
    yc, sconv_new = sconv_branch(proj, bsz, t, past["sconv"], p["sconv_w"])
    q, ckv, kpe_pad = mla_prep(proj, n, tm, tabs, p)
    q3, ckv3, kpe3 = q.reshape(bsz, t, -1), ckv.reshape(bsz, t, MLA_KV_RANK), kpe_pad.reshape(bsz, t, V7X_LANES)
    if past["mla_ckv"] is not None:
        assert t <= V7X_LANES
        kk, vv = mla_kv_cached(past["mla_ckv"], past["mla_kpe"], layer, ckv3, kpe3, p)
        ob = mla_attention_rows(q3, kk, vv, q_off).reshape(n, MLA_WIDTH)
    else:
        tkc = next(c for c in (256, 128, t) if t % c == 0)
        kk, vt = mla_kv(ckv3, kpe3, tkc, p)
        ob = mla_attention(q3, kk, vt, q_off, min(t, 512), 8)
    band_bias = _band_bias_table(p["band_rel_bias"], 2 * V7X_LANES if t % (2 * V7X_LANES) == 0 else V7X_LANES)
    od, kn = band_branch(proj, past["band_k"], past["band_v"], layer, bsz, t, band_bias, p["band_gq"], p["band_gk"])
    if ob.ndim != od.ndim or t % tm:
        ob = ob if ob.ndim == 2 else jnp.swapaxes(ob, 1, 2).reshape(n, MLA_WIDTH)
        od = jnp.swapaxes(od, 1, 2).reshape(n, BAND_WIDTH)
    x = merge(x, proj, ya, ob, yc, od, p["b_gate"], p["w_a_out"], p["w_b_out"], p["w_c_out"], p["w_d_out"], p["w_o"], tm)
    x = ffn(x, p["norm_ffn_g"], p["w_ffn_up"], p["w_ffn_down"], tm, FF_HIDDEN // 2)
    dv0 = SEG["d_v"][0]
    keep = min(BAND_ROWS, t)
    heads = lambda a: a.reshape(bsz, keep, BAND_HEADS, BAND_HEAD_DIM)
    kn = heads(kn.reshape(bsz, t, BAND_WIDTH)[:, t - keep:])
    vn = heads(proj.reshape(bsz, t, PROJ_COLS)[:, t - keep:, dv0:dv0 + BAND_WIDTH])
    new = {"ssm": ssm_new, "ssd_conv": ssd_conv_new, "sconv": sconv_new,
           "mla_ckv": ckv.reshape(bsz, t, MLA_KV_RANK),
           "mla_kpe": kpe_pad[:, KPE_LANE:KPE_LANE + MLA_ROPE].reshape(bsz, t, MLA_ROPE),
           "band_k": kn, "band_v": vn}
    return x, new


def kernel(x_prompt, x_sample, state_ssm, state_ssd_conv, cache_mla_ckv, cache_mla_kpe, state_sconv, cache_band_k, cache_band_v, norm_mix_g, w_in, b_gate, ssd_conv_w, ssd_conv_b, ssd_dt_bias, ssd_a_log, ssd_d, ssd_norm_g, w_a_out, mla_q_norm_g, mla_w_q_up, mla_kv_norm_g, mla_w_kv_up, mla_qn_g, mla_kn_g, mla_qr_g, mla_kr_g, w_b_out, sconv_w, w_c_out, band_qn_g, band_kn_g, band_rel_bias, w_d_out, w_o, norm_ffn_g, w_ffn_up, w_ffn_down):
    weights = dict(norm_mix_g=norm_mix_g, w_in=w_in, b_gate=b_gate, ssd_conv_w=ssd_conv_w, ssd_conv_b=ssd_conv_b,
                   ssd_dt_bias=ssd_dt_bias, ssd_a_log=ssd_a_log, ssd_d=ssd_d, ssd_norm_g=ssd_norm_g, w_a_out=w_a_out,
                   mla_q_norm_g=mla_q_norm_g, mla_w_q_up=mla_w_q_up, mla_kv_norm_g=mla_kv_norm_g,
                   mla_w_kv_up=mla_w_kv_up, mla_qn_g=mla_qn_g, mla_kn_g=mla_kn_g, mla_qr_g=mla_qr_g,
                   mla_kr_g=mla_kr_g, w_b_out=w_b_out, sconv_w=sconv_w, w_c_out=w_c_out, band_qn_g=band_qn_g,
                   band_kn_g=band_kn_g, band_rel_bias=band_rel_bias, w_d_out=w_d_out, w_o=w_o,
                   norm_ffn_g=norm_ffn_g, w_ffn_up=w_ffn_up, w_ffn_down=w_ffn_down)
    depth = w_in.shape[0]
    b_p, t_p, d = x_prompt.shape
    b_s, t_s, _ = x_sample.shape
    past_len = cache_mla_ckv.shape[2]
    assert d == D_MODEL and t_p % BAND_ROWS == 0 and t_s == CHUNK and cache_band_k.shape[2] == BAND_ROWS
    assert past_len % CHUNK == 0
    tabs_p = _rope_tables(jnp.arange(t_p, dtype=jnp.int32), min(b_p * t_p, 512))
    tabs_s = _rope_tables(past_len + jnp.arange(t_s, dtype=jnp.int32), min(b_s * t_s, 512))
    y_p = x_prompt.reshape(b_p * t_p, d)
    y_s = x_sample.reshape(b_s * t_s, d)
    new_p, new_s = [], []
    band_k_rows = jnp.transpose(cache_band_k, (0, 1, 3, 4, 2)).reshape(depth, b_s, BAND_WIDTH, BAND_ROWS)
    band_v_rows = jnp.transpose(cache_band_v, (0, 1, 3, 4, 2)).reshape(depth, b_s, BAND_WIDTH, BAND_ROWS)
    kpe_cache_t = jnp.swapaxes(cache_mla_kpe, 2, 3)
    for l in range(depth):
        p = _layer_params(l, weights)
        init_p = {"ssm": jnp.zeros((b_p, SSD_HEADS, SSD_HEAD_DIM, SSD_STATE), F32),
                  "ssd_conv": jnp.zeros((b_p, SSD_CONV - 1, SSD_CONV_DIM), F32),
                  "sconv": jnp.zeros((b_p, SCONV_K - 1, SCONV_WIDTH), F32),
                  "mla_ckv": None, "mla_kpe": None, "band_k": None, "band_v": None}
        y_p, st_p = _layer(y_p, b_p, t_p, init_p, p, tabs_p, 0, l)
        new_p.append(st_p)
        past_s = {"ssm": state_ssm[l], "ssd_conv": state_ssd_conv[l], "sconv": state_sconv[l],
                  "mla_ckv": cache_mla_ckv, "mla_kpe": kpe_cache_t,
                  "band_k": band_k_rows, "band_v": band_v_rows}
        y_s, st_s = _layer(y_s, b_s, t_s, past_s, p, tabs_s, past_len, l)
        new_s.append(st_s)

    def stack(states, name):
        return jnp.stack([s[name] for s in states], axis=0)

    out = [y_p.reshape(b_p, t_p, d), y_s.reshape(b_s, t_s, d)]
    for name in ("ssm", "ssd_conv", "mla_ckv", "mla_kpe", "sconv"):
        out += [stack(new_p, name), stack(new_s, name)]
    for name, cache in (("band_k", cache_band_k), ("band_v", cache_band_v)):
        out += [stack(new_p, name), jnp.concatenate([cache[:, :, t_s:], stack(new_s, name)], axis=2)]
    return tuple(out)
```

```python
import functools
import math

import jax
import jax.numpy as jnp
import numpy as np
from jax import lax
from jax.experimental import pallas as pl
from jax.experimental.pallas import tpu as pltpu

F32 = jnp.float32
MXU_DTYPE = jnp.bfloat16
EPS = 1e-6
NEG_BIG = -1e30

V7X_LANES = 128
V7X_VMEM_LIMIT_BYTES = 56 * 1024 * 1024

D_MODEL = 1024
CHUNK = 64
N_BRANCH = 4
SSD_HEADS, SSD_HEAD_DIM, SSD_GROUPS, SSD_STATE, SSD_CONV = 8, 64, 2, 128, 4
SSD_INNER = SSD_HEADS * SSD_HEAD_DIM
SSD_CONV_DIM = SSD_INNER + 2 * SSD_GROUPS * SSD_STATE
MLA_HEADS, MLA_Q_RANK, MLA_KV_RANK, MLA_NOPE, MLA_ROPE, MLA_V = 8, 384, 256, 64, 32, 64
MLA_WIDTH = MLA_HEADS * MLA_V
ROPE_THETA = 10000.0
SCONV_WIDTH, SCONV_K = 512, 3
BAND_HEADS, BAND_HEAD_DIM, BAND_PAST_CHUNKS, REL_CLIP = 8, 64, 8, 128
BAND_WIDTH = BAND_HEADS * BAND_HEAD_DIM
BAND_ROWS = BAND_PAST_CHUNKS * CHUNK
BAND_SPAN = BAND_ROWS + CHUNK
FF_HIDDEN = ((8 * D_MODEL // 3 + 255) // 256) * 256

SEG = {
    "gate": (0, 4096), "c_b": (4096, 512), "c_c": (4608, 512), "c_x": (5120, 512),
    "d_q": (5632, 512), "d_k": (6144, 512), "d_v": (6656, 512),
    "a_z": (7168, 512), "a_x": (7680, 512), "a_b": (8192, 256), "a_c": (8448, 256),
    "b_kvl": (8704, 256), "a_dt": (8960, 128), "b_kpe": (9088, 128), "b_ql": (9216, 384),
}
PROJ_COLS = 9600
KPE_LANE = 64


def _cparams(*sem):
    return pltpu.CompilerParams(dimension_semantics=sem, vmem_limit_bytes=V7X_VMEM_LIMIT_BYTES)


def _mm(a, b):
    return jnp.dot(a.astype(MXU_DTYPE), b.astype(MXU_DTYPE), preferred_element_type=F32)


def _mm_nt(a, b):
    return lax.dot_general(a.astype(MXU_DTYPE), b.astype(MXU_DTYPE), (((1,), (1,)), ((), ())),
                           preferred_element_type=F32)


def _mm_tn(a, b):
    return lax.dot_general(a.astype(MXU_DTYPE), b.astype(MXU_DTYPE), (((0,), (0,)), ((), ())),
                           preferred_element_type=F32)


def _split3(x):
    hi = x.astype(MXU_DTYPE)
    r1 = x - hi.astype(F32)
    mid = r1.astype(MXU_DTYPE)
    lo = (r1 - mid.astype(F32)).astype(MXU_DTYPE)
    return hi, mid, lo


def _mm_exact_rhs01(x, sel):
    hi, mid, lo = _split3(x)
    sel = sel.astype(MXU_DTYPE)
    d = functools.partial(jnp.dot, preferred_element_type=F32)
    return d(hi, sel) + d(mid, sel) + d(lo, sel)


def _silu(x):
    return x * jax.nn.sigmoid(x)


def _row_rms(x):
    return x * lax.rsqrt(jnp.mean(x * x, axis=-1, keepdims=True) + EPS)


def _lane_iota(rows):
    return lax.broadcasted_iota(jnp.int32, (rows, V7X_LANES), 1)


def _head64_rms(x):
    rows, width = x.shape
    low = _lane_iota(rows) < 64
    out = []
    for j in range(width // V7X_LANES):
        blk = x[:, j * V7X_LANES:(j + 1) * V7X_LANES]
        sq = blk * blk
        s_lo = jnp.sum(jnp.where(low, sq, 0.0), axis=-1, keepdims=True) * (1.0 / 64)
        s_hi = jnp.sum(jnp.where(low, 0.0, sq), axis=-1, keepdims=True) * (1.0 / 64)
        out.append(blk * jnp.where(low, lax.rsqrt(s_lo + EPS), lax.rsqrt(s_hi + EPS)))
    return jnp.concatenate(out, axis=-1)


def _norm_matmul_nt_kernel(x_ref, g_ref, wt_ref, o_ref, h_s):
    j = pl.program_id(1)
    tn = o_ref.shape[1]

    @pl.when(j == 0)
    def _():
        h_s[...] = (_row_rms(x_ref[...]) * g_ref[...]).astype(h_s.dtype)

    o_ref[...] = _mm_nt(h_s[...], wt_ref[pl.ds(pl.multiple_of(j * tn, tn), tn), :])


def norm_matmul_nt(x, g, wt, tm, tn):
    n, k = x.shape
    c = wt.shape[0]
    return pl.pallas_call(
        _norm_matmul_nt_kernel, out_shape=jax.ShapeDtypeStruct((n, c), F32), grid=(n // tm, c // tn),
        in_specs=[pl.BlockSpec((tm, k), lambda i, j: (i, 0)), pl.BlockSpec((1, k), lambda i, j: (0, 0)),
                  pl.BlockSpec((c, k), lambda i, j: (0, 0), pipeline_mode=pl.Buffered(1))],
        out_specs=pl.BlockSpec((tm, tn), lambda i, j: (i, j)),
        scratch_shapes=[pltpu.VMEM((tm, k), MXU_DTYPE)],
        compiler_params=_cparams("parallel", "arbitrary"), name="in_proj_matmul")(x, g.reshape(1, k), wt)


def _merge_kernel(x_ref, g_ref, ya_ref, ob_ref, yc_ref, od_ref, bg_ref, wa_ref, wb_ref, wc_ref, wd_ref, wo_ref, o_ref,
                  *, bd_transposed):
    def gate(k):
        return jax.nn.sigmoid(g_ref[:, k * D_MODEL:(k + 1) * D_MODEL] + bg_ref[:, k * D_MODEL:(k + 1) * D_MODEL])

    def attn_out(a_ref, w_ref):
        return _mm_tn(a_ref[0], w_ref[...]) if bd_transposed else _mm(a_ref[...], w_ref[...])

    merged = gate(0) * _mm(ya_ref[...], wa_ref[...])
    merged = merged + gate(1) * attn_out(ob_ref, wb_ref)
    merged = merged + gate(2) * _mm(yc_ref[...], wc_ref[...])
    merged = merged + gate(3) * attn_out(od_ref, wd_ref)
    o_ref[...] = x_ref[...] + _mm(merged, wo_ref[...])


def merge(x, proj, ya, ob, yc, od, b_gate, wa, wb, wc, wd, wo, tm):
    n, d = x.shape
    row = lambda w: pl.BlockSpec((tm, w), lambda i: (i, 0))
    full = lambda a: pl.BlockSpec(a.shape, lambda i: (0, 0))
    bd_transposed = ob.ndim == 3
    if bd_transposed:
        per_seq = ob.shape[2] // tm
        attn = pl.BlockSpec((1, ob.shape[1], tm), lambda i: (i // per_seq, 0, i % per_seq))
    else:
        attn = row(512)
    bg = b_gate.reshape(1, N_BRANCH * d)
    return pl.pallas_call(
        functools.partial(_merge_kernel, bd_transposed=bd_transposed),
        out_shape=jax.ShapeDtypeStruct((n, d), F32), grid=(n // tm,),
        in_specs=[row(d), pl.BlockSpec((tm, N_BRANCH * d), lambda i: (i, 0)), row(512), attn, row(512), attn,
                  full(bg), full(wa), full(wb), full(wc), full(wd), full(wo)],
        out_specs=row(d), compiler_params=_cparams("parallel"), name="merge")(
            x, proj, ya, ob, yc, od, bg, wa, wb, wc, wd, wo)


def _ffn_kernel(x_ref, g_ref, wup_ref, wdn_ref, o_ref):
    nk, _, tk = wup_ref.shape
    nk //= 2
    x = x_ref[...]
    h = (_row_rms(x) * g_ref[...]).astype(MXU_DTYPE)
    out = x
    for k in range(nk):
        gate = jnp.dot(h, wup_ref[k], preferred_element_type=F32)
        val = jnp.dot(h, wup_ref[nk + k], preferred_element_type=F32)
        out = out + _mm(_silu(gate) * val, wdn_ref[k * tk:(k + 1) * tk, :])
    o_ref[...] = out


def ffn(x, g, w_up_tiles, w_down, tm):
    n, d = x.shape
    resident = lambda a: pl.BlockSpec(a.shape, lambda i: (0,) * a.ndim, pipeline_mode=pl.Buffered(1))
    return pl.pallas_call(
        _ffn_kernel, out_shape=jax.ShapeDtypeStruct((n, d), F32), grid=(n // tm,),
        in_specs=[pl.BlockSpec((tm, d), lambda i: (i, 0)), pl.BlockSpec((1, d), lambda i: (0, 0)),
                  resident(w_up_tiles), resident(w_down)],
        out_specs=pl.BlockSpec((tm, d), lambda i: (i, 0)),
        compiler_params=_cparams("parallel"), name="ffn")(x, g.reshape(1, d), w_up_tiles, w_down)


def _ssd_kernel(z_ref, x_ref, b_ref, c_ref, dt_ref, conv0_ref, h0_ref, cw_ref, cbias_ref, dtb_ref, alog_ref,
                dfull_ref, ng_ref, tri_ref, expand_ref, y_ref, convn_ref, hn_ref, xp_s, h_s):
    L = CHUNK
    rows = z_ref.shape[0]
    c = pl.program_id(1)

    @pl.when(c == 0)
    def _():
        xp_s[0:8, :] = conv0_ref[0]
        h_s[...] = h0_ref[0]

    xp_s[8:8 + rows, 0:512] = x_ref[...]
    xp_s[8:8 + rows, 512:768] = b_ref[...]
    xp_s[8:8 + rows, 768:1024] = c_ref[...]
    base = 8 - (SSD_CONV - 1)
    acc = xp_s[base:base + rows, :] * cw_ref[0:1, :]
    for i in range(1, SSD_CONV):
        acc = acc + xp_s[base + i:base + i + rows, :] * cw_ref[i:i + 1, :]
    xbc = _silu(acc + cbias_ref[...])
    tail = xp_s[rows:rows + 8, :]
    convn_ref[0] = tail
    xp_s[0:8, :] = tail

    xs = xbc[:, 0:512]
    dtr = dt_ref[...] + dtb_ref[...]
    dt = jnp.maximum(dtr, 0.0) + jnp.log1p(jnp.exp(-jnp.abs(dtr)))
    dt_full = _mm_exact_rhs01(dt, expand_ref[...])
    da_full = dt_full * (-jnp.exp(alog_ref[...]))
    da_parts = _split3(da_full)
    dot = functools.partial(jnp.dot, preferred_element_type=F32)
    tri = tri_ref[...].astype(MXU_DTYPE)
    acs = dot(tri, da_parts[0]) + dot(tri, da_parts[1]) + dot(tri, da_parts[2])
    x_dt = xs * dt_full
    exp_acs = jnp.exp(acs)

    row = lax.broadcasted_iota(jnp.int32, (L, V7X_LANES), 0)
    lane = _lane_iota(L)
    lane_s = jnp.where(lane < 64, lane, lane - 64)
    diag2 = row == lane_s
    tril2 = lane_s <= row
    row2 = lax.broadcasted_iota(jnp.int32, (2 * L, V7X_LANES), 0)
    lane2 = lax.broadcasted_iota(jnp.int32, (2 * L, V7X_LANES), 1)
    blockdiag = (row2 < L) == (lane2 < 64)
    ones_ln = jnp.ones((L, SSD_STATE), MXU_DTYPE)
    dot_tn = functools.partial(lax.dot_general, dimension_numbers=(((0,), (0,)), ((), ())), preferred_element_type=F32)

    h = [h_s[256 * g:256 * (g + 1), :] for g in range(SSD_GROUPS)]
    y_chunks = []
    for k in range(rows // L):
        r = slice(k * L, (k + 1) * L)
        acs_k = acs[r]
        xd_end = x_dt[r] * jnp.exp(acs_k[L - 1:L, :] - acs_k)
        y_parts = []
        for g in range(SSD_GROUPS):
            cols = slice(256 * g, 256 * (g + 1))
            bg = xbc[r, 512 + 128 * g:512 + 128 * (g + 1)]
            cg = xbc[r, 768 + 128 * g:768 + 128 * (g + 1)]
            cb2 = _mm_nt(cg, jnp.concatenate([bg, bg], axis=0))
            y_off = _mm_nt(cg, h[g]) * exp_acs[r, cols]
            for jp in range(2):
                sl = slice(128 * (2 * g + jp), 128 * (2 * g + jp + 1))
                blk = acs_k[:, sl]
                at_s = jnp.sum(jnp.where(diag2, blk, 0.0), axis=0, keepdims=True)
                decay = jnp.where(tril2, jnp.exp(jnp.minimum(blk - at_s, 0.0)), 0.0)
                xpair = x_dt[r, sl]
                xblk = jnp.where(blockdiag, jnp.concatenate([xpair, xpair], axis=0), 0.0)
                y_parts.append(_mm(cb2 * decay, xblk) + y_off[:, 128 * jp:128 * (jp + 1)])
            state = _mm_tn(xd_end[:, cols], bg)
            total = sum(dot_tn(part[r, cols], ones_ln) for part in da_parts)
            h[g] = jnp.exp(total) * h[g] + state
        y_chunks.append(jnp.concatenate(y_parts, axis=-1))
    y = jnp.concatenate(y_chunks, axis=0) + dfull_ref[...] * xs
    gated = y * _silu(z_ref[...])
    y_ref[...] = (_row_rms(gated) * ng_ref[...]).astype(y_ref.dtype)
    for g in range(SSD_GROUPS):
        h_s[256 * g:256 * (g + 1), :] = h[g]
        hn_ref[0, 256 * g:256 * (g + 1), :] = h[g]


def ssd_branch(proj, bsz, t, conv_prev, ssm_prev, p):
    rows = math.gcd(t, 4 * CHUNK)
    nb = t // rows
    n = bsz * t
    col = lambda name: SEG[name][0] // SEG[name][1]
    seg = lambda name: pl.BlockSpec((rows, SEG[name][1]), lambda b, c, _j=col(name): (b * nb + c, _j))
    full = lambda a: pl.BlockSpec(a.shape, lambda b, c: (0,) * a.ndim)
    conv0 = jnp.pad(conv_prev, ((0, 0), (8 - (SSD_CONV - 1), 0), (0, 0)))
    h0 = ssm_prev.reshape(bsz, SSD_INNER, SSD_STATE)
    tri = jnp.kron(jnp.eye(rows // CHUNK, dtype=F32), jnp.tril(jnp.ones((CHUNK, CHUNK), F32)))
    consts = [p["ssd_conv_w"], p["ssd_conv_b"], p["ssd_dt_bias"], p["ssd_a_log"], p["ssd_d"], p["ssd_norm_g"], tri,
              p["ssd_expand"]]
    y, convn, hn = pl.pallas_call(
        _ssd_kernel,
        out_shape=[jax.ShapeDtypeStruct((n, SSD_INNER), MXU_DTYPE), jax.ShapeDtypeStruct((bsz, 8, SSD_CONV_DIM), F32),
                   jax.ShapeDtypeStruct((bsz, SSD_INNER, SSD_STATE), F32)],
        grid=(bsz, nb),
        in_specs=[seg("a_z"), seg("a_x"), seg("a_b"), seg("a_c"), seg("a_dt"),
                  pl.BlockSpec((1, 8, SSD_CONV_DIM), lambda b, c: (b, 0, 0)),
                  pl.BlockSpec((1, SSD_INNER, SSD_STATE), lambda b, c: (b, 0, 0))] + [full(a) for a in consts],
        out_specs=[pl.BlockSpec((rows, SSD_INNER), lambda b, c: (b * nb + c, 0)),
                   pl.BlockSpec((1, 8, SSD_CONV_DIM), lambda b, c: (b, 0, 0)),
                   pl.BlockSpec((1, SSD_INNER, SSD_STATE), lambda b, c: (b, 0, 0))],
        scratch_shapes=[pltpu.VMEM((rows + 8, SSD_CONV_DIM), F32), pltpu.VMEM((SSD_INNER, SSD_STATE), F32)],
        compiler_params=_cparams("parallel", "arbitrary"), name="ssd_branch")(
            proj, proj, proj, proj, proj, conv0, h0, *consts)
    return y, hn.reshape(bsz, SSD_HEADS, SSD_HEAD_DIM, SSD_STATE), convn[:, 8 - (SSD_CONV - 1):]


def _sconv_kernel(cb_ref, cc_ref, cx_ref, st0_ref, w_ref, o_ref, stn_ref, up_s):
    rows = cb_ref.shape[0]

    @pl.when(pl.program_id(1) == 0)
    def _():
        up_s[0:8, :] = st0_ref[0]

    up_s[8:8 + rows, :] = cc_ref[...] * cx_ref[...]
    base = 8 - (SCONV_K - 1)
    uc = up_s[base:base + rows, :] * w_ref[0:1, :]
    for i in range(1, SCONV_K):
        uc = uc + up_s[base + i:base + i + rows, :] * w_ref[i:i + 1, :]
    o_ref[...] = (cb_ref[...] * uc).astype(o_ref.dtype)
    tail = up_s[rows:rows + 8, :]
    stn_ref[0] = tail
    up_s[0:8, :] = tail


def sconv_branch(proj, bsz, t, prev, w):
    rows = min(t, 512)
    nt = t // rows
    col = lambda name: SEG[name][0] // SCONV_WIDTH
    seg = lambda name: pl.BlockSpec((rows, SCONV_WIDTH), lambda b, c, _j=col(name): (b * nt + c, _j))
    st0 = jnp.pad(prev, ((0, 0), (8 - (SCONV_K - 1), 0), (0, 0)))
    o, stn = pl.pallas_call(
        _sconv_kernel,
        out_shape=[jax.ShapeDtypeStruct((bsz * t, SCONV_WIDTH), MXU_DTYPE), jax.ShapeDtypeStruct((bsz, 8, SCONV_WIDTH), F32)],
        grid=(bsz, nt),
        in_specs=[seg("c_b"), seg("c_c"), seg("c_x"), pl.BlockSpec((1, 8, SCONV_WIDTH), lambda b, c: (b, 0, 0)),
                  pl.BlockSpec(w.shape, lambda b, c: (0, 0))],
        out_specs=[pl.BlockSpec((rows, SCONV_WIDTH), lambda b, c: (b * nt + c, 0)),
                   pl.BlockSpec((1, 8, SCONV_WIDTH), lambda b, c: (b, 0, 0))],
        scratch_shapes=[pltpu.VMEM((rows + 8, SCONV_WIDTH), F32)],
        compiler_params=_cparams("parallel", "arbitrary"), name="sconv_branch")(proj, proj, proj, st0, w)
    return o, stn[:, 8 - (SCONV_K - 1):]


def _rope(y, cos_t, sin_a, sin_b):
    return y * cos_t + pltpu.roll(y, 16, 1) * sin_a + pltpu.roll(y, V7X_LANES - 16, 1) * sin_b


def _mla_prep_kernel(ql_ref, kvl_ref, kpe_ref, cos_ref, sa_ref, sb_ref, gq_ref, wq_ref, wqs_ref, gqh_ref, gqhs_ref,
                     gkv_ref, gkr_ref, q_ref, ckv_ref, kpeo_ref):
    rows = ql_ref.shape[0]
    cos_t, sin_a, sin_b = cos_ref[...], sa_ref[...], sb_ref[...]
    qn = (_row_rms(ql_ref[...]) * gq_ref[...]).astype(MXU_DTYPE)
    q = jnp.dot(qn, wq_ref[...], preferred_element_type=F32)
    q_swap = jnp.dot(qn, wqs_ref[...], preferred_element_type=F32)
    lane = _lane_iota(rows)
    nope = lane < MLA_NOPE
    sin_t = sin_a + sin_b
    scale = (MLA_NOPE + MLA_ROPE) ** -0.5 * math.log2(math.e)
    for h in range(MLA_HEADS):
        sl = slice(h * V7X_LANES, (h + 1) * V7X_LANES)
        blk = q[:, sl]
        sq = blk * blk
        s_n = jnp.sum(jnp.where(nope, sq, 0.0), axis=-1, keepdims=True) * (1.0 / MLA_NOPE)
        s_r = jnp.sum(jnp.where(nope, 0.0, sq), axis=-1, keepdims=True) * (1.0 / MLA_ROPE)
        r = jnp.where(nope, lax.rsqrt(s_n + EPS), lax.rsqrt(s_r + EPS))
        y = blk * r * gqh_ref[:, sl]
        y_swap = q_swap[:, sl] * r * gqhs_ref[:, sl]
        q_ref[:, sl] = ((y * cos_t + y_swap * sin_t) * scale).astype(q_ref.dtype)
    ckv_ref[...] = _row_rms(kvl_ref[...]) * gkv_ref[...]
    kp = kpe_ref[...]
    ms = jnp.sum(kp * kp, axis=-1, keepdims=True) * (1.0 / MLA_ROPE)
    kpeo_ref[...] = _rope(kp * lax.rsqrt(ms + EPS) * gkr_ref[...], cos_t, sin_a, sin_b)


def mla_prep(proj, n, tm, tabs, p):
    cos_t, sin_a, sin_b = tabs
    ntab = cos_t.shape[0] // tm
    seg = lambda name: pl.BlockSpec((tm, SEG[name][1]), lambda i, _j=SEG[name][0] // SEG[name][1]: (i, _j))
    tab = pl.BlockSpec((tm, V7X_LANES), lambda i: (i % ntab, 0))
    full = lambda a: pl.BlockSpec(a.shape, lambda i: (0, 0))
    consts = [p["mla_q_norm_g"], p["mla_wq"], p["mla_wq_swap"], p["mla_gq_head"], p["mla_gq_head_swap"],
              p["mla_kv_norm_g"], p["mla_gkr"]]
    return pl.pallas_call(
        _mla_prep_kernel,
        out_shape=[jax.ShapeDtypeStruct((n, MLA_HEADS * V7X_LANES), MXU_DTYPE),
                   jax.ShapeDtypeStruct((n, MLA_KV_RANK), F32), jax.ShapeDtypeStruct((n, V7X_LANES), F32)],
        grid=(n // tm,),
        in_specs=[seg("b_ql"), seg("b_kvl"), seg("b_kpe"), tab, tab, tab] + [full(a) for a in consts],
        out_specs=[pl.BlockSpec((tm, MLA_HEADS * V7X_LANES), lambda i: (i, 0)),
                   pl.BlockSpec((tm, MLA_KV_RANK), lambda i: (i, 0)), pl.BlockSpec((tm, V7X_LANES), lambda i: (i, 0))],
        compiler_params=_cparams("parallel"), name="mla_prep")(proj, proj, proj, cos_t, sin_a, sin_b, *consts)


def _mla_kv_kernel(ckv_ref, kpe_ref, wk_ref, wvt_ref, gk_ref, k_ref, vt_ref):
    c = ckv_ref[0].astype(MXU_DTYPE)
    kk = jnp.dot(c, wk_ref[...], preferred_element_type=F32)
    vt_ref[0, 0] = _mm_nt(wvt_ref[...], c).astype(vt_ref.dtype)
    kpe = kpe_ref[0]
    for h in range(MLA_HEADS):
        sl = slice(h * V7X_LANES, (h + 1) * V7X_LANES)
        blk = kk[:, sl]
        ms = jnp.sum(blk * blk, axis=-1, keepdims=True) * (1.0 / MLA_NOPE)
        k_ref[0, :, sl] = (blk * lax.rsqrt(ms + EPS) * gk_ref[...] + kpe).astype(k_ref.dtype)


def mla_kv(ckv_all, kpe_all, ts, p):
    bsz, s, _ = ckv_all.shape
    full = lambda a: pl.BlockSpec(a.shape, lambda b, i: (0, 0))
    consts = [p["mla_wk"], p["mla_wvt"], p["mla_gk"]]
    return pl.pallas_call(
        _mla_kv_kernel,
        out_shape=[jax.ShapeDtypeStruct((bsz, s, MLA_HEADS * V7X_LANES), MXU_DTYPE),
                   jax.ShapeDtypeStruct((bsz, s // ts, MLA_WIDTH, ts), MXU_DTYPE)],
        grid=(bsz, s // ts),
        in_specs=[pl.BlockSpec((1, ts, MLA_KV_RANK), lambda b, i: (b, i, 0)),
                  pl.BlockSpec((1, ts, V7X_LANES), lambda b, i: (b, i, 0))] + [full(a) for a in consts],
        out_specs=[pl.BlockSpec((1, ts, MLA_HEADS * V7X_LANES), lambda b, i: (b, i, 0)),
                   pl.BlockSpec((1, 1, MLA_WIDTH, ts), lambda b, i: (b, i, 0, 0))],
        compiler_params=_cparams("parallel", "parallel"), name="mla_kv")(ckv_all, kpe_all, *consts)


def _mla_kv_cached_kernel(ckv_old_ref, ckv_new_ref, kpe_old_ref, kpe_new_ref, wk_ref, wv_ref, gk_ref, k_ref, v_ref):
    c = jnp.concatenate([ckv_old_ref[0, 0], ckv_new_ref[0]], axis=0).astype(MXU_DTYPE)
    old_t = kpe_old_ref[0, 0]
    tail = V7X_LANES - KPE_LANE - MLA_ROPE
    old_t = jnp.concatenate([jnp.zeros((KPE_LANE, old_t.shape[1]), F32), old_t, jnp.zeros((tail, old_t.shape[1]), F32)], axis=0)
    kpe = jnp.concatenate([old_t.T, kpe_new_ref[0]], axis=0)
    kk = jnp.dot(c, wk_ref[...], preferred_element_type=F32)
    v_ref[0] = jnp.dot(c, wv_ref[...], preferred_element_type=F32).astype(v_ref.dtype)
    for h in range(MLA_HEADS):
        sl = slice(h * V7X_LANES, (h + 1) * V7X_LANES)
        blk = kk[:, sl]
        ms = jnp.sum(blk * blk, axis=-1, keepdims=True) * (1.0 / MLA_NOPE)
        k_ref[0, :, sl] = (blk * lax.rsqrt(ms + EPS) * gk_ref[...] + kpe).astype(k_ref.dtype)


def mla_kv_cached(ckv_cache, kpe_cache, layer, ckv_new, kpe_new, p):
    _, bsz, past, _ = ckv_cache.shape
    t = ckv_new.shape[1]
    s = past + t
    full = lambda a: pl.BlockSpec(a.shape, lambda b: (0, 0))
    consts = [p["mla_wk"], p["mla_wv"], p["mla_gk"]]
    return pl.pallas_call(
        _mla_kv_cached_kernel,
        out_shape=[jax.ShapeDtypeStruct((bsz, s, MLA_HEADS * V7X_LANES), MXU_DTYPE),
                   jax.ShapeDtypeStruct((bsz, s, MLA_WIDTH), MXU_DTYPE)],
        grid=(bsz,),
        in_specs=[pl.BlockSpec((1, 1, past, MLA_KV_RANK), lambda b: (layer, b, 0, 0)),
                  pl.BlockSpec((1, t, MLA_KV_RANK), lambda b: (b, 0, 0)),
                  pl.BlockSpec((1, 1, MLA_ROPE, past), lambda b: (layer, b, 0, 0)),
                  pl.BlockSpec((1, t, V7X_LANES), lambda b: (b, 0, 0))] + [full(a) for a in consts],
        out_specs=[pl.BlockSpec((1, s, MLA_HEADS * V7X_LANES), lambda b: (b, 0, 0)),
                   pl.BlockSpec((1, s, MLA_WIDTH), lambda b: (b, 0, 0))],
        compiler_params=_cparams("parallel"), name="mla_kv_cached")(ckv_cache, ckv_new, kpe_cache, kpe_new, *consts)


def _mla_attn_rows_kernel(q_ref, k_ref, v_ref, o_ref, *, q_off):
    t, s = q_ref.shape[1], k_ref.shape[1]
    low = _lane_iota(t) < MLA_V
    needs_mask = (s - 1) // CHUNK > q_off // CHUNK
    if needs_mask:
        q_chunk = (q_off + lax.broadcasted_iota(jnp.int32, (t, s), 0)) // CHUNK
        visible = lax.broadcasted_iota(jnp.int32, (t, s), 1) // CHUNK <= q_chunk
    for hp in range(MLA_HEADS // 2):
        v_pair = v_ref[0, :, hp * V7X_LANES:(hp + 1) * V7X_LANES]
        outs = []
        for sub in range(2):
            sl = slice((2 * hp + sub) * V7X_LANES, (2 * hp + sub + 1) * V7X_LANES)
            sc = _mm_nt(q_ref[0, :, sl], k_ref[0, :, sl])
            if needs_mask:
                sc = jnp.where(visible, sc, NEG_BIG)
            e = jnp.exp2(sc - jnp.max(sc, axis=-1, keepdims=True))
            outs.append(_mm(e, v_pair) / jnp.sum(e, axis=-1, keepdims=True))
        o_ref[0, :, hp * V7X_LANES:(hp + 1) * V7X_LANES] = jnp.where(low, outs[0], outs[1]).astype(o_ref.dtype)


def mla_attention_rows(q, k, v, q_off):
    bsz, t, _ = q.shape
    s = k.shape[1]
    return pl.pallas_call(
        functools.partial(_mla_attn_rows_kernel, q_off=q_off),
        out_shape=jax.ShapeDtypeStruct((bsz, t, MLA_WIDTH), MXU_DTYPE), grid=(bsz,),
        in_specs=[pl.BlockSpec((1, t, MLA_HEADS * V7X_LANES), lambda b: (b, 0, 0)),
                  pl.BlockSpec((1, s, MLA_HEADS * V7X_LANES), lambda b: (b, 0, 0)),
                  pl.BlockSpec((1, s, MLA_WIDTH), lambda b: (b, 0, 0))],
        out_specs=pl.BlockSpec((1, t, MLA_WIDTH), lambda b: (b, 0, 0)),
        compiler_params=_cparams("parallel"), name="mla_attention_rows")(q, k, v)


def _mla_attn_kernel(q_ref, k_ref, vt_ref, ot_ref, sa_s, sb_s, *, tq, tkc, q_off, nh):
    assert tq % (2 * tkc) == 0 and q_off % (2 * tkc) == 0 and tkc % CHUNK == 0
    q_start = q_off + pl.program_id(2) * tq
    n_full = q_start // tkc
    n_own = tq // tkc
    lanes = lambda h: slice(h * V7X_LANES, (h + 1) * V7X_LANES)
    qs = [q_ref[0, :, lanes(h)] for h in range(nh)]
    ones = jnp.ones((8, tkc), MXU_DTYPE)

    def produce(c, buf):
        k0 = pl.multiple_of(c * tkc, tkc)
        for h in range(nh):
            buf[h] = _mm_nt(k_ref[0, pl.ds(k0, tkc), lanes(h)], qs[h])

    def consume(c, buf, stats, own=None):
        if own is not None:
            k_chunk = (own * tkc + lax.broadcasted_iota(jnp.int32, (tkc, tq), 0)) // CHUNK
            visible = k_chunk <= lax.broadcasted_iota(jnp.int32, (tkc, tq), 1) // CHUNK
        out = []
        for h in range(nh):
            m, acc = stats[h]
            s = buf[h] if own is None else jnp.where(visible, buf[h], NEG_BIG)
            m_new = jnp.maximum(m, jnp.max(s, axis=0, keepdims=True))
            alpha = jnp.exp2(m - m_new)
            p = jnp.exp2(s - m_new).astype(MXU_DTYPE)
            v_ext = jnp.concatenate([vt_ref[0, c, h * MLA_V:(h + 1) * MLA_V, :], ones], axis=0)
            out.append((m_new, alpha * acc + jnp.dot(v_ext, p, preferred_element_type=F32)))
        return tuple(out)

    def pair(i, stats):
        c = 2 * i
        produce(c + 1, sb_s)
        stats = consume(c, sa_s, stats)
        produce(c + 2, sa_s)
        return consume(c + 1, sb_s, stats)

    init = tuple((jnp.full((1, tq), NEG_BIG, F32), jnp.zeros((MLA_V + 8, tq), F32)) for _ in range(nh))
    produce(0, sa_s)
    stats = lax.fori_loop(0, n_full // 2, pair, init)
    bufs = (sa_s, sb_s)
    for own in range(n_own):
        if own + 1 < n_own:
            produce(n_full + own + 1, bufs[(own + 1) % 2])
        stats = consume(n_full + own, bufs[own % 2], stats, own)
    for h in range(nh):
        acc = stats[h][1]
        ot_ref[0, h * MLA_V:(h + 1) * MLA_V, :] = (acc[:MLA_V] / acc[MLA_V:MLA_V + 1]).astype(ot_ref.dtype)


def mla_attention(q, k, vt, q_off, tq, nh):
    bsz, t, _ = q.shape
    _, nchunk, _, tkc = vt.shape
    s = k.shape[1]
    kern = functools.partial(_mla_attn_kernel, tq=tq, tkc=tkc, q_off=q_off, nh=nh)
    return pl.pallas_call(
        kern, out_shape=jax.ShapeDtypeStruct((bsz, MLA_WIDTH, t), MXU_DTYPE),
        grid=(bsz, MLA_HEADS // nh, t // tq),
        in_specs=[pl.BlockSpec((1, tq, nh * V7X_LANES), lambda b, h, qi: (b, qi, h)),
                  pl.BlockSpec((1, s, nh * V7X_LANES), lambda b, h, qi: (b, 0, h), pipeline_mode=pl.Buffered(1)),
                  pl.BlockSpec((1, nchunk, nh * MLA_V, tkc), lambda b, h, qi: (b, 0, h, 0),
                               pipeline_mode=pl.Buffered(1))],
        out_specs=pl.BlockSpec((1, nh * MLA_V, tq), lambda b, h, qi: (b, h, qi)),
        scratch_shapes=[pltpu.VMEM((nh, tkc, tq), F32), pltpu.VMEM((nh, tkc, tq), F32)],
        compiler_params=_cparams("parallel", "parallel", "parallel"), name="mla_attention")(q, k, vt)


def _band_kernel(q_ref, ka_ref, kb_ref, va_ref, vb_ref, bias_ref, gq_ref, gk_ref, ot_ref, kn_ref, q_s, k_s, vt_s,
                 *, a_is_cache):
    tb = q_ref.shape[0]
    rows_q = q_s.shape[0]
    pad = rows_q - tb
    qn = _head64_rms(q_ref[...]) * gq_ref[...] * (BAND_HEAD_DIM ** -0.5 * math.log2(math.e))
    kb = _head64_rms(kb_ref[...]) * gk_ref[...]
    kn_ref[...] = kb
    if a_is_cache:
        k_s[0:BAND_ROWS, :] = ka_ref[0, 0].T.astype(k_s.dtype)
        vt_s[:, 0:BAND_ROWS] = va_ref[0, 0].astype(vt_s.dtype)
    else:
        k_s[0:BAND_ROWS, :] = (_head64_rms(ka_ref[...]) * gk_ref[...]).astype(k_s.dtype)
        vt_s[:, 0:BAND_ROWS] = va_ref[...].T.astype(vt_s.dtype)
    vb = vb_ref[...]
    if pad:
        zeros = jnp.zeros((pad, BAND_WIDTH), F32)
        qn, kb, vb = (jnp.concatenate([a, zeros], axis=0) for a in (qn, kb, vb))
    q_s[...] = qn.astype(q_s.dtype)
    k_s[BAND_ROWS:, :] = kb.astype(k_s.dtype)
    vt_s[:, BAND_ROWS:] = vb.T.astype(vt_s.dtype)

    _, win, tw = bias_ref.shape
    low = _lane_iota(tw) < BAND_HEAD_DIM
    ones = jnp.ones((8, win), MXU_DTYPE)
    win_row = lax.broadcasted_iota(jnp.int32, (win, 2 * tw), 0)

    def tiles(first):
        for j in range(rows_q // tw):
            r0 = j * tw
            qj = q_s[r0:r0 + tw, :]
            kw = k_s[r0:r0 + win, :]
            ss = []
            for hp in range(BAND_HEADS // 2):
                sl = slice(hp * V7X_LANES, (hp + 1) * V7X_LANES)
                zero = jnp.zeros((), q_s.dtype)
                q2 = jnp.concatenate([jnp.where(low, qj[:, sl], zero), jnp.where(low, zero, qj[:, sl])], axis=0)
                ss.append(_mm_nt(kw[:, sl], q2))
            for hp in range(BAND_HEADS // 2):
                sl = slice(hp * V7X_LANES, (hp + 1) * V7X_LANES)
                v_ext = jnp.concatenate([vt_s[sl, r0:r0 + win], ones], axis=0)
                s = ss[hp] + jnp.concatenate([bias_ref[2 * hp], bias_ref[2 * hp + 1]], axis=1)
                if first:
                    s = jnp.where(win_row + r0 < BAND_ROWS, NEG_BIG, s)
                p = jnp.exp2(s - jnp.max(s, axis=0, keepdims=True)).astype(MXU_DTYPE)
                res = jnp.dot(v_ext, p, preferred_element_type=F32)
                den = res[V7X_LANES:V7X_LANES + 1]
                halves = [res[0:BAND_HEAD_DIM, 0:tw] / den[:, 0:tw], res[BAND_HEAD_DIM:V7X_LANES, tw:] / den[:, tw:]]
                o_pair = jnp.concatenate(halves, axis=0).astype(ot_ref.dtype)
                if pad:
                    ot_ref[0, sl, :] = o_pair[:, 0:tb]
                else:
                    ot_ref[0, sl, r0:r0 + tw] = o_pair

    if a_is_cache:
        tiles(False)
    else:
        pl.when(pl.program_id(1) == 0)(lambda: tiles(True))
        pl.when(pl.program_id(1) > 0)(lambda: tiles(False))


def band_branch(q_src, k_prev, v_prev, layer, bsz, t, bias, gq, gk):
    a_is_cache = k_prev is not None
    tb = min(t, BAND_ROWS)
    nt = t // tb
    tw = bias.shape[2]
    rows_q = -(-tb // tw) * tw
    col = lambda name: SEG[name][0] // BAND_WIDTH
    cur = lambda name: pl.BlockSpec((tb, BAND_WIDTH), lambda b, i, _j=col(name): (b * nt + i, _j))
    if a_is_cache:
        prev = lambda name: pl.BlockSpec((1, 1, BAND_WIDTH, BAND_ROWS), lambda b, i: (layer, b, 0, 0))
        ka, va = k_prev, v_prev
    else:
        prev = lambda name: pl.BlockSpec((BAND_ROWS, BAND_WIDTH), lambda b, i, _j=col(name): (b * nt + jnp.maximum(i - 1, 0), _j))
        ka, va = q_src, q_src
    full = lambda a: pl.BlockSpec(a.shape, lambda b, i: (0,) * a.ndim)
    kern = functools.partial(_band_kernel, a_is_cache=a_is_cache)
    return pl.pallas_call(
        kern,
        out_shape=[jax.ShapeDtypeStruct((bsz, BAND_WIDTH, t), MXU_DTYPE), jax.ShapeDtypeStruct((bsz * t, BAND_WIDTH), F32)],
        grid=(bsz, nt),
        in_specs=[cur("d_q"), prev("d_k"), cur("d_k"), prev("d_v"), cur("d_v"), full(bias), full(gq), full(gk)],
        out_specs=[pl.BlockSpec((1, BAND_WIDTH, tb), lambda b, i: (b, 0, i)),
                   pl.BlockSpec((tb, BAND_WIDTH), lambda b, i: (b * nt + i, 0))],
        scratch_shapes=[pltpu.VMEM((rows_q, BAND_WIDTH), MXU_DTYPE), pltpu.VMEM((BAND_ROWS + rows_q, BAND_WIDTH), MXU_DTYPE),
                        pltpu.VMEM((BAND_WIDTH, BAND_ROWS + rows_q), MXU_DTYPE)],
        compiler_params=_cparams("parallel", "parallel"), name="band_branch")(
            q_src, ka, q_src, va, q_src, bias, gq, gk)


def _head_blocks(w, per_head, used):
    k = w.shape[0]
    w = w.reshape(k, -1, per_head)[:, :, :used]
    return jnp.pad(w, ((0, 0), (0, 0), (0, V7X_LANES - used))).reshape(k, -1)


def _lane_pad(v, offset, width=V7X_LANES):
    return jnp.pad(v, (offset, width - offset - v.shape[0])).reshape(1, width)


def _layer_params(l, w):
    src = {}
    off = 0
    for name, width in (("a_z", 512), ("a_x", 512), ("a_b", 256), ("a_c", 256), ("a_dt", 8), ("b_ql", 384),
                        ("b_kvl", 256), ("b_kpe", 32), ("c_b", 512), ("c_c", 512), ("c_x", 512),
                        ("d_q", 512), ("d_k", 512), ("d_v", 512), ("gate", 4096)):
        src[name] = (off, width)
        off += width
    w_in_t = w["w_in"][l].T.astype(MXU_DTYPE)
    pieces = []
    for name, (dst, dwidth) in sorted(SEG.items(), key=lambda kv: kv[1][0]):
        s0, sw = src[name]
        lead = KPE_LANE if name == "b_kpe" else 0
        for width in (lead, None, dwidth - sw - lead):
            if width is None:
                pieces.append(w_in_t[s0:s0 + sw])
            elif width:
                pieces.append(jnp.zeros((width, D_MODEL), MXU_DTYPE))
    p = {"w_in_t": jnp.concatenate(pieces, axis=0)}
    cast = lambda a: a.astype(MXU_DTYPE)
    row = lambda a: a.reshape(1, -1)
    p["norm_mix_g"], p["norm_ffn_g"], p["b_gate"] = w["norm_mix_g"][l], w["norm_ffn_g"][l], w["b_gate"][l]
    for name in ("w_a_out", "w_b_out", "w_c_out", "w_d_out", "w_o", "w_ffn_down"):
        p[name] = cast(w[name][l])
    ffn_tk = FF_HIDDEN // 2
    p["w_ffn_up_tiles"] = cast(w["w_ffn_up"][l]).reshape(D_MODEL, 2 * FF_HIDDEN // ffn_tk, ffn_tk).swapaxes(0, 1)
    p["ssd_conv_w"] = w["ssd_conv_w"][l]
    p["ssd_conv_b"] = row(w["ssd_conv_b"][l])
    p["ssd_dt_bias"] = _lane_pad(w["ssd_dt_bias"][l], 0)
    p["ssd_a_log"] = row(jnp.repeat(w["ssd_a_log"][l], SSD_HEAD_DIM))
    p["ssd_d"] = row(jnp.repeat(w["ssd_d"][l], SSD_HEAD_DIM))
    p["ssd_norm_g"] = row(w["ssd_norm_g"][l])
    head_of_lane = jnp.arange(SSD_INNER) // SSD_HEAD_DIM
    p["ssd_expand"] = (jnp.arange(V7X_LANES)[:, None] == head_of_lane[None, :]).astype(F32)
    p["mla_q_norm_g"] = row(w["mla_q_norm_g"][l])
    p["mla_wq"] = cast(_head_blocks(w["mla_w_q_up"][l], MLA_NOPE + MLA_ROPE, MLA_NOPE + MLA_ROPE))
    gqh = jnp.concatenate([w["mla_qn_g"][l], w["mla_qr_g"][l], jnp.zeros((V7X_LANES - MLA_NOPE - MLA_ROPE,), F32)])
    p["mla_gq_head"] = row(jnp.tile(gqh, MLA_HEADS))
    half = MLA_ROPE // 2
    partner = np.concatenate([np.arange(MLA_NOPE + half, MLA_NOPE + MLA_ROPE), np.arange(MLA_NOPE, MLA_NOPE + half)])
    wq_heads = w["mla_w_q_up"][l].reshape(MLA_Q_RANK, MLA_HEADS, MLA_NOPE + MLA_ROPE)
    wq_swap = jnp.pad(wq_heads[:, :, partner], ((0, 0), (0, 0), (MLA_NOPE, V7X_LANES - MLA_NOPE - MLA_ROPE)))
    p["mla_wq_swap"] = cast(wq_swap.reshape(MLA_Q_RANK, MLA_HEADS * V7X_LANES))
    gqh_swap = jnp.pad(w["mla_qr_g"][l][partner - MLA_NOPE], (MLA_NOPE, V7X_LANES - MLA_NOPE - MLA_ROPE))
    p["mla_gq_head_swap"] = row(jnp.tile(gqh_swap, MLA_HEADS))
    p["mla_kv_norm_g"] = row(w["mla_kv_norm_g"][l])
    p["mla_gkr"] = _lane_pad(w["mla_kr_g"][l], KPE_LANE)
    p["mla_wk"] = cast(_head_blocks(w["mla_w_kv_up"][l], MLA_NOPE + MLA_V, MLA_NOPE))
    p["mla_wv"] = cast(w["mla_w_kv_up"][l].reshape(MLA_KV_RANK, MLA_HEADS, MLA_NOPE + MLA_V)[:, :, MLA_NOPE:]
                       .reshape(MLA_KV_RANK, MLA_WIDTH))
    p["mla_wvt"] = p["mla_wv"].T
    p["mla_gk"] = _lane_pad(w["mla_kn_g"][l], 0)
    p["sconv_w"] = w["sconv_w"][l]
    p["band_gq"] = row(jnp.tile(w["band_qn_g"][l], BAND_HEADS))
    p["band_gk"] = row(jnp.tile(w["band_kn_g"][l], BAND_HEADS))
    p["band_rel_bias"] = w["band_rel_bias"][l]
    return p


def _band_bias_table(rel_bias, nqry):
    nkey = BAND_ROWS + nqry
    key = np.arange(nkey)[:, None]
    qry = np.arange(nqry)[None, :]
    in_band = np.logical_and(key // CHUNK >= qry // CHUNK, key // CHUNK <= qry // CHUNK + BAND_PAST_CHUNKS)
    diff = np.arange(nqry - 1 + BAND_ROWS, -nkey + BAND_ROWS, -1)
    by_diff = rel_bias[:, np.clip(diff, -REL_CLIP, REL_CLIP) + REL_CLIP]
    skew = jnp.tile(jnp.pad(by_diff, ((0, 0), (0, 1)))[:, None, :], (1, nqry, 1)).reshape(BAND_HEADS, -1)
    skew = skew[:, :nqry * (nkey + nqry - 1)].reshape(BAND_HEADS, nqry, nkey + nqry - 1)
    bias = jnp.swapaxes(skew[:, :, nqry - 1:], 1, 2) * math.log2(math.e)
    return jnp.where(in_band[None], bias, NEG_BIG)


def _rope_tables(pos, rows):
    inv = ROPE_THETA ** (-jnp.arange(0, MLA_ROPE, 2, dtype=F32) / MLA_ROPE)
    ang = pos.astype(F32)[:, None] * inv[None, :]
    cos, sin = jnp.cos(ang), jnp.sin(ang)
    half = MLA_ROPE // 2
    z = lambda w: jnp.zeros((pos.shape[0], w), F32)
    tail = V7X_LANES - MLA_NOPE - MLA_ROPE
    cos_t = jnp.concatenate([jnp.ones((pos.shape[0], MLA_NOPE), F32), cos, cos, z(tail)], axis=1)
    sin_a = jnp.concatenate([z(MLA_NOPE + half), sin, z(tail)], axis=1)
    sin_b = jnp.concatenate([z(MLA_NOPE), -sin, z(half + tail)], axis=1)
    reps = max(rows // pos.shape[0], 1)
    return tuple(jnp.tile(a, (reps, 1)) for a in (cos_t, sin_a, sin_b))


def _layer(x, bsz, t, past, p, tabs, q_off, layer):
    n = bsz * t
    tm = min(n, 512)
    proj = norm_matmul_nt(x, p["norm_mix_g"], p["w_in_t"], min(n, 1024), 1920)
    ya, ssm_new, ssd_conv_new = ssd_branch(proj, bsz, t, past["ssd_conv"], past["ssm"], p)
    yc, sconv_new = sconv_branch(proj, bsz, t, past["sconv"], p["sconv_w"])
    q, ckv, kpe_pad = mla_prep(proj, n, tm, tabs, p)
    q3, ckv3, kpe3 = q.reshape(bsz, t, -1), ckv.reshape(bsz, t, MLA_KV_RANK), kpe_pad.reshape(bsz, t, V7X_LANES)
    if past["mla_ckv"] is not None:
        assert t <= V7X_LANES
        kk, vv = mla_kv_cached(past["mla_ckv"], past["mla_kpe"], layer, ckv3, kpe3, p)
        ob = mla_attention_rows(q3, kk, vv, q_off).reshape(n, MLA_WIDTH)
    else:
        tkc = next(c for c in (256, 128, t) if t % c == 0)
        kk, vt = mla_kv(ckv3, kpe3, tkc, p)
        ob = mla_attention(q3, kk, vt, q_off, min(t, 512), 8)
    band_bias = _band_bias_table(p["band_rel_bias"], 2 * V7X_LANES if t % (2 * V7X_LANES) == 0 else V7X_LANES)
    od, kn = band_branch(proj, past["band_k"], past["band_v"], layer, bsz, t, band_bias, p["band_gq"], p["band_gk"])
    if ob.ndim != od.ndim or t % tm:
        ob = ob if ob.ndim == 2 else jnp.swapaxes(ob, 1, 2).reshape(n, MLA_WIDTH)
        od = jnp.swapaxes(od, 1, 2).reshape(n, BAND_WIDTH)
    x = merge(x, proj, ya, ob, yc, od, p["b_gate"], p["w_a_out"], p["w_b_out"], p["w_c_out"], p["w_d_out"], p["w_o"], tm)
    x = ffn(x, p["norm_ffn_g"], p["w_ffn_up_tiles"], p["w_ffn_down"], tm)
    dv0 = SEG["d_v"][0]
    keep = min(BAND_ROWS, t)
    heads = lambda a: a.reshape(bsz, keep, BAND_HEADS, BAND_HEAD_DIM)
    kn = heads(kn.reshape(bsz, t, BAND_WIDTH)[:, t - keep:])
    vn = heads(proj.reshape(bsz, t, PROJ_COLS)[:, t - keep:, dv0:dv0 + BAND_WIDTH])
    new = {"ssm": ssm_new, "ssd_conv": ssd_conv_new, "sconv": sconv_new,
           "mla_ckv": ckv.reshape(bsz, t, MLA_KV_RANK),
           "mla_kpe": kpe_pad[:, KPE_LANE:KPE_LANE + MLA_ROPE].reshape(bsz, t, MLA_ROPE),
           "band_k": kn, "band_v": vn}
    return x, new


def kernel(x_prompt, x_sample, state_ssm, state_ssd_conv, cache_mla_ckv, cache_mla_kpe, state_sconv, cache_band_k, cache_band_v, norm_mix_g, w_in, b_gate, ssd_conv_w, ssd_conv_b, ssd_dt_bias, ssd_a_log, ssd_d, ssd_norm_g, w_a_out, mla_q_norm_g, mla_w_q_up, mla_kv_norm_g, mla_w_kv_up, mla_qn_g, mla_kn_g, mla_qr_g, mla_kr_g, w_b_out, sconv_w, w_c_out, band_qn_g, band_kn_g, band_rel_bias, w_d_out, w_o, norm_ffn_g, w_ffn_up, w_ffn_down):
    weights = dict(norm_mix_g=norm_mix_g, w_in=w_in, b_gate=b_gate, ssd_conv_w=ssd_conv_w, ssd_conv_b=ssd_conv_b,
                   ssd_dt_bias=ssd_dt_bias, ssd_a_log=ssd_a_log, ssd_d=ssd_d, ssd_norm_g=ssd_norm_g, w_a_out=w_a_out,
                   mla_q_norm_g=mla_q_norm_g, mla_w_q_up=mla_w_q_up, mla_kv_norm_g=mla_kv_norm_g,
                   mla_w_kv_up=mla_w_kv_up, mla_qn_g=mla_qn_g, mla_kn_g=mla_kn_g, mla_qr_g=mla_qr_g,
                   mla_kr_g=mla_kr_g, w_b_out=w_b_out, sconv_w=sconv_w, w_c_out=w_c_out, band_qn_g=band_qn_g,
                   band_kn_g=band_kn_g, band_rel_bias=band_rel_bias, w_d_out=w_d_out, w_o=w_o,
                   norm_ffn_g=norm_ffn_g, w_ffn_up=w_ffn_up, w_ffn_down=w_ffn_down)
    depth = w_in.shape[0]
    b_p, t_p, d = x_prompt.shape
    b_s, t_s, _ = x_sample.shape
    past_len = cache_mla_ckv.shape[2]
    assert d == D_MODEL and t_p % BAND_ROWS == 0 and t_s == CHUNK and cache_band_k.shape[2] == BAND_ROWS
    assert past_len % CHUNK == 0
    tabs_p = _rope_tables(jnp.arange(t_p, dtype=jnp.int32), min(b_p * t_p, 512))
    tabs_s = _rope_tables(past_len + jnp.arange(t_s, dtype=jnp.int32), min(b_s * t_s, 512))
    y_p = x_prompt.reshape(b_p * t_p, d)
    y_s = x_sample.reshape(b_s * t_s, d)
    new_p, new_s = [], []
    band_k_rows = jnp.transpose(cache_band_k, (0, 1, 3, 4, 2)).reshape(depth, b_s, BAND_WIDTH, BAND_ROWS)
    band_v_rows = jnp.transpose(cache_band_v, (0, 1, 3, 4, 2)).reshape(depth, b_s, BAND_WIDTH, BAND_ROWS)
    kpe_cache_t = jnp.swapaxes(cache_mla_kpe, 2, 3)
    for l in range(depth):
        p = _layer_params(l, weights)
        init_p = {"ssm": jnp.zeros((b_p, SSD_HEADS, SSD_HEAD_DIM, SSD_STATE), F32),
                  "ssd_conv": jnp.zeros((b_p, SSD_CONV - 1, SSD_CONV_DIM), F32),
                  "sconv": jnp.zeros((b_p, SCONV_K - 1, SCONV_WIDTH), F32),
                  "mla_ckv": None, "mla_kpe": None, "band_k": None, "band_v": None}
        y_p, st_p = _layer(y_p, b_p, t_p, init_p, p, tabs_p, 0, l)
        new_p.append(st_p)
        past_s = {"ssm": state_ssm[l], "ssd_conv": state_ssd_conv[l], "sconv": state_sconv[l],
                  "mla_ckv": cache_mla_ckv, "mla_kpe": kpe_cache_t,
                  "band_k": band_k_rows, "band_v": band_v_rows}
        y_s, st_s = _layer(y_s, b_s, t_s, past_s, p, tabs_s, past_len, l)
        new_s.append(st_s)

    def stack(states, name):
        return jnp.stack([s[name] for s in states], axis=0)

    out = [y_p.reshape(b_p, t_p, d), y_s.reshape(b_s, t_s, d)]
    for name in ("ssm", "ssd_conv", "mla_ckv", "mla_kpe", "sconv"):
        out += [stack(new_p, name), stack(new_s, name)]
    for name, cache in (("band_k", cache_band_k), ("band_v", cache_band_v)):
        out += [stack(new_p, name), jnp.concatenate([cache[:, :, t_s:], stack(new_s, name)], axis=2)]
    return tuple(out)
```

```python
import functools
import math

import jax
import jax.numpy as jnp
import numpy as np
from jax import lax
from jax.experimental import pallas as pl
from jax.experimental.pallas import tpu as pltpu

F32 = jnp.float32
MXU_DTYPE = jnp.bfloat16
EPS = 1e-6
NEG_BIG = -1e30

V7X_LANES = 128
V7X_VMEM_LIMIT_BYTES = 56 * 1024 * 1024

D_MODEL = 1024
CHUNK = 64
N_BRANCH = 4
SSD_HEADS, SSD_HEAD_DIM, SSD_GROUPS, SSD_STATE, SSD_CONV = 8, 64, 2, 128, 4
SSD_INNER = SSD_HEADS * SSD_HEAD_DIM
SSD_CONV_DIM = SSD_INNER + 2 * SSD_GROUPS * SSD_STATE
MLA_HEADS, MLA_Q_RANK, MLA_KV_RANK, MLA_NOPE, MLA_ROPE, MLA_V = 8, 384, 256, 64, 32, 64
MLA_WIDTH = MLA_HEADS * MLA_V
ROPE_THETA = 10000.0
SCONV_WIDTH, SCONV_K = 512, 3
BAND_HEADS, BAND_HEAD_DIM, BAND_PAST_CHUNKS, REL_CLIP = 8, 64, 8, 128
BAND_WIDTH = BAND_HEADS * BAND_HEAD_DIM
BAND_ROWS = BAND_PAST_CHUNKS * CHUNK
BAND_SPAN = BAND_ROWS + CHUNK
FF_HIDDEN = ((8 * D_MODEL // 3 + 255) // 256) * 256

SEG = {
    "gate": (0, 4096), "c_b": (4096, 512), "c_c": (4608, 512), "c_x": (5120, 512),
    "d_q": (5632, 512), "d_k": (6144, 512), "d_v": (6656, 512),
    "a_z": (7168, 512), "a_x": (7680, 512), "a_b": (8192, 256), "a_c": (8448, 256),
    "b_kvl": (8704, 256), "a_dt": (8960, 128), "b_kpe": (9088, 128), "b_ql": (9216, 384),
}
PROJ_COLS = 9600
KPE_LANE = 64


def _cparams(*sem):
    return pltpu.CompilerParams(dimension_semantics=sem, vmem_limit_bytes=V7X_VMEM_LIMIT_BYTES)


def _mm(a, b):
    return jnp.dot(a.astype(MXU_DTYPE), b.astype(MXU_DTYPE), preferred_element_type=F32)


def _mm_nt(a, b):
    return lax.dot_general(a.astype(MXU_DTYPE), b.astype(MXU_DTYPE), (((1,), (1,)), ((), ())),
                           preferred_element_type=F32)


def _mm_tn(a, b):
    return lax.dot_general(a.astype(MXU_DTYPE), b.astype(MXU_DTYPE), (((0,), (0,)), ((), ())),
                           preferred_element_type=F32)


def _split3(x):
    hi = x.astype(MXU_DTYPE)
    r1 = x - hi.astype(F32)
    mid = r1.astype(MXU_DTYPE)
    lo = (r1 - mid.astype(F32)).astype(MXU_DTYPE)
    return hi, mid, lo


def _mm_exact_rhs01(x, sel):
    hi, mid, lo = _split3(x)
    sel = sel.astype(MXU_DTYPE)
    d = functools.partial(jnp.dot, preferred_element_type=F32)
    return d(hi, sel) + d(mid, sel) + d(lo, sel)


def _silu(x):
    return x * jax.nn.sigmoid(x)


def _row_rms(x):
    return x * lax.rsqrt(jnp.mean(x * x, axis=-1, keepdims=True) + EPS)


def _lane_iota(rows):
    return lax.broadcasted_iota(jnp.int32, (rows, V7X_LANES), 1)


def _head64_rms(x):
    rows, width = x.shape
    low = _lane_iota(rows) < 64
    out = []
    for j in range(width // V7X_LANES):
        blk = x[:, j * V7X_LANES:(j + 1) * V7X_LANES]
        sq = blk * blk
        s_lo = jnp.sum(jnp.where(low, sq, 0.0), axis=-1, keepdims=True) * (1.0 / 64)
        s_hi = jnp.sum(jnp.where(low, 0.0, sq), axis=-1, keepdims=True) * (1.0 / 64)
        out.append(blk * jnp.where(low, lax.rsqrt(s_lo + EPS), lax.rsqrt(s_hi + EPS)))
    return jnp.concatenate(out, axis=-1)


def _norm_matmul_nt_kernel(x_ref, g_ref, wt_ref, o_ref, h_s):
    j = pl.program_id(1)
    tn = o_ref.shape[1]

    @pl.when(j == 0)
    def _():
        h_s[...] = (_row_rms(x_ref[...]) * g_ref[...]).astype(h_s.dtype)

    o_ref[...] = _mm_nt(h_s[...], wt_ref[pl.ds(pl.multiple_of(j * tn, tn), tn), :])


def norm_matmul_nt(x, g, wt, tm, tn):
    n, k = x.shape
    c = wt.shape[0]
    return pl.pallas_call(
        _norm_matmul_nt_kernel, out_shape=jax.ShapeDtypeStruct((n, c), F32), grid=(n // tm, c // tn),
        in_specs=[pl.BlockSpec((tm, k), lambda i, j: (i, 0)), pl.BlockSpec((1, k), lambda i, j: (0, 0)),
                  pl.BlockSpec((c, k), lambda i, j: (0, 0), pipeline_mode=pl.Buffered(1))],
        out_specs=pl.BlockSpec((tm, tn), lambda i, j: (i, j)),
        scratch_shapes=[pltpu.VMEM((tm, k), MXU_DTYPE)],
        compiler_params=_cparams("parallel", "arbitrary"), name="in_proj_matmul")(x, g.reshape(1, k), wt)


def _merge_kernel(x_ref, g_ref, ya_ref, ob_ref, yc_ref, od_ref, bg_ref, wa_ref, wb_ref, wc_ref, wd_ref, wo_ref, o_ref,
                  *, bd_transposed):
    def gate(k):
        return jax.nn.sigmoid(g_ref[:, k * D_MODEL:(k + 1) * D_MODEL] + bg_ref[:, k * D_MODEL:(k + 1) * D_MODEL])

    def attn_out(a_ref, w_ref):
        return _mm_tn(a_ref[0], w_ref[...]) if bd_transposed else _mm(a_ref[...], w_ref[...])

    merged = gate(0) * _mm(ya_ref[...], wa_ref[...])
    merged = merged + gate(1) * attn_out(ob_ref, wb_ref)
    merged = merged + gate(2) * _mm(yc_ref[...], wc_ref[...])
    merged = merged + gate(3) * attn_out(od_ref, wd_ref)
    o_ref[...] = x_ref[...] + _mm(merged, wo_ref[...])


def merge(x, proj, ya, ob, yc, od, b_gate, wa, wb, wc, wd, wo, tm):
    n, d = x.shape
    row = lambda w: pl.BlockSpec((tm, w), lambda i: (i, 0))
    full = lambda a: pl.BlockSpec(a.shape, lambda i: (0, 0))
    bd_transposed = ob.ndim == 3
    if bd_transposed:
        per_seq = ob.shape[2] // tm
        attn = pl.BlockSpec((1, ob.shape[1], tm), lambda i: (i // per_seq, 0, i % per_seq))
    else:
        attn = row(512)
    bg = b_gate.reshape(1, N_BRANCH * d)
    return pl.pallas_call(
        functools.partial(_merge_kernel, bd_transposed=bd_transposed),
        out_shape=jax.ShapeDtypeStruct((n, d), F32), grid=(n // tm,),
        in_specs=[row(d), pl.BlockSpec((tm, N_BRANCH * d), lambda i: (i, 0)), row(512), attn, row(512), attn,
                  full(bg), full(wa), full(wb), full(wc), full(wd), full(wo)],
        out_specs=row(d), compiler_params=_cparams("parallel"), name="merge")(
            x, proj, ya, ob, yc, od, bg, wa, wb, wc, wd, wo)


def _ffn_kernel(x_ref, g_ref, wup_ref, wdn_ref, o_ref):
    nk, _, tk = wup_ref.shape
    nk //= 2
    x = x_ref[...]
    h = (_row_rms(x) * g_ref[...]).astype(MXU_DTYPE)
    out = x
    for k in range(nk):
        gate = jnp.dot(h, wup_ref[k], preferred_element_type=F32)
        val = jnp.dot(h, wup_ref[nk + k], preferred_element_type=F32)
        out = out + _mm(_silu(gate) * val, wdn_ref[k * tk:(k + 1) * tk, :])
    o_ref[...] = out


def ffn(x, g, w_up_tiles, w_down, tm):
    n, d = x.shape
    resident = lambda a: pl.BlockSpec(a.shape, lambda i: (0,) * a.ndim, pipeline_mode=pl.Buffered(1))
    return pl.pallas_call(
        _ffn_kernel, out_shape=jax.ShapeDtypeStruct((n, d), F32), grid=(n // tm,),
        in_specs=[pl.BlockSpec((tm, d), lambda i: (i, 0)), pl.BlockSpec((1, d), lambda i: (0, 0)),
                  resident(w_up_tiles), resident(w_down)],
        out_specs=pl.BlockSpec((tm, d), lambda i: (i, 0)),
        compiler_params=_cparams("parallel"), name="ffn")(x, g.reshape(1, d), w_up_tiles, w_down)


def _ssd_kernel(z_ref, x_ref, b_ref, c_ref, dt_ref, conv0_ref, h0_ref, cw_ref, cbias_ref, dtb_ref, alog_ref,
                dfull_ref, ng_ref, tri_ref, expand_ref, y_ref, convn_ref, hn_ref, xp_s, h_s):
    L = CHUNK
    rows = z_ref.shape[0]
    c = pl.program_id(1)

    @pl.when(c == 0)
    def _():
        xp_s[0:8, :] = conv0_ref[0]
        h_s[...] = h0_ref[0]

    xp_s[8:8 + rows, 0:512] = x_ref[...]
    xp_s[8:8 + rows, 512:768] = b_ref[...]
    xp_s[8:8 + rows, 768:1024] = c_ref[...]
    base = 8 - (SSD_CONV - 1)
    acc = xp_s[base:base + rows, :] * cw_ref[0:1, :]
    for i in range(1, SSD_CONV):
        acc = acc + xp_s[base + i:base + i + rows, :] * cw_ref[i:i + 1, :]
    xbc = _silu(acc + cbias_ref[...])
    tail = xp_s[rows:rows + 8, :]
    convn_ref[0] = tail
    xp_s[0:8, :] = tail

    xs = xbc[:, 0:512]
    dtr = dt_ref[...] + dtb_ref[...]
    dt = jnp.maximum(dtr, 0.0) + jnp.log1p(jnp.exp(-jnp.abs(dtr)))
    dt_full = _mm_exact_rhs01(dt, expand_ref[...])
    da_full = dt_full * (-jnp.exp(alog_ref[...]))
    da_parts = _split3(da_full)
    dot = functools.partial(jnp.dot, preferred_element_type=F32)
    tri = tri_ref[...].astype(MXU_DTYPE)
    acs = dot(tri, da_parts[0]) + dot(tri, da_parts[1]) + dot(tri, da_parts[2])
    x_dt = xs * dt_full
    exp_acs = jnp.exp(acs)

    row = lax.broadcasted_iota(jnp.int32, (L, V7X_LANES), 0)
    lane = _lane_iota(L)
    lane_s = jnp.where(lane < 64, lane, lane - 64)
    diag2 = row == lane_s
    tril2 = lane_s <= row
    row2 = lax.broadcasted_iota(jnp.int32, (2 * L, V7X_LANES), 0)
    lane2 = lax.broadcasted_iota(jnp.int32, (2 * L, V7X_LANES), 1)
    blockdiag = (row2 < L) == (lane2 < 64)
    ones_ln = jnp.ones((L, SSD_STATE), MXU_DTYPE)
    dot_tn = functools.partial(lax.dot_general, dimension_numbers=(((0,), (0,)), ((), ())), preferred_element_type=F32)

    h = [h_s[256 * g:256 * (g + 1), :] for g in range(SSD_GROUPS)]
    y_chunks = []
    for k in range(rows // L):
        r = slice(k * L, (k + 1) * L)
        acs_k = acs[r]
        xd_end = x_dt[r] * jnp.exp(acs_k[L - 1:L, :] - acs_k)
        y_parts = []
        for g in range(SSD_GROUPS):
            cols = slice(256 * g, 256 * (g + 1))
            bg = xbc[r, 512 + 128 * g:512 + 128 * (g + 1)]
            cg = xbc[r, 768 + 128 * g:768 + 128 * (g + 1)]
            cb2 = _mm_nt(cg, jnp.concatenate([bg, bg], axis=0))
            y_off = _mm_nt(cg, h[g]) * exp_acs[r, cols]
            for jp in range(2):
                sl = slice(128 * (2 * g + jp), 128 * (2 * g + jp + 1))
                blk = acs_k[:, sl]
                at_s = jnp.sum(jnp.where(diag2, blk, 0.0), axis=0, keepdims=True)
                decay = jnp.where(tril2, jnp.exp(jnp.minimum(blk - at_s, 0.0)), 0.0)
                xpair = x_dt[r, sl]
                xblk = jnp.where(blockdiag, jnp.concatenate([xpair, xpair], axis=0), 0.0)
                y_parts.append(_mm(cb2 * decay, xblk) + y_off[:, 128 * jp:128 * (jp + 1)])
            state = _mm_tn(xd_end[:, cols], bg)
            total = sum(dot_tn(part[r, cols], ones_ln) for part in da_parts)
            h[g] = jnp.exp(total) * h[g] + state
        y_chunks.append(jnp.concatenate(y_parts, axis=-1))
    y = jnp.concatenate(y_chunks, axis=0) + dfull_ref[...] * xs
    gated = y * _silu(z_ref[...])
    y_ref[...] = (_row_rms(gated) * ng_ref[...]).astype(y_ref.dtype)
    for g in range(SSD_GROUPS):
        h_s[256 * g:256 * (g + 1), :] = h[g]
        hn_ref[0, 256 * g:256 * (g + 1), :] = h[g]


def ssd_branch(proj, bsz, t, conv_prev, ssm_prev, p):
    rows = math.gcd(t, 8 * CHUNK)
    nb = t // rows
    n = bsz * t
    col = lambda name: SEG[name][0] // SEG[name][1]
    seg = lambda name: pl.BlockSpec((rows, SEG[name][1]), lambda b, c, _j=col(name): (b * nb + c, _j))
    full = lambda a: pl.BlockSpec(a.shape, lambda b, c: (0,) * a.ndim)
    conv0 = jnp.pad(conv_prev, ((0, 0), (8 - (SSD_CONV - 1), 0), (0, 0)))
    h0 = ssm_prev.reshape(bsz, SSD_INNER, SSD_STATE)
    tri = jnp.kron(jnp.eye(rows // CHUNK, dtype=F32), jnp.tril(jnp.ones((CHUNK, CHUNK), F32)))
    consts = [p["ssd_conv_w"], p["ssd_conv_b"], p["ssd_dt_bias"], p["ssd_a_log"], p["ssd_d"], p["ssd_norm_g"], tri,
              p["ssd_expand"]]
    y, convn, hn = pl.pallas_call(
        _ssd_kernel,
        out_shape=[jax.ShapeDtypeStruct((n, SSD_INNER), MXU_DTYPE), jax.ShapeDtypeStruct((bsz, 8, SSD_CONV_DIM), F32),
                   jax.ShapeDtypeStruct((bsz, SSD_INNER, SSD_STATE), F32)],
        grid=(bsz, nb),
        in_specs=[seg("a_z"), seg("a_x"), seg("a_b"), seg("a_c"), seg("a_dt"),
                  pl.BlockSpec((1, 8, SSD_CONV_DIM), lambda b, c: (b, 0, 0)),
                  pl.BlockSpec((1, SSD_INNER, SSD_STATE), lambda b, c: (b, 0, 0))] + [full(a) for a in consts],
        out_specs=[pl.BlockSpec((rows, SSD_INNER), lambda b, c: (b * nb + c, 0)),
                   pl.BlockSpec((1, 8, SSD_CONV_DIM), lambda b, c: (b, 0, 0)),
                   pl.BlockSpec((1, SSD_INNER, SSD_STATE), lambda b, c: (b, 0, 0))],
        scratch_shapes=[pltpu.VMEM((rows + 8, SSD_CONV_DIM), F32), pltpu.VMEM((SSD_INNER, SSD_STATE), F32)],
        compiler_params=_cparams("parallel", "arbitrary"), name="ssd_branch")(
            proj, proj, proj, proj, proj, conv0, h0, *consts)
    return y, hn.reshape(bsz, SSD_HEADS, SSD_HEAD_DIM, SSD_STATE), convn[:, 8 - (SSD_CONV - 1):]


def _sconv_kernel(cb_ref, cc_ref, cx_ref, st0_ref, w_ref, o_ref, stn_ref, up_s):
    rows = cb_ref.shape[0]

    @pl.when(pl.program_id(1) == 0)
    def _():
        up_s[0:8, :] = st0_ref[0]

    up_s[8:8 + rows, :] = cc_ref[...] * cx_ref[...]
    base = 8 - (SCONV_K - 1)
    uc = up_s[base:base + rows, :] * w_ref[0:1, :]
    for i in range(1, SCONV_K):
        uc = uc + up_s[base + i:base + i + rows, :] * w_ref[i:i + 1, :]
    o_ref[...] = (cb_ref[...] * uc).astype(o_ref.dtype)
    tail = up_s[rows:rows + 8, :]
    stn_ref[0] = tail
    up_s[0:8, :] = tail


def sconv_branch(proj, bsz, t, prev, w):
    rows = min(t, 512)
    nt = t // rows
    col = lambda name: SEG[name][0] // SCONV_WIDTH
    seg = lambda name: pl.BlockSpec((rows, SCONV_WIDTH), lambda b, c, _j=col(name): (b * nt + c, _j))
    st0 = jnp.pad(prev, ((0, 0), (8 - (SCONV_K - 1), 0), (0, 0)))
    o, stn = pl.pallas_call(
        _sconv_kernel,
        out_shape=[jax.ShapeDtypeStruct((bsz * t, SCONV_WIDTH), MXU_DTYPE), jax.ShapeDtypeStruct((bsz, 8, SCONV_WIDTH), F32)],
        grid=(bsz, nt),
        in_specs=[seg("c_b"), seg("c_c"), seg("c_x"), pl.BlockSpec((1, 8, SCONV_WIDTH), lambda b, c: (b, 0, 0)),
                  pl.BlockSpec(w.shape, lambda b, c: (0, 0))],
        out_specs=[pl.BlockSpec((rows, SCONV_WIDTH), lambda b, c: (b * nt + c, 0)),
                   pl.BlockSpec((1, 8, SCONV_WIDTH), lambda b, c: (b, 0, 0))],
        scratch_shapes=[pltpu.VMEM((rows + 8, SCONV_WIDTH), F32)],
        compiler_params=_cparams("parallel", "arbitrary"), name="sconv_branch")(proj, proj, proj, st0, w)
    return o, stn[:, 8 - (SCONV_K - 1):]


def _rope(y, cos_t, sin_a, sin_b):
    return y * cos_t + pltpu.roll(y, 16, 1) * sin_a + pltpu.roll(y, V7X_LANES - 16, 1) * sin_b


def _mla_prep_kernel(ql_ref, kvl_ref, kpe_ref, cos_ref, sa_ref, sb_ref, gq_ref, wq_ref, wqs_ref, gqh_ref, gqhs_ref,
                     gkv_ref, gkr_ref, q_ref, ckv_ref, kpeo_ref):
    rows = ql_ref.shape[0]
    cos_t, sin_a, sin_b = cos_ref[...], sa_ref[...], sb_ref[...]
    qn = (_row_rms(ql_ref[...]) * gq_ref[...]).astype(MXU_DTYPE)
    q = jnp.dot(qn, wq_ref[...], preferred_element_type=F32)
    q_swap = jnp.dot(qn, wqs_ref[...], preferred_element_type=F32)
    lane = _lane_iota(rows)
    nope = lane < MLA_NOPE
    sin_t = sin_a + sin_b
    scale = (MLA_NOPE + MLA_ROPE) ** -0.5 * math.log2(math.e)
    for h in range(MLA_HEADS):
        sl = slice(h * V7X_LANES, (h + 1) * V7X_LANES)
        blk = q[:, sl]
        sq = blk * blk
        s_n = jnp.sum(jnp.where(nope, sq, 0.0), axis=-1, keepdims=True) * (1.0 / MLA_NOPE)
        s_r = jnp.sum(jnp.where(nope, 0.0, sq), axis=-1, keepdims=True) * (1.0 / MLA_ROPE)
        r = jnp.where(nope, lax.rsqrt(s_n + EPS), lax.rsqrt(s_r + EPS))
        y = blk * r * gqh_ref[:, sl]
        y_swap = q_swap[:, sl] * r * gqhs_ref[:, sl]
        q_ref[:, sl] = ((y * cos_t + y_swap * sin_t) * scale).astype(q_ref.dtype)
    ckv_ref[...] = _row_rms(kvl_ref[...]) * gkv_ref[...]
    kp = kpe_ref[...]
    ms = jnp.sum(kp * kp, axis=-1, keepdims=True) * (1.0 / MLA_ROPE)
    kpeo_ref[...] = _rope(kp * lax.rsqrt(ms + EPS) * gkr_ref[...], cos_t, sin_a, sin_b)


def mla_prep(proj, n, tm, tabs, p):
    cos_t, sin_a, sin_b = tabs
    ntab = cos_t.shape[0] // tm
    seg = lambda name: pl.BlockSpec((tm, SEG[name][1]), lambda i, _j=SEG[name][0] // SEG[name][1]: (i, _j))
    tab = pl.BlockSpec((tm, V7X_LANES), lambda i: (i % ntab, 0))
    full = lambda a: pl.BlockSpec(a.shape, lambda i: (0, 0))
    consts = [p["mla_q_norm_g"], p["mla_wq"], p["mla_wq_swap"], p["mla_gq_head"], p["mla_gq_head_swap"],
              p["mla_kv_norm_g"], p["mla_gkr"]]
    return pl.pallas_call(
        _mla_prep_kernel,
        out_shape=[jax.ShapeDtypeStruct((n, MLA_HEADS * V7X_LANES), MXU_DTYPE),
                   jax.ShapeDtypeStruct((n, MLA_KV_RANK), F32), jax.ShapeDtypeStruct((n, V7X_LANES), F32)],
        grid=(n // tm,),
        in_specs=[seg("b_ql"), seg("b_kvl"), seg("b_kpe"), tab, tab, tab] + [full(a) for a in consts],
        out_specs=[pl.BlockSpec((tm, MLA_HEADS * V7X_LANES), lambda i: (i, 0)),
                   pl.BlockSpec((tm, MLA_KV_RANK), lambda i: (i, 0)), pl.BlockSpec((tm, V7X_LANES), lambda i: (i, 0))],
        compiler_params=_cparams("parallel"), name="mla_prep")(proj, proj, proj, cos_t, sin_a, sin_b, *consts)


def _mla_kv_kernel(ckv_ref, kpe_ref, wk_ref, wvt_ref, gk_ref, k_ref, vt_ref):
    c = ckv_ref[0].astype(MXU_DTYPE)
    kk = jnp.dot(c, wk_ref[...], preferred_element_type=F32)
    vt_ref[0, 0] = _mm_nt(wvt_ref[...], c).astype(vt_ref.dtype)
    kpe = kpe_ref[0]
    for h in range(MLA_HEADS):
        sl = slice(h * V7X_LANES, (h + 1) * V7X_LANES)
        blk = kk[:, sl]
        ms = jnp.sum(blk * blk, axis=-1, keepdims=True) * (1.0 / MLA_NOPE)
        k_ref[0, :, sl] = (blk * lax.rsqrt(ms + EPS) * gk_ref[...] + kpe).astype(k_ref.dtype)


def mla_kv(ckv_all, kpe_all, ts, p):
    bsz, s, _ = ckv_all.shape
    full = lambda a: pl.BlockSpec(a.shape, lambda b, i: (0, 0))
    consts = [p["mla_wk"], p["mla_wvt"], p["mla_gk"]]
    return pl.pallas_call(
        _mla_kv_kernel,
        out_shape=[jax.ShapeDtypeStruct((bsz, s, MLA_HEADS * V7X_LANES), MXU_DTYPE),
                   jax.ShapeDtypeStruct((bsz, s // ts, MLA_WIDTH, ts), MXU_DTYPE)],
        grid=(bsz, s // ts),
        in_specs=[pl.BlockSpec((1, ts, MLA_KV_RANK), lambda b, i: (b, i, 0)),
                  pl.BlockSpec((1, ts, V7X_LANES), lambda b, i: (b, i, 0))] + [full(a) for a in consts],
        out_specs=[pl.BlockSpec((1, ts, MLA_HEADS * V7X_LANES), lambda b, i: (b, i, 0)),
                   pl.BlockSpec((1, 1, MLA_WIDTH, ts), lambda b, i: (b, i, 0, 0))],
        compiler_params=_cparams("parallel", "parallel"), name="mla_kv")(ckv_all, kpe_all, *consts)


def _mla_kv_cached_kernel(ckv_old_ref, ckv_new_ref, kpe_old_ref, kpe_new_ref, wk_ref, wv_ref, gk_ref, k_ref, v_ref):
    c = jnp.concatenate([ckv_old_ref[0, 0], ckv_new_ref[0]], axis=0).astype(MXU_DTYPE)
    old_t = kpe_old_ref[0, 0]
    tail = V7X_LANES - KPE_LANE - MLA_ROPE
    old_t = jnp.concatenate([jnp.zeros((KPE_LANE, old_t.shape[1]), F32), old_t, jnp.zeros((tail, old_t.shape[1]), F32)], axis=0)
    kpe = jnp.concatenate([old_t.T, kpe_new_ref[0]], axis=0)
    kk = jnp.dot(c, wk_ref[...], preferred_element_type=F32)
    v_ref[0] = jnp.dot(c, wv_ref[...], preferred_element_type=F32).astype(v_ref.dtype)
    for h in range(MLA_HEADS):
        sl = slice(h * V7X_LANES, (h + 1) * V7X_LANES)
        blk = kk[:, sl]
        ms = jnp.sum(blk * blk, axis=-1, keepdims=True) * (1.0 / MLA_NOPE)
        k_ref[0, :, sl] = (blk * lax.rsqrt(ms + EPS) * gk_ref[...] + kpe).astype(k_ref.dtype)


def mla_kv_cached(ckv_cache, kpe_cache, layer, ckv_new, kpe_new, p):
    _, bsz, past, _ = ckv_cache.shape
    t = ckv_new.shape[1]
    s = past + t
    full = lambda a: pl.BlockSpec(a.shape, lambda b: (0, 0))
    consts = [p["mla_wk"], p["mla_wv"], p["mla_gk"]]
    return pl.pallas_call(
        _mla_kv_cached_kernel,
        out_shape=[jax.ShapeDtypeStruct((bsz, s, MLA_HEADS * V7X_LANES), MXU_DTYPE),
                   jax.ShapeDtypeStruct((bsz, s, MLA_WIDTH), MXU_DTYPE)],
        grid=(bsz,),
        in_specs=[pl.BlockSpec((1, 1, past, MLA_KV_RANK), lambda b: (layer, b, 0, 0)),
                  pl.BlockSpec((1, t, MLA_KV_RANK), lambda b: (b, 0, 0)),
                  pl.BlockSpec((1, 1, MLA_ROPE, past), lambda b: (layer, b, 0, 0)),
                  pl.BlockSpec((1, t, V7X_LANES), lambda b: (b, 0, 0))] + [full(a) for a in consts],
        out_specs=[pl.BlockSpec((1, s, MLA_HEADS * V7X_LANES), lambda b: (b, 0, 0)),
                   pl.BlockSpec((1, s, MLA_WIDTH), lambda b: (b, 0, 0))],
        compiler_params=_cparams("parallel"), name="mla_kv_cached")(ckv_cache, ckv_new, kpe_cache, kpe_new, *consts)


def _mla_attn_rows_kernel(q_ref, k_ref, v_ref, o_ref, *, q_off):
    t, s = q_ref.shape[1], k_ref.shape[1]
    low = _lane_iota(t) < MLA_V
    needs_mask = (s - 1) // CHUNK > q_off // CHUNK
    if needs_mask:
        q_chunk = (q_off + lax.broadcasted_iota(jnp.int32, (t, s), 0)) // CHUNK
        visible = lax.broadcasted_iota(jnp.int32, (t, s), 1) // CHUNK <= q_chunk
    for hp in range(MLA_HEADS // 2):
        v_pair = v_ref[0, :, hp * V7X_LANES:(hp + 1) * V7X_LANES]
        outs = []
        for sub in range(2):
            sl = slice((2 * hp + sub) * V7X_LANES, (2 * hp + sub + 1) * V7X_LANES)
            sc = _mm_nt(q_ref[0, :, sl], k_ref[0, :, sl])
            if needs_mask:
                sc = jnp.where(visible, sc, NEG_BIG)
            e = jnp.exp2(sc - jnp.max(sc, axis=-1, keepdims=True))
            outs.append(_mm(e, v_pair) / jnp.sum(e, axis=-1, keepdims=True))
        o_ref[0, :, hp * V7X_LANES:(hp + 1) * V7X_LANES] = jnp.where(low, outs[0], outs[1]).astype(o_ref.dtype)


def mla_attention_rows(q, k, v, q_off):
    bsz, t, _ = q.shape
    s = k.shape[1]
    return pl.pallas_call(
        functools.partial(_mla_attn_rows_kernel, q_off=q_off),
        out_shape=jax.ShapeDtypeStruct((bsz, t, MLA_WIDTH), MXU_DTYPE), grid=(bsz,),
        in_specs=[pl.BlockSpec((1, t, MLA_HEADS * V7X_LANES), lambda b: (b, 0, 0)),
                  pl.BlockSpec((1, s, MLA_HEADS * V7X_LANES), lambda b: (b, 0, 0)),
                  pl.BlockSpec((1, s, MLA_WIDTH), lambda b: (b, 0, 0))],
        out_specs=pl.BlockSpec((1, t, MLA_WIDTH), lambda b: (b, 0, 0)),
        compiler_params=_cparams("parallel"), name="mla_attention_rows")(q, k, v)


def _mla_attn_kernel(q_ref, k_ref, vt_ref, ot_ref, sa_s, sb_s, *, tq, tkc, q_off, nh):
    assert tq % (2 * tkc) == 0 and q_off % (2 * tkc) == 0 and tkc % CHUNK == 0
    q_start = q_off + pl.program_id(2) * tq
    n_full = q_start // tkc
    n_own = tq // tkc
    lanes = lambda h: slice(h * V7X_LANES, (h + 1) * V7X_LANES)
    qs = [q_ref[0, :, lanes(h)] for h in range(nh)]
    ones = jnp.ones((8, tkc), MXU_DTYPE)

    def produce(c, buf):
        k0 = pl.multiple_of(c * tkc, tkc)
        for h in range(nh):
            buf[h] = _mm_nt(k_ref[0, pl.ds(k0, tkc), lanes(h)], qs[h])

    def consume(c, buf, stats, own=None):
        if own is not None:
            k_chunk = (own * tkc + lax.broadcasted_iota(jnp.int32, (tkc, tq), 0)) // CHUNK
            visible = k_chunk <= lax.broadcasted_iota(jnp.int32, (tkc, tq), 1) // CHUNK
        out = []
        for h in range(nh):
            m, acc = stats[h]
            s = buf[h] if own is None else jnp.where(visible, buf[h], NEG_BIG)
            m_new = jnp.maximum(m, jnp.max(s, axis=0, keepdims=True))
            alpha = jnp.exp2(m - m_new)
            p = jnp.exp2(s - m_new).astype(MXU_DTYPE)
            v_ext = jnp.concatenate([vt_ref[0, c, h * MLA_V:(h + 1) * MLA_V, :], ones], axis=0)
            out.append((m_new, alpha * acc + jnp.dot(v_ext, p, preferred_element_type=F32)))
        return tuple(out)

    def pair(i, stats):
        c = 2 * i
        produce(c + 1, sb_s)
        stats = consume(c, sa_s, stats)
        produce(c + 2, sa_s)
        return consume(c + 1, sb_s, stats)

    init = tuple((jnp.full((1, tq), NEG_BIG, F32), jnp.zeros((MLA_V + 8, tq), F32)) for _ in range(nh))
    produce(0, sa_s)
    stats = lax.fori_loop(0, n_full // 2, pair, init)
    bufs = (sa_s, sb_s)
    for own in range(n_own):
        if own + 1 < n_own:
            produce(n_full + own + 1, bufs[(own + 1) % 2])
        stats = consume(n_full + own, bufs[own % 2], stats, own)
    for h in range(nh):
        acc = stats[h][1]
        ot_ref[0, h * MLA_V:(h + 1) * MLA_V, :] = (acc[:MLA_V] / acc[MLA_V:MLA_V + 1]).astype(ot_ref.dtype)


def mla_attention(q, k, vt, q_off, tq, nh):
    bsz, t, _ = q.shape
    _, nchunk, _, tkc = vt.shape
    s = k.shape[1]
    kern = functools.partial(_mla_attn_kernel, tq=tq, tkc=tkc, q_off=q_off, nh=nh)
    return pl.pallas_call(
        kern, out_shape=jax.ShapeDtypeStruct((bsz, MLA_WIDTH, t), MXU_DTYPE),
        grid=(bsz, MLA_HEADS // nh, t // tq),
        in_specs=[pl.BlockSpec((1, tq, nh * V7X_LANES), lambda b, h, qi: (b, qi, h)),
                  pl.BlockSpec((1, s, nh * V7X_LANES), lambda b, h, qi: (b, 0, h), pipeline_mode=pl.Buffered(1)),
                  pl.BlockSpec((1, nchunk, nh * MLA_V, tkc), lambda b, h, qi: (b, 0, h, 0),
                               pipeline_mode=pl.Buffered(1))],
        out_specs=pl.BlockSpec((1, nh * MLA_V, tq), lambda b, h, qi: (b, h, qi)),
        scratch_shapes=[pltpu.VMEM((nh, tkc, tq), F32), pltpu.VMEM((nh, tkc, tq), F32)],
        compiler_params=_cparams("parallel", "parallel", "parallel"), name="mla_attention")(q, k, vt)


def _band_kernel(q_ref, ka_ref, kb_ref, va_ref, vb_ref, bias_ref, gq_ref, gk_ref, ot_ref, kn_ref, q_s, k_s, vt_s,
                 *, a_is_cache):
    tb = q_ref.shape[0]
    rows_q = q_s.shape[0]
    pad = rows_q - tb
    qn = _head64_rms(q_ref[...]) * gq_ref[...] * (BAND_HEAD_DIM ** -0.5 * math.log2(math.e))
    kb = _head64_rms(kb_ref[...]) * gk_ref[...]
    kn_ref[...] = kb
    if a_is_cache:
        k_s[0:BAND_ROWS, :] = ka_ref[0, 0].T.astype(k_s.dtype)
        vt_s[:, 0:BAND_ROWS] = va_ref[0, 0].astype(vt_s.dtype)
    else:
        k_s[0:BAND_ROWS, :] = (_head64_rms(ka_ref[...]) * gk_ref[...]).astype(k_s.dtype)
        vt_s[:, 0:BAND_ROWS] = va_ref[...].T.astype(vt_s.dtype)
    vb = vb_ref[...]
    if pad:
        zeros = jnp.zeros((pad, BAND_WIDTH), F32)
        qn, kb, vb = (jnp.concatenate([a, zeros], axis=0) for a in (qn, kb, vb))
    q_s[...] = qn.astype(q_s.dtype)
    k_s[BAND_ROWS:, :] = kb.astype(k_s.dtype)
    vt_s[:, BAND_ROWS:] = vb.T.astype(vt_s.dtype)

    _, win, tw = bias_ref.shape
    low = _lane_iota(tw) < BAND_HEAD_DIM
    ones = jnp.ones((8, win), MXU_DTYPE)
    win_row = lax.broadcasted_iota(jnp.int32, (win, 2 * tw), 0)

    def tiles(first):
        for j in range(rows_q // tw):
            r0 = j * tw
            qj = q_s[r0:r0 + tw, :]
            kw = k_s[r0:r0 + win, :]
            ss = []
            for hp in range(BAND_HEADS // 2):
                sl = slice(hp * V7X_LANES, (hp + 1) * V7X_LANES)
                zero = jnp.zeros((), q_s.dtype)
                q2 = jnp.concatenate([jnp.where(low, qj[:, sl], zero), jnp.where(low, zero, qj[:, sl])], axis=0)
                ss.append(_mm_nt(kw[:, sl], q2))
            for hp in range(BAND_HEADS // 2):
                sl = slice(hp * V7X_LANES, (hp + 1) * V7X_LANES)
                v_ext = jnp.concatenate([vt_s[sl, r0:r0 + win], ones], axis=0)
                s = ss[hp] + jnp.concatenate([bias_ref[2 * hp], bias_ref[2 * hp + 1]], axis=1)
                if first:
                    s = jnp.where(win_row + r0 < BAND_ROWS, NEG_BIG, s)
                p = jnp.exp2(s - jnp.max(s, axis=0, keepdims=True)).astype(MXU_DTYPE)
                res = jnp.dot(v_ext, p, preferred_element_type=F32)
                den = res[V7X_LANES:V7X_LANES + 1]
                halves = [res[0:BAND_HEAD_DIM, 0:tw] / den[:, 0:tw], res[BAND_HEAD_DIM:V7X_LANES, tw:] / den[:, tw:]]
                o_pair = jnp.concatenate(halves, axis=0).astype(ot_ref.dtype)
                if pad:
                    ot_ref[0, sl, :] = o_pair[:, 0:tb]
                else:
                    ot_ref[0, sl, r0:r0 + tw] = o_pair

    if a_is_cache:
        tiles(False)
    else:
        pl.when(pl.program_id(1) == 0)(lambda: tiles(True))
        pl.when(pl.program_id(1) > 0)(lambda: tiles(False))


def band_branch(q_src, k_prev, v_prev, layer, bsz, t, bias, gq, gk):
    a_is_cache = k_prev is not None
    tb = min(t, BAND_ROWS)
    nt = t // tb
    tw = bias.shape[2]
    rows_q = -(-tb // tw) * tw
    col = lambda name: SEG[name][0] // BAND_WIDTH
    cur = lambda name: pl.BlockSpec((tb, BAND_WIDTH), lambda b, i, _j=col(name): (b * nt + i, _j))
    if a_is_cache:
        prev = lambda name: pl.BlockSpec((1, 1, BAND_WIDTH, BAND_ROWS), lambda b, i: (layer, b, 0, 0))
        ka, va = k_prev, v_prev
    else:
        prev = lambda name: pl.BlockSpec((BAND_ROWS, BAND_WIDTH), lambda b, i, _j=col(name): (b * nt + jnp.maximum(i - 1, 0), _j))
        ka, va = q_src, q_src
    full = lambda a: pl.BlockSpec(a.shape, lambda b, i: (0,) * a.ndim)
    kern = functools.partial(_band_kernel, a_is_cache=a_is_cache)
    return pl.pallas_call(
        kern,
        out_shape=[jax.ShapeDtypeStruct((bsz, BAND_WIDTH, t), MXU_DTYPE), jax.ShapeDtypeStruct((bsz * t, BAND_WIDTH), F32)],
        grid=(bsz, nt),
        in_specs=[cur("d_q"), prev("d_k"), cur("d_k"), prev("d_v"), cur("d_v"), full(bias), full(gq), full(gk)],
        out_specs=[pl.BlockSpec((1, BAND_WIDTH, tb), lambda b, i: (b, 0, i)),
                   pl.BlockSpec((tb, BAND_WIDTH), lambda b, i: (b * nt + i, 0))],
        scratch_shapes=[pltpu.VMEM((rows_q, BAND_WIDTH), MXU_DTYPE), pltpu.VMEM((BAND_ROWS + rows_q, BAND_WIDTH), MXU_DTYPE),
                        pltpu.VMEM((BAND_WIDTH, BAND_ROWS + rows_q), MXU_DTYPE)],
        compiler_params=_cparams("parallel", "parallel"), name="band_branch")(
            q_src, ka, q_src, va, q_src, bias, gq, gk)


def _head_blocks(w, per_head, used):
    k = w.shape[0]
    w = w.reshape(k, -1, per_head)[:, :, :used]
    return jnp.pad(w, ((0, 0), (0, 0), (0, V7X_LANES - used))).reshape(k, -1)


def _lane_pad(v, offset, width=V7X_LANES):
    return jnp.pad(v, (offset, width - offset - v.shape[0])).reshape(1, width)


def _layer_params(l, w):
    src = {}
    off = 0
    for name, width in (("a_z", 512), ("a_x", 512), ("a_b", 256), ("a_c", 256), ("a_dt", 8), ("b_ql", 384),
                        ("b_kvl", 256), ("b_kpe", 32), ("c_b", 512), ("c_c", 512), ("c_x", 512),
                        ("d_q", 512), ("d_k", 512), ("d_v", 512), ("gate", 4096)):
        src[name] = (off, width)
        off += width
    w_in_t = w["w_in"][l].T.astype(MXU_DTYPE)
    pieces = []
    for name, (dst, dwidth) in sorted(SEG.items(), key=lambda kv: kv[1][0]):
        s0, sw = src[name]
        lead = KPE_LANE if name == "b_kpe" else 0
        for width in (lead, None, dwidth - sw - lead):
            if width is None:
                pieces.append(w_in_t[s0:s0 + sw])
            elif width:
                pieces.append(jnp.zeros((width, D_MODEL), MXU_DTYPE))
    p = {"w_in_t": jnp.concatenate(pieces, axis=0)}
    cast = lambda a: a.astype(MXU_DTYPE)
    row = lambda a: a.reshape(1, -1)
    p["norm_mix_g"], p["norm_ffn_g"], p["b_gate"] = w["norm_mix_g"][l], w["norm_ffn_g"][l], w["b_gate"][l]
    for name in ("w_a_out", "w_b_out", "w_c_out", "w_d_out", "w_o", "w_ffn_down"):
        p[name] = cast(w[name][l])
    ffn_tk = FF_HIDDEN // 2
    p["w_ffn_up_tiles"] = cast(w["w_ffn_up"][l]).reshape(D_MODEL, 2 * FF_HIDDEN // ffn_tk, ffn_tk).swapaxes(0, 1)
    p["ssd_conv_w"] = w["ssd_conv_w"][l]
    p["ssd_conv_b"] = row(w["ssd_conv_b"][l])
    p["ssd_dt_bias"] = _lane_pad(w["ssd_dt_bias"][l], 0)
    p["ssd_a_log"] = row(jnp.repeat(w["ssd_a_log"][l], SSD_HEAD_DIM))
    p["ssd_d"] = row(jnp.repeat(w["ssd_d"][l], SSD_HEAD_DIM))
    p["ssd_norm_g"] = row(w["ssd_norm_g"][l])
    head_of_lane = jnp.arange(SSD_INNER) // SSD_HEAD_DIM
    p["ssd_expand"] = (jnp.arange(V7X_LANES)[:, None] == head_of_lane[None, :]).astype(F32)
    p["mla_q_norm_g"] = row(w["mla_q_norm_g"][l])
    p["mla_wq"] = cast(_head_blocks(w["mla_w_q_up"][l], MLA_NOPE + MLA_ROPE, MLA_NOPE + MLA_ROPE))
    gqh = jnp.concatenate([w["mla_qn_g"][l], w["mla_qr_g"][l], jnp.zeros((V7X_LANES - MLA_NOPE - MLA_ROPE,), F32)])
    p["mla_gq_head"] = row(jnp.tile(gqh, MLA_HEADS))
    half = MLA_ROPE // 2
    partner = np.concatenate([np.arange(MLA_NOPE + half, MLA_NOPE + MLA_ROPE), np.arange(MLA_NOPE, MLA_NOPE + half)])
    wq_heads = w["mla_w_q_up"][l].reshape(MLA_Q_RANK, MLA_HEADS, MLA_NOPE + MLA_ROPE)
    wq_swap = jnp.pad(wq_heads[:, :, partner], ((0, 0), (0, 0), (MLA_NOPE, V7X_LANES - MLA_NOPE - MLA_ROPE)))
    p["mla_wq_swap"] = cast(wq_swap.reshape(MLA_Q_RANK, MLA_HEADS * V7X_LANES))
    gqh_swap = jnp.pad(w["mla_qr_g"][l][partner - MLA_NOPE], (MLA_NOPE, V7X_LANES - MLA_NOPE - MLA_ROPE))
    p["mla_gq_head_swap"] = row(jnp.tile(gqh_swap, MLA_HEADS))
    p["mla_kv_norm_g"] = row(w["mla_kv_norm_g"][l])
    p["mla_gkr"] = _lane_pad(w["mla_kr_g"][l], KPE_LANE)
    p["mla_wk"] = cast(_head_blocks(w["mla_w_kv_up"][l], MLA_NOPE + MLA_V, MLA_NOPE))
    p["mla_wv"] = cast(w["mla_w_kv_up"][l].reshape(MLA_KV_RANK, MLA_HEADS, MLA_NOPE + MLA_V)[:, :, MLA_NOPE:]
                       .reshape(MLA_KV_RANK, MLA_WIDTH))
    p["mla_wvt"] = p["mla_wv"].T
    p["mla_gk"] = _lane_pad(w["mla_kn_g"][l], 0)
    p["sconv_w"] = w["sconv_w"][l]
    p["band_gq"] = row(jnp.tile(w["band_qn_g"][l], BAND_HEADS))
    p["band_gk"] = row(jnp.tile(w["band_kn_g"][l], BAND_HEADS))
    p["band_rel_bias"] = w["band_rel_bias"][l]
    return p


def _band_bias_table(rel_bias, nqry):
    nkey = BAND_ROWS + nqry
    key = np.arange(nkey)[:, None]
    qry = np.arange(nqry)[None, :]
    in_band = np.logical_and(key // CHUNK >= qry // CHUNK, key // CHUNK <= qry // CHUNK + BAND_PAST_CHUNKS)
    diff = np.arange(nqry - 1 + BAND_ROWS, -nkey + BAND_ROWS, -1)
    by_diff = rel_bias[:, np.clip(diff, -REL_CLIP, REL_CLIP) + REL_CLIP]
    skew = jnp.tile(jnp.pad(by_diff, ((0, 0), (0, 1)))[:, None, :], (1, nqry, 1)).reshape(BAND_HEADS, -1)
    skew = skew[:, :nqry * (nkey + nqry - 1)].reshape(BAND_HEADS, nqry, nkey + nqry - 1)
    bias = jnp.swapaxes(skew[:, :, nqry - 1:], 1, 2) * math.log2(math.e)
    return jnp.where(in_band[None], bias, NEG_BIG)


def _rope_tables(pos, rows):
    inv = ROPE_THETA ** (-jnp.arange(0, MLA_ROPE, 2, dtype=F32) / MLA_ROPE)
    ang = pos.astype(F32)[:, None] * inv[None, :]
    cos, sin = jnp.cos(ang), jnp.sin(ang)
    half = MLA_ROPE // 2
    z = lambda w: jnp.zeros((pos.shape[0], w), F32)
    tail = V7X_LANES - MLA_NOPE - MLA_ROPE
    cos_t = jnp.concatenate([jnp.ones((pos.shape[0], MLA_NOPE), F32), cos, cos, z(tail)], axis=1)
    sin_a = jnp.concatenate([z(MLA_NOPE + half), sin, z(tail)], axis=1)
    sin_b = jnp.concatenate([z(MLA_NOPE), -sin, z(half + tail)], axis=1)
    reps = max(rows // pos.shape[0], 1)
    return tuple(jnp.tile(a, (reps, 1)) for a in (cos_t, sin_a, sin_b))


def _layer(x, bsz, t, past, p, tabs, q_off, layer):
    n = bsz * t
    tm = min(n, 512)
    proj = norm_matmul_nt(x, p["norm_mix_g"], p["w_in_t"], min(n, 1024), 1920)
    ya, ssm_new, ssd_conv_new = ssd_branch(proj, bsz, t, past["ssd_conv"], past["ssm"], p)
    yc, sconv_new = sconv_branch(proj, bsz, t, past["sconv"], p["sconv_w"])
    q, ckv, kpe_pad = mla_prep(proj, n, tm, tabs, p)
    q3, ckv3, kpe3 = q.reshape(bsz, t, -1), ckv.reshape(bsz, t, MLA_KV_RANK), kpe_pad.reshape(bsz, t, V7X_LANES)
    if past["mla_ckv"] is not None:
        assert t <= V7X_LANES
        kk, vv = mla_kv_cached(past["mla_ckv"], past["mla_kpe"], layer, ckv3, kpe3, p)
        ob = mla_attention_rows(q3, kk, vv, q_off).reshape(n, MLA_WIDTH)
    else:
        tkc = next(c for c in (256, 128, t) if t % c == 0)
        kk, vt = mla_kv(ckv3, kpe3, tkc, p)
        ob = mla_attention(q3, kk, vt, q_off, min(t, 512), 8)
    band_bias = _band_bias_table(p["band_rel_bias"], 2 * V7X_LANES if t % (2 * V7X_LANES) == 0 else V7X_LANES)
    od, kn = band_branch(proj, past["band_k"], past["band_v"], layer, bsz, t, band_bias, p["band_gq"], p["band_gk"])
    if ob.ndim != od.ndim or t % tm:
        ob = ob if ob.ndim == 2 else jnp.swapaxes(ob, 1, 2).reshape(n, MLA_WIDTH)
        od = jnp.swapaxes(od, 1, 2).reshape(n, BAND_WIDTH)
    x = merge(x, proj, ya, ob, yc, od, p["b_gate"], p["w_a_out"], p["w_b_out"], p["w_c_out"], p["w_d_out"], p["w_o"], tm)
    x = ffn(x, p["norm_ffn_g"], p["w_ffn_up_tiles"], p["w_ffn_down"], tm)
    dv0 = SEG["d_v"][0]
    keep = min(BAND_ROWS, t)
    heads = lambda a: a.reshape(bsz, keep, BAND_HEADS, BAND_HEAD_DIM)
    kn = heads(kn.reshape(bsz, t, BAND_WIDTH)[:, t - keep:])
    vn = heads(proj.reshape(bsz, t, PROJ_COLS)[:, t - keep:, dv0:dv0 + BAND_WIDTH])
    new = {"ssm": ssm_new, "ssd_conv": ssd_conv_new, "sconv": sconv_new,
           "mla_ckv": ckv.reshape(bsz, t, MLA_KV_RANK),
           "mla_kpe": kpe_pad[:, KPE_LANE:KPE_LANE + MLA_ROPE].reshape(bsz, t, MLA_ROPE),
           "band_k": kn, "band_v": vn}
    return x, new


def kernel(x_prompt, x_sample, state_ssm, state_ssd_conv, cache_mla_ckv, cache_mla_kpe, state_sconv, cache_band_k, cache_band_v, norm_mix_g, w_in, b_gate, ssd_conv_w, ssd_conv_b, ssd_dt_bias, ssd_a_log, ssd_d, ssd_norm_g, w_a_out, mla_q_norm_g, mla_w_q_up, mla_kv_norm_g, mla_w_kv_up, mla_qn_g, mla_kn_g, mla_qr_g, mla_kr_g, w_b_out, sconv_w, w_c_out, band_qn_g, band_kn_g, band_rel_bias, w_d_out, w_o, norm_ffn_g, w_ffn_up, w_ffn_down):
    weights = dict(norm_mix_g=norm_mix_g, w_in=w_in, b_gate=b_gate, ssd_conv_w=ssd_conv_w, ssd_conv_b=ssd_conv_b,
                   ssd_dt_bias=ssd_dt_bias, ssd_a_log=ssd_a_log, ssd_d=ssd_d, ssd_norm_g=ssd_norm_g, w_a_out=w_a_out,
                   mla_q_norm_g=mla_q_norm_g, mla_w_q_up=mla_w_q_up, mla_kv_norm_g=mla_kv_norm_g,
                   mla_w_kv_up=mla_w_kv_up, mla_qn_g=mla_qn_g, mla_kn_g=mla_kn_g, mla_qr_g=mla_qr_g,
                   mla_kr_g=mla_kr_g, w_b_out=w_b_out, sconv_w=sconv_w, w_c_out=w_c_out, band_qn_g=band_qn_g,
                   band_kn_g=band_kn_g, band_rel_bias=band_rel_bias, w_d_out=w_d_out, w_o=w_o,
                   norm_ffn_g=norm_ffn_g, w_ffn_up=w_ffn_up, w_ffn_down=w_ffn_down)
    depth = w_in.shape[0]
    b_p, t_p, d = x_prompt.shape
    b_s, t_s, _ = x_sample.shape
    past_len = cache_mla_ckv.shape[2]
    assert d == D_MODEL and t_p % BAND_ROWS == 0 and t_s == CHUNK and cache_band_k.shape[2] == BAND_ROWS
    assert past_len % CHUNK == 0
    tabs_p = _rope_tables(jnp.arange(t_p, dtype=jnp.int32), min(b_p * t_p, 512))
    tabs_s = _rope_tables(past_len + jnp.arange(t_s, dtype=jnp.int32), min(b_s * t_s, 512))
    y_p = x_prompt.reshape(b_p * t_p, d)
    y_s = x_sample.reshape(b_s * t_s, d)
    new_p, new_s = [], []
    band_k_rows = jnp.transpose(cache_band_k, (0, 1, 3, 4, 2)).reshape(depth, b_s, BAND_WIDTH, BAND_ROWS)
    band_v_rows = jnp.transpose(cache_band_v, (0, 1, 3, 4, 2)).reshape(depth, b_s, BAND_WIDTH, BAND_ROWS)
    kpe_cache_t = jnp.swapaxes(cache_mla_kpe, 2, 3)
    for l in range(depth):
        p = _layer_params(l, weights)
        init_p = {"ssm": jnp.zeros((b_p, SSD_HEADS, SSD_HEAD_DIM, SSD_STATE), F32),
                  "ssd_conv": jnp.zeros((b_p, SSD_CONV - 1, SSD_CONV_DIM), F32),
                  "sconv": jnp.zeros((b_p, SCONV_K - 1, SCONV_WIDTH), F32),
                  "mla_ckv": None, "mla_kpe": None, "band_k": None, "band_v": None}
        y_p, st_p = _layer(y_p, b_p, t_p, init_p, p, tabs_p, 0, l)
        new_p.append(st_p)
        past_s = {"ssm": state_ssm[l], "ssd_conv": state_ssd_conv[l], "sconv": state_sconv[l],
                  "mla_ckv": cache_mla_ckv, "mla_kpe": kpe_cache_t,
                  "band_k": band_k_rows, "band_v": band_v_rows}
        y_s, st_s = _layer(y_s, b_s, t_s, past_s, p, tabs_s, past_len, l)
        new_s.append(st_s)

    def stack(states, name):
        return jnp.stack([s[name] for s in states], axis=0)

    out = [y_p.reshape(b_p, t_p, d), y_s.reshape(b_s, t_s, d)]
    for name in ("ssm", "ssd_conv", "mla_ckv", "mla_kpe", "sconv"):
        out += [stack(new_p, name), stack(new_s, name)]
    for name, cache in (("band_k", cache_band_k), ("band_v", cache_band_v)):
        out += [stack(new_p, name), jnp.concatenate([cache[:, :, t_s:], stack(new_s, name)], axis=2)]
    return tuple(out)
```
